```python
import math
import jax, jax.numpy as jnp
from jax import lax
import numpy as np

D_MODEL = 2048
BATCH = 16
SEQ = 2048
DEPTH = 4

N_GROUPS = 4
GROUP_W = D_MODEL // N_GROUPS
HEAD_DIM = 128
N_HEADS = GROUP_W // HEAD_DIM
SCONV_K = 3
MLA_Q_RANK = 384
MLA_KV_RANK = 256
MLA_NOPE = 128
MLA_ROPE = 64
MLA_V = 128
ROPE_THETA = 10000.0
ATTN_BLOCK = 128
GDN_CONV_K = 4
CHUNK = 64
D_FF = 4 * D_MODEL
EPS = 1e-6
LB_OFFSET_MAX = 4096

SPLITS = (GROUP_W, GROUP_W, GROUP_W,
          MLA_Q_RANK, MLA_KV_RANK, MLA_ROPE,
          GROUP_W, GROUP_W, GROUP_W, GROUP_W, N_HEADS, N_HEADS,
          GROUP_W, GROUP_W, GROUP_W, GROUP_W)
D_IN = sum(SPLITS)

kernel_name = 'hybrid_parallel_heads_conv_mla_gdn_hgrn2'


def rms_norm(x, g):
    xf = x.astype(jnp.float32)
    y = xf * lax.rsqrt(jnp.mean(xf * xf, axis=-1, keepdims=True) + EPS)
    return (y * g.astype(jnp.float32)).astype(x.dtype)


def l2_norm(x):
    return x * lax.rsqrt(jnp.sum(x * x, axis=-1, keepdims=True) + EPS)


def split_cols(proj):
    out, start = [], 0
    for w in SPLITS:
        out.append(proj[..., start:start + w])
        start += w
    return out


def causal_dwconv(x, w):
    k, c = w.shape
    return lax.conv_general_dilated(x, w[:, None, :].astype(x.dtype), window_strides=(1,),
                                    padding=[(k - 1, 0)], dimension_numbers=('NWC', 'WIO', 'NWC'),
                                    feature_group_count=c)


def rope(x, cos, sin):
    half = x.shape[-1] // 2
    xf = x.astype(jnp.float32)
    x1, x2 = xf[..., :half], xf[..., half:]
    return jnp.concatenate([x1 * cos - x2 * sin, x2 * cos + x1 * sin], axis=-1).astype(x.dtype)


def to_chunks(t):
    b, s, h = t.shape[:3]
    t = t.reshape((b, s // CHUNK, CHUNK, h) + t.shape[3:])
    return t.transpose((1, 0, 3, 2) + tuple(range(4, t.ndim)))


def from_chunks(t):
    n, b, h, c, d = t.shape
    return t.transpose(1, 0, 3, 2, 4).reshape(b, n * c, h, d)


def mla_attention(cq, ckv, kr, cos, sin, gq, gkv, w_uq, w_ukv):
    b, s, _ = cq.shape
    q = (rms_norm(cq, gq) @ w_uq).reshape(b, s, N_HEADS, MLA_NOPE + MLA_ROPE)
    kv = (rms_norm(ckv, gkv) @ w_ukv).reshape(b, s, N_HEADS, MLA_NOPE + MLA_V)
    q = jnp.concatenate([q[..., :MLA_NOPE], rope(q[..., MLA_NOPE:], cos[:, :, None], sin[:, :, None])], axis=-1)
    kr = rope(kr, cos, sin)[:, :, None, :]
    k = jnp.concatenate([kv[..., :MLA_NOPE], jnp.broadcast_to(kr, (b, s, N_HEADS, MLA_ROPE))], axis=-1)
    v = kv[..., MLA_NOPE:]
    scale = (MLA_NOPE + MLA_ROPE) ** -0.5
    nb = s // ATTN_BLOCK
    qb = q.reshape(b, nb, ATTN_BLOCK, N_HEADS, -1).transpose(1, 0, 3, 2, 4)
    kpos = jnp.arange(s)

    def block(args):
        qi, i = args
        sc = jnp.einsum('bhqd,bkhd->bhqk', qi, k, preferred_element_type=jnp.float32) * scale
        qpos = i * ATTN_BLOCK + jnp.arange(ATTN_BLOCK)
        sc = jnp.where(kpos[None, :] <= qpos[:, None], sc, -jnp.inf)
        p = jax.nn.softmax(sc, axis=-1).astype(v.dtype)
        return jnp.einsum('bhqk,bkhd->bqhd', p, v)

    o = lax.map(block, (qb, jnp.arange(nb)))
    return o.transpose(1, 0, 2, 3, 4).reshape(b, s, N_HEADS * MLA_V)


def gated_delta_rule(q, k, v, log_a, beta):
    b, s, h, dk = q.shape
    dv = v.shape[-1]
    q = l2_norm(q) * (dk ** -0.5)
    k = l2_norm(k)
    tri = jnp.tril(jnp.ones((CHUNK, CHUNK), dtype=bool))
    strict = jnp.tril(jnp.ones((CHUNK, CHUNK), dtype=bool), k=-1)
    eye = jnp.eye(CHUNK, dtype=jnp.float32)

    def step(state, inp):
        qc, kc, vc, gc, bc = inp
        gcum = jnp.cumsum(gc, axis=-1)
        decay = jnp.exp(jnp.where(tri, gcum[..., :, None] - gcum[..., None, :], -jnp.inf))
        kb = kc * bc[..., None]
        lower = jnp.where(strict, jnp.einsum('bhtd,bhsd->bhts', kb, kc) * decay, 0.0)
        rhs = jnp.concatenate([vc * bc[..., None], kb * jnp.exp(gcum)[..., None]], axis=-1)
        sol = lax.linalg.triangular_solve(eye + lower, rhs, left_side=True, lower=True, unit_diagonal=True)
        u, w = sol[..., :dv], sol[..., dv:]
        v_new = u - jnp.einsum('bhck,bhkv->bhcv', w, state)
        attn = jnp.einsum('bhtd,bhsd->bhts', qc, kc) * decay
        out = (jnp.einsum('bhtk,bhkv->bhtv', qc * jnp.exp(gcum)[..., None], state)
               + jnp.einsum('bhts,bhsv->bhtv', attn, v_new))
        g_last = gcum[..., -1:]
        state = (state * jnp.exp(g_last)[..., None]
                 + jnp.einsum('bhck,bhcv->bhkv', kc * jnp.exp(g_last - gcum)[..., None], v_new))
        return state, out

    state0 = jnp.zeros((b, h, dk, dv), jnp.float32)
    _, o = lax.scan(step, state0, (to_chunks(q), to_chunks(k), to_chunks(v), to_chunks(log_a), to_chunks(beta)))
    return from_chunks(o)


def hgrn2_recurrence(q, k, i, log_f):
    b, s, h, dk = q.shape
    dv = i.shape[-1]
    tri = jnp.tril(jnp.ones((CHUNK, CHUNK), dtype=bool))[:, :, None]

    def step(state, inp):
        qc, kc, ic, gc = inp
        bcum = jnp.cumsum(gc, axis=2)
        dec = jnp.exp(jnp.where(tri, bcum[:, :, :, None, :] - bcum[:, :, None, :, :], -jnp.inf))
        attn = jnp.einsum('bhtd,bhsd,bhtsd->bhts', qc, kc, dec)
        out = (jnp.einsum('bhtk,bhkv->bhtv', qc * jnp.exp(bcum), state)
               + jnp.einsum('bhts,bhsv->bhtv', attn, ic))
        b_last = bcum[:, :, -1:, :]
        state = (state * jnp.exp(b_last[:, :, 0, :])[..., None]
                 + jnp.einsum('bhck,bhcv->bhkv', kc * jnp.exp(b_last - bcum), ic))
        return state, out

    state0 = jnp.zeros((b, h, dk, dv), jnp.float32)
    _, o = lax.scan(step, state0, (to_chunks(q), to_chunks(k), to_chunks(i), to_chunks(log_f)))
    return from_chunks(o)


def setup_inputs(seed: int = 0) -> dict:
    key = jax.random.key(seed)
    ks = jax.random.split(key, 24)
    f32 = jnp.float32
    L = DEPTH

    def nrm(k, shape, scale):
        return jax.random.normal(k, shape, f32) * scale

    def gain(k, shape):
        return 1.0 + 0.02 * jax.random.normal(k, shape, f32)

    x = nrm(ks[0], (BATCH, SEQ, D_MODEL), 1.0)
    positions = (jax.random.randint(ks[1], (BATCH, 1), 0, LB_OFFSET_MAX, dtype=jnp.int32)
                 + jnp.arange(SEQ, dtype=jnp.int32)[None, :])
    dt = jnp.exp(jax.random.uniform(ks[13], (L, N_HEADS), f32, math.log(1e-3), math.log(1e-1)))
    return {
        'x': x,
        'positions': positions,
        'norm1_g': gain(ks[2], (L, D_MODEL)),
        'w_in': nrm(ks[3], (L, D_MODEL, D_IN), D_MODEL ** -0.5),
        'sconv_w': nrm(ks[4], (L, SCONV_K, GROUP_W), SCONV_K ** -0.5),
        'sconv_out_g': gain(ks[5], (L, GROUP_W)),
        'mla_q_g': gain(ks[6], (L, MLA_Q_RANK)),
        'mla_kv_g': gain(ks[7], (L, MLA_KV_RANK)),
        'mla_w_uq': nrm(ks[8], (L, MLA_Q_RANK, N_HEADS * (MLA_NOPE + MLA_ROPE)), MLA_Q_RANK ** -0.5),
        'mla_w_ukv': nrm(ks[9], (L, MLA_KV_RANK, N_HEADS * (MLA_NOPE + MLA_V)), MLA_KV_RANK ** -0.5),
        'mla_out_g': gain(ks[10], (L, GROUP_W)),
        'gdn_conv_w': nrm(ks[11], (L, GDN_CONV_K, 3 * GROUP_W), GDN_CONV_K ** -0.5),
        'gdn_a_log': jnp.log(jax.random.uniform(ks[12], (L, N_HEADS), f32, 1.0, 16.0)),
        'gdn_dt_bias': dt + jnp.log(-jnp.expm1(-dt)),
        'gdn_norm_g': gain(ks[14], (L, HEAD_DIM)),
        'hgrn_lb_logits': nrm(ks[15], (L, GROUP_W), 0.1),
        'hgrn_norm_g': gain(ks[16], (L, HEAD_DIM)),
        'w_o': nrm(ks[17], (L, N_GROUPS * GROUP_W, D_MODEL), (N_GROUPS * GROUP_W) ** -0.5),
        'norm2_g': gain(ks[18], (L, D_MODEL)),
        'w_ff1': nrm(ks[19], (L, D_MODEL, D_FF), D_MODEL ** -0.5),
        'w_ff2': nrm(ks[20], (L, D_FF, D_MODEL), D_FF ** -0.5),
        'final_g': gain(ks[21], (D_MODEL,)),
    }


def reference(x, positions, norm1_g, w_in, sconv_w, sconv_out_g, mla_q_g, mla_kv_g, mla_w_uq, mla_w_ukv,
              mla_out_g, gdn_conv_w, gdn_a_log, gdn_dt_bias, gdn_norm_g, hgrn_lb_logits, hgrn_norm_g,
              w_o, norm2_g, w_ff1, w_ff2, final_g):
    f32 = jnp.float32
    b, s, _ = x.shape
    half = MLA_ROPE // 2
    inv_freq = ROPE_THETA ** (-jnp.arange(half, dtype=f32) / half)
    ang = positions.astype(f32)[..., None] * inv_freq
    cos, sin = jnp.cos(ang), jnp.sin(ang)
    p_lb = jax.nn.softmax(hgrn_lb_logits.astype(f32), axis=0)
    lower_bounds = jnp.cumsum(p_lb, axis=0) - p_lb[0]

    def heads(t):
        return t.astype(f32).reshape(b, s, N_HEADS, HEAD_DIM)

    h = x
    for l in range(DEPTH):
        u = rms_norm(h, norm1_g[l])
        (sc_x, sc_c, sc_b, m_cq, m_ckv, m_kr,
         g_q, g_k, g_v, g_z, g_a, g_b,
         r_q, r_f, r_i, r_z) = split_cols(u @ w_in[l])

        y_sc = rms_norm(sc_b * causal_dwconv(sc_c * sc_x, sconv_w[l]), sconv_out_g[l])

        y_mla = rms_norm(mla_attention(m_cq, m_ckv, m_kr, cos, sin, mla_q_g[l], mla_kv_g[l],
                                       mla_w_uq[l], mla_w_ukv[l]), mla_out_g[l])

        qkv = jax.nn.silu(causal_dwconv(jnp.concatenate([g_q, g_k, g_v], axis=-1), gdn_conv_w[l]))
        qkv = qkv.astype(f32).reshape(b, s, 3, N_HEADS, HEAD_DIM)
        beta = jax.nn.sigmoid(g_b.astype(f32))
        log_a = -jnp.exp(gdn_a_log[l].astype(f32)) * jax.nn.softplus(g_a.astype(f32) + gdn_dt_bias[l].astype(f32))
        o_gdn = gated_delta_rule(qkv[:, :, 0], qkv[:, :, 1], qkv[:, :, 2], log_a, beta)
        y_gdn = (rms_norm(o_gdn, gdn_norm_g[l]) * jax.nn.silu(heads(g_z))).reshape(b, s, GROUP_W).astype(x.dtype)

        lb = lower_bounds[l].reshape(N_HEADS, HEAD_DIM)
        fr = heads(r_f)
        log_f = jnp.log(lb + (1.0 - lb) * jax.nn.sigmoid(fr))
        k_in = (1.0 - lb) * jax.nn.sigmoid(-fr)
        o_hg = hgrn2_recurrence(jax.nn.silu(heads(r_q)), k_in, heads(r_i), log_f)
        y_hg = (rms_norm(o_hg, hgrn_norm_g[l]) * jax.nn.sigmoid(heads(r_z))).reshape(b, s, GROUP_W).astype(x.dtype)

        h = h + jnp.concatenate([y_sc, y_mla, y_gdn, y_hg], axis=-1) @ w_o[l]

        v_ff = rms_norm(h, norm2_g[l])
        h = h + jnp.square(jax.nn.relu(v_ff @ w_ff1[l])) @ w_ff2[l]

    return rms_norm(h, final_g)
```

```python
import functools

import jax
import jax.numpy as jnp
from jax import lax
from jax.experimental import pallas as pl
from jax.experimental.pallas import tpu as pltpu

F32 = jnp.float32
BF16 = jnp.bfloat16
HIGHEST = lax.Precision.HIGHEST

D_MODEL = 2048
GROUP_W = 512
HEAD_DIM = 128
N_HEADS = 4
MLA_Q_RANK = 384
MLA_KV_RANK = 256
MLA_NOPE = 128
MLA_ROPE = 64
ROPE_THETA = 10000.0
CHUNK = 64
D_FF = 4 * D_MODEL
EPS = 1e-6

SEG_W = 512
(SEG_SC_X, SEG_SC_C, SEG_SC_B, SEG_G_Q, SEG_G_K, SEG_G_V, SEG_G_Z,
 SEG_R_Q, SEG_R_F, SEG_R_I, SEG_R_Z, SEG_MLA_Q, SEG_MLA_KV) = range(13)
N_SEG = 13
QK_PAD = 256
LANES = 128
SUBLANES = 8
VMEM_LIMIT = 56 * 1024 * 1024


def _params(sem, vmem=VMEM_LIMIT):
    return pltpu.CompilerParams(dimension_semantics=sem, vmem_limit_bytes=vmem)


def _dot(a, b, precision=None):
    return jnp.dot(a, b, preferred_element_type=F32, precision=precision)


def _dot_nt(a, b, precision=None):
    return lax.dot_general(a, b, (((1,), (1,)), ((), ())), preferred_element_type=F32, precision=precision)


def _dot_tn(a, b, precision=None):
    return lax.dot_general(a, b, (((0,), (0,)), ((), ())), preferred_element_type=F32, precision=precision)


def _rms(x, g):
    return x * lax.rsqrt(jnp.mean(x * x, axis=-1, keepdims=True) + EPS) * g


def _sigmoid(x):
    return jax.nn.sigmoid(x)


def _inproj_kernel(x_ref, g_ref, w_ref, o_ref, xn_ref):
    @pl.when(pl.program_id(1) == 0)
    def _():
        xn_ref[...] = _rms(x_ref[...], g_ref[...]).astype(BF16)

    o_ref[...] = _dot(xn_ref[...], w_ref[...])


def _in_proj(h, g, w):
    n, d = h.shape
    tm = min(1024, n)
    return pl.pallas_call(
        _inproj_kernel,
        grid=(n // tm, N_SEG),
        in_specs=[pl.BlockSpec((tm, d), lambda i, j: (i, 0)),
                  pl.BlockSpec((1, d), lambda i, j: (0, 0)),
                  pl.BlockSpec((None, d, SEG_W), lambda i, j: (j, 0, 0))],
        out_specs=pl.BlockSpec((None, tm, SEG_W), lambda i, j: (j, i, 0)),
        out_shape=jax.ShapeDtypeStruct((N_SEG, n, SEG_W), F32),
        scratch_shapes=[pltpu.VMEM((tm, d), BF16)],
        compiler_params=_params(("parallel", "arbitrary")),
        name="in_proj",
    )(h, g, w)


def _sconv_kernel(x_ref, c_ref, b_ref, w_ref, g_ref, o_ref, zbuf):
    s = pl.program_id(1)
    ts = x_ref.shape[0]

    @pl.when(s == 0)
    def _():
        zbuf[0:SUBLANES, :] = jnp.zeros((SUBLANES, GROUP_W), F32)

    @pl.when(s > 0)
    def _():
        zbuf[0:SUBLANES, :] = zbuf[ts:ts + SUBLANES, :]

    zbuf[SUBLANES:ts + SUBLANES, :] = c_ref[...] * x_ref[...]
    w = w_ref[...]
    y = (w[2:3] * zbuf[SUBLANES:ts + SUBLANES, :]
         + w[1:2] * zbuf[SUBLANES - 1:ts + SUBLANES - 1, :]
         + w[0:1] * zbuf[SUBLANES - 2:ts + SUBLANES - 2, :])
    o_ref[...] = _rms(b_ref[...] * y, g_ref[...]).astype(BF16)


def _seq_tile(s):
    return min(512, s)


def _sconv(proj, w, g, b, s):
    ts = _seq_tile(s)
    ns = s // ts

    def seg(k):
        return pl.BlockSpec((None, ts, SEG_W), lambda bi, si: (k, bi * ns + si, 0))

    return pl.pallas_call(
        _sconv_kernel,
        grid=(b, ns),
        in_specs=[seg(SEG_SC_X), seg(SEG_SC_C), seg(SEG_SC_B),
                  pl.BlockSpec((3, GROUP_W), lambda bi, si: (0, 0)),
                  pl.BlockSpec((1, GROUP_W), lambda bi, si: (0, 0))],
        out_specs=pl.BlockSpec((ts, GROUP_W), lambda bi, si: (bi * ns + si, 0)),
        out_shape=jax.ShapeDtypeStruct((b * s, GROUP_W), BF16),
        scratch_shapes=[pltpu.VMEM((ts + SUBLANES, GROUP_W), F32)],
        compiler_params=_params(("parallel", "arbitrary")),
        name="sconv",
    )(proj, proj, proj, w, g)


def _rope_kernel(pos_ref, invf_ref, cos_ref, sin_ref):
    ang = pos_ref[...] * invf_ref[...]
    lane = lax.broadcasted_iota(jnp.int32, ang.shape, 1)
    half = MLA_ROPE // 2
    cos_ref[...] = jnp.where(lane < MLA_ROPE, jnp.cos(ang), 0.0)
    sn = jnp.sin(ang)
    sin_ref[...] = jnp.where(lane < half, -sn, jnp.where(lane < MLA_ROPE, sn, 0.0))


def _rope_tables(pos_col, invf):
    n = pos_col.shape[0]
    tm = min(2048, n)
    return pl.pallas_call(
        _rope_kernel,
        grid=(n // tm,),
        in_specs=[pl.BlockSpec((tm, 1), lambda i: (i, 0)),
                  pl.BlockSpec((1, LANES), lambda i: (0, 0))],
        out_specs=[pl.BlockSpec((tm, LANES), lambda i: (i, 0))] * 2,
        out_shape=[jax.ShapeDtypeStruct((n, LANES), F32)] * 2,
        compiler_params=_params(("parallel",)),
        name="rope_tables",
    )(pos_col, invf)


def _mla_up_kernel(a_ref, b_ref, cos_ref, sin_ref, gq_ref, gkv_ref,
                   wqa_ref, wqb_ref, wk_ref, wv_ref, q_ref, k_ref, v_ref):
    scale = (MLA_NOPE + MLA_ROPE) ** -0.5
    cos = cos_ref[...]
    sin = sin_ref[...]
    a = a_ref[...]
    cqn = _rms(a[:, :MLA_Q_RANK], gq_ref[...]).astype(BF16)
    qa = _dot(cqn, wqa_ref[...])
    qb = _dot(cqn, wqb_ref[...])
    for h in range(N_HEADS):
        o = h * QK_PAD
        q_ref[:, o:o + LANES] = (qa[:, o:o + LANES] * scale).astype(BF16)
        roped = qa[:, o + LANES:o + QK_PAD] * cos + qb[:, h * LANES:(h + 1) * LANES] * sin
        q_ref[:, o + LANES:o + QK_PAD] = (roped * scale).astype(BF16)
    bb = b_ref[...]
    ckvn = _rms(bb[:, :MLA_KV_RANK], gkv_ref[...]).astype(BF16)
    kk = _dot(ckvn, wk_ref[...])
    v_ref[...] = _dot(ckvn, wv_ref[...]).astype(BF16)
    kr = (bb[:, MLA_KV_RANK:MLA_KV_RANK + LANES] * cos
          + bb[:, MLA_KV_RANK + LANES:MLA_KV_RANK + 2 * LANES] * sin).astype(BF16)
    for h in range(N_HEADS):
        o = h * QK_PAD
        k_ref[:, o:o + LANES] = kk[:, h * LANES:(h + 1) * LANES].astype(BF16)
        k_ref[:, o + LANES:o + QK_PAD] = kr


def _mla_up(proj, cos, sin, gq, gkv, wqa, wqb, wk, wv):
    n = proj.shape[1]
    tm = min(512, n)

    def seg(k):
        return pl.BlockSpec((None, tm, SEG_W), lambda i: (k, i, 0))

    def full(x):
        return pl.BlockSpec(x.shape, lambda i: (0,) * x.ndim)

    def rows(w):
        return pl.BlockSpec((tm, w), lambda i: (i, 0))

    return pl.pallas_call(
        _mla_up_kernel,
        grid=(n // tm,),
        in_specs=[seg(SEG_MLA_Q), seg(SEG_MLA_KV), rows(LANES), rows(LANES),
                  full(gq), full(gkv), full(wqa), full(wqb), full(wk), full(wv)],
        out_specs=[rows(N_HEADS * QK_PAD), rows(N_HEADS * QK_PAD), rows(GROUP_W)],
        out_shape=[jax.ShapeDtypeStruct((n, N_HEADS * QK_PAD), BF16),
                   jax.ShapeDtypeStruct((n, N_HEADS * QK_PAD), BF16),
                   jax.ShapeDtypeStruct((n, GROUP_W), BF16)],
        compiler_params=_params(("parallel",)),
        name="mla_up",
    )(proj, proj, cos, sin, gq, gkv, wqa, wqb, wk, wv)


def _attn_kernel(q_ref, k_ref, v_ref, g_ref, o_ref):
    i = pl.program_id(1)
    tq = q_ref.shape[0]
    row = lax.broadcasted_iota(jnp.int32, (tq, tq), 0)
    col = lax.broadcasted_iota(jnp.int32, (tq, tq), 1)
    causal = col <= row
    outs = []
    for h in range(N_HEADS):
        q = q_ref[:, h * QK_PAD:(h + 1) * QK_PAD]

        def step(j, carry, masked, h=h, q=q):
            m, l, acc = carry
            start = pl.multiple_of(j * tq, tq)
            k = k_ref[pl.ds(start, tq), h * QK_PAD:(h + 1) * QK_PAD]
            v = v_ref[pl.ds(start, tq), h * HEAD_DIM:(h + 1) * HEAD_DIM]
            sc = _dot_nt(q, k)
            if masked:
                sc = jnp.where(causal, sc, -jnp.inf)
            m_new = jnp.maximum(m, jnp.max(sc, axis=-1, keepdims=True))
            alpha = jnp.exp(m - m_new)
            p = jnp.exp(sc - m_new)
            l = alpha * l + jnp.sum(p, axis=-1, keepdims=True)
            acc = alpha * acc + _dot(p.astype(BF16), v)
            return m_new, l, acc

        init = (jnp.full((tq, 1), -jnp.inf, F32), jnp.zeros((tq, 1), F32),
                jnp.zeros((tq, HEAD_DIM), F32))
        carry = lax.fori_loop(0, i, functools.partial(step, masked=False), init)
        _, l, acc = step(i, carry, True)
        outs.append(acc / l)
    o_ref[...] = _rms(jnp.concatenate(outs, axis=-1), g_ref[...]).astype(BF16)


def _attention(q, k, v, g, b, s):
    tq = min(256, s)
    nq = s // tq
    q3 = q.reshape(b, s, N_HEADS * QK_PAD)
    k3 = k.reshape(b, s, N_HEADS * QK_PAD)
    v3 = v.reshape(b, s, GROUP_W)
    out = pl.pallas_call(
        _attn_kernel,
        grid=(b, nq),
        in_specs=[pl.BlockSpec((None, tq, N_HEADS * QK_PAD), lambda bi, i: (bi, i, 0)),
                  pl.BlockSpec((None, s, N_HEADS * QK_PAD), lambda bi, i: (bi, 0, 0)),
                  pl.BlockSpec((None, s, GROUP_W), lambda bi, i: (bi, 0, 0)),
                  pl.BlockSpec((1, GROUP_W), lambda bi, i: (0, 0))],
        out_specs=pl.BlockSpec((None, tq, GROUP_W), lambda bi, i: (bi, i, 0)),
        out_shape=jax.ShapeDtypeStruct((b, s, GROUP_W), BF16),
        compiler_params=_params(("parallel", "arbitrary")),
        name="mla_attention",
    )(q3, k3, v3, g)
    return out.reshape(b * s, GROUP_W)


GDN_CONV_K = 4


def _gdn_prep_kernel(q_ref, k_ref, v_ref, a_ref, w_ref, alog_ref, dtb_ref,
                     qo_ref, ko_ref, vo_ref, go_ref, buf):
    s = pl.program_id(1)
    ts = q_ref.shape[0]
    w = w_ref[...]
    for idx, (src, dst) in enumerate(((q_ref, qo_ref), (k_ref, ko_ref), (v_ref, vo_ref))):
        @pl.when(s == 0)
        def _(idx=idx):
            buf[idx, 0:SUBLANES, :] = jnp.zeros((SUBLANES, GROUP_W), F32)

        @pl.when(s > 0)
        def _(idx=idx):
            buf[idx, 0:SUBLANES, :] = buf[idx, ts:ts + SUBLANES, :]

        buf[idx, SUBLANES:ts + SUBLANES, :] = src[...]
        wi = w[:, idx * GROUP_W:(idx + 1) * GROUP_W]
        y = wi[3:4] * buf[idx, SUBLANES:ts + SUBLANES, :]
        for j in range(1, GDN_CONV_K):
            y = y + wi[3 - j:4 - j] * buf[idx, SUBLANES - j:ts + SUBLANES - j, :]
        y = y * _sigmoid(y)
        if idx < 2:
            for h in range(N_HEADS):
                yh = y[:, h * HEAD_DIM:(h + 1) * HEAD_DIM]
                yh = yh * lax.rsqrt(jnp.sum(yh * yh, axis=-1, keepdims=True) + EPS)
                if idx == 0:
                    yh = yh * (HEAD_DIM ** -0.5)
                dst[:, h * HEAD_DIM:(h + 1) * HEAD_DIM] = yh
        else:
            dst[...] = y
    gx = a_ref[:, MLA_Q_RANK:MLA_Q_RANK + LANES]
    lane = lax.broadcasted_iota(jnp.int32, gx.shape, 1)
    xs = gx + dtb_ref[...]
    softplus = jnp.maximum(xs, 0.0) + jnp.log1p(jnp.exp(-jnp.abs(xs)))
    log_a = -jnp.exp(alog_ref[...]) * softplus
    go_ref[...] = jnp.where(lane < N_HEADS, log_a, _sigmoid(gx))


def _gdn_prep(proj, w, alog, dtb, b, s):
    ts = _seq_tile(s)
    ns = s // ts

    def seg(k):
        return pl.BlockSpec((None, ts, SEG_W), lambda bi, si: (k, bi * ns + si, 0))

    def rows(wd):
        return pl.BlockSpec((ts, wd), lambda bi, si: (bi * ns + si, 0))

    def full(x):
        return pl.BlockSpec(x.shape, lambda bi, si: (0,) * x.ndim)

    n = b * s
    return pl.pallas_call(
        _gdn_prep_kernel,
        grid=(b, ns),
        in_specs=[seg(SEG_G_Q), seg(SEG_G_K), seg(SEG_G_V), seg(SEG_MLA_Q),
                  full(w), full(alog), full(dtb)],
        out_specs=[rows(GROUP_W), rows(GROUP_W), rows(GROUP_W), rows(LANES)],
        out_shape=[jax.ShapeDtypeStruct((n, GROUP_W), F32)] * 3 + [jax.ShapeDtypeStruct((n, LANES), F32)],
        scratch_shapes=[pltpu.VMEM((3, ts + SUBLANES, GROUP_W), F32)],
        compiler_params=_params(("parallel", "arbitrary")),
        name="gdn_prep",
    )(proj, proj, proj, proj, w, alog, dtb)


def _gdn_chunk_kernel(q_ref, k_ref, v_ref, g_ref, z_ref, ng_ref, o_ref, st_ref):
    @pl.when(pl.program_id(1) == 0)
    def _():
        st_ref[...] = jnp.zeros(st_ref.shape, F32)

    c = CHUNK
    n_chunks = q_ref.shape[0] // c
    row = lax.broadcasted_iota(jnp.int32, (c, c), 0)
    col = lax.broadcasted_iota(jnp.int32, (c, c), 1)
    tri = col <= row
    strict = col < row
    trif = tri.astype(F32)
    ng = ng_ref[...]

    def chunk(ci, carry):
        r0 = pl.multiple_of(ci * c, c)
        g = g_ref[pl.ds(r0, c), :]
        for h in range(N_HEADS):
            hs = slice(h * HEAD_DIM, (h + 1) * HEAD_DIM)
            qh = q_ref[pl.ds(r0, c), hs]
            kh = k_ref[pl.ds(r0, c), hs]
            vh = v_ref[pl.ds(r0, c), hs]
            la = g[:, h:h + 1]
            beta = g[:, N_HEADS + h:N_HEADS + h + 1]
            dlt = _dot(trif, jnp.where(strict, la, 0.0), HIGHEST)
            gc = _dot(trif, jnp.broadcast_to(la, (c, HEAD_DIM)), HIGHEST)
            decay = jnp.exp(jnp.where(tri, dlt, -jnp.inf))
            kb = kh * beta
            khb = kh.astype(BF16)
            lower = jnp.where(strict, _dot_nt(kb.astype(BF16), khb) * decay, 0.0)
            eg = jnp.exp(gc)
            x = jnp.concatenate([vh * beta, kb * eg], axis=-1)
            x = x - _dot(lower, x, HIGHEST)
            p = lower
            for _ in range(5):
                p = _dot(p, p, HIGHEST)
                x = x + _dot(p, x, HIGHEST)
            u = x[:, :HEAD_DIM]
            w = x[:, HEAD_DIM:]
            st = st_ref[h]
            stb = st.astype(BF16)
            v_new = u - _dot(w.astype(BF16), stb)
            vnb = v_new.astype(BF16)
            attn = jnp.where(tri, _dot_nt(qh.astype(BF16), khb) * decay, 0.0)
            out = _dot((qh * eg).astype(BF16), stb) + _dot(attn.astype(BF16), vnb)
            gl = gc[c - 1:c, :]
            kd = kh * jnp.exp(gl - gc)
            st_ref[h] = st * jnp.exp(gl) + _dot_tn(kd.astype(BF16), vnb)
            zh = z_ref[pl.ds(r0, c), hs]
            o_ref[pl.ds(r0, c), hs] = (_rms(out, ng) * (zh * _sigmoid(zh))).astype(BF16)
        return carry

    lax.fori_loop(0, n_chunks, chunk, 0)


def _gdn_chunk(q, k, v, g, proj, ng, b, s):
    ts = _seq_tile(s)
    ns = s // ts

    def rows(wd):
        return pl.BlockSpec((ts, wd), lambda bi, si: (bi * ns + si, 0))

    return pl.pallas_call(
        _gdn_chunk_kernel,
        grid=(b, ns),
        in_specs=[rows(GROUP_W), rows(GROUP_W), rows(GROUP_W), rows(LANES),
                  pl.BlockSpec((None, ts, SEG_W), lambda bi, si: (SEG_G_Z, bi * ns + si, 0)),
                  pl.BlockSpec((1, HEAD_DIM), lambda bi, si: (0, 0))],
        out_specs=rows(GROUP_W),
        out_shape=jax.ShapeDtypeStruct((b * s, GROUP_W), BF16),
        scratch_shapes=[pltpu.VMEM((N_HEADS, HEAD_DIM, HEAD_DIM), F32)],
        compiler_params=_params(("parallel", "arbitrary")),
        name="gdn_chunk",
    )(q, k, v, g, proj, ng)


def _hgrn_kernel(q_ref, f_ref, i_ref, z_ref, lbl_ref, ng_ref, o_ref, st_ref, *, layer):
    @pl.when(pl.program_id(1) == 0)
    def _():
        st_ref[...] = jnp.zeros(st_ref.shape, F32)

    c = CHUNK
    n_chunks = q_ref.shape[0] // c
    logits = lbl_ref[...]
    e = jnp.exp(logits - jnp.max(logits, axis=0, keepdims=True))
    p = e / jnp.sum(e, axis=0, keepdims=True)
    lb = jnp.sum(p[0:layer + 1], axis=0, keepdims=True) - p[0:1]
    ng = ng_ref[...]

    tt = lax.broadcasted_iota(jnp.int32, (c, c), 0)
    ss = lax.broadcasted_iota(jnp.int32, (c, c), 1)
    trow = lax.broadcasted_iota(jnp.int32, (c, HEAD_DIM), 0)
    levels = (1, 2, 4, 8, 16, 32)
    pair_masks = [((tt & m) != 0) & ((ss & m) == 0) & ((tt // (2 * m)) == (ss // (2 * m))) for m in levels]
    odd_rows = [(trow & m) != 0 for m in levels]
    eye = tt == ss

    def chunk(ci, carry):
        r0 = pl.multiple_of(ci * c, c)
        for h in range(N_HEADS):
            hs = slice(h * HEAD_DIM, (h + 1) * HEAD_DIM)
            fr = f_ref[pl.ds(r0, c), hs]
            rq = q_ref[pl.ds(r0, c), hs]
            ii = i_ref[pl.ds(r0, c), hs].astype(BF16)
            lbh = lb[:, hs]
            lf = jnp.log(lbh + (1.0 - lbh) * _sigmoid(fr))
            kk = (1.0 - lbh) * _sigmoid(-fr)
            qq = rq * _sigmoid(rq)
            attn = jnp.where(eye, _dot_nt(qq.astype(BF16), kk.astype(BF16)), 0.0)
            cs = lf
            tot = lf
            for m, pmask, odd in zip(levels, pair_masks, odd_rows):
                qm = (qq * jnp.exp(cs)).astype(BF16)
                km = (kk * jnp.exp(tot - cs)).astype(BF16)
                attn = jnp.where(pmask, _dot_nt(qm, km), attn)
                prev = pltpu.roll(tot, m, axis=0)
                nxt = pltpu.roll(tot, c - m, axis=0)
                cs = cs + jnp.where(odd, prev, 0.0)
                tot = tot + jnp.where(odd, prev, nxt)
            st = st_ref[h]
            out = _dot_nt((qq * jnp.exp(cs)).astype(BF16), st.astype(BF16)) + _dot(attn.astype(BF16), ii)
            kd = (kk * jnp.exp(tot - cs)).astype(BF16)
            st_ref[h] = st * jnp.exp(tot[0:1, :]) + _dot_tn(ii, kd)
            zh = z_ref[pl.ds(r0, c), hs]
            o_ref[pl.ds(r0, c), hs] = (_rms(out, ng) * _sigmoid(zh)).astype(BF16)
        return carry

    lax.fori_loop(0, n_chunks, chunk, 0)


def _hgrn(proj, lbl, ng, layer, b, s):
    ts = _seq_tile(s)
    ns = s // ts

    def seg(k):
        return pl.BlockSpec((None, ts, SEG_W), lambda bi, si: (k, bi * ns + si, 0))

    return pl.pallas_call(
        functools.partial(_hgrn_kernel, layer=layer),
        grid=(b, ns),
        in_specs=[seg(SEG_R_Q), seg(SEG_R_F), seg(SEG_R_I), seg(SEG_R_Z),
                  pl.BlockSpec(lbl.shape, lambda bi, si: (0, 0)),
                  pl.BlockSpec((1, HEAD_DIM), lambda bi, si: (0, 0))],
        out_specs=pl.BlockSpec((ts, GROUP_W), lambda bi, si: (bi * ns + si, 0)),
        out_shape=jax.ShapeDtypeStruct((b * s, GROUP_W), BF16),
        scratch_shapes=[pltpu.VMEM((N_HEADS, HEAD_DIM, HEAD_DIM), F32)],
        compiler_params=_params(("parallel", "arbitrary")),
        name="hgrn2",
    )(proj, proj, proj, proj, lbl, ng)


def _outproj_kernel(ya_ref, yb_ref, yc_ref, yd_ref, w_ref, h_ref, o_ref):
    acc = h_ref[...]
    for g, y_ref in enumerate((ya_ref, yb_ref, yc_ref, yd_ref)):
        acc = acc + _dot(y_ref[...], w_ref[g * GROUP_W:(g + 1) * GROUP_W, :])
    o_ref[...] = acc


def _out_proj(ys, w, h):
    n, d = h.shape
    tm = min(1024, n)
    tn = 512
    ysp = pl.BlockSpec((tm, GROUP_W), lambda i, j: (i, 0))
    return pl.pallas_call(
        _outproj_kernel,
        grid=(n // tm, d // tn),
        in_specs=[ysp, ysp, ysp, ysp,
                  pl.BlockSpec((4 * GROUP_W, tn), lambda i, j: (0, j)),
                  pl.BlockSpec((tm, tn), lambda i, j: (i, j))],
        out_specs=pl.BlockSpec((tm, tn), lambda i, j: (i, j)),
        out_shape=jax.ShapeDtypeStruct((n, d), F32),
        compiler_params=_params(("parallel", "arbitrary")),
        name="out_proj",
    )(*ys, w, h)


def _mlp_kernel(x_ref, g_ref, w1_ref, w2_ref, o_ref, xn_ref):
    f = pl.program_id(1)

    @pl.when(f == 0)
    def _():
        x = x_ref[...]
        xn_ref[...] = _rms(x, g_ref[...]).astype(BF16)
        o_ref[...] = x

    a = jnp.maximum(_dot(xn_ref[...], w1_ref[...]), 0.0)
    o_ref[...] += _dot((a * a).astype(BF16), w2_ref[...])


def _mlp(h, g, w1, w2):
    n, d = h.shape
    dff = w1.shape[1]
    tm = min(512, n)
    tf = 512
    return pl.pallas_call(
        _mlp_kernel,
        grid=(n // tm, dff // tf),
        in_specs=[pl.BlockSpec((tm, d), lambda i, f: (i, 0)),
                  pl.BlockSpec((1, d), lambda i, f: (0, 0)),
                  pl.BlockSpec((d, tf), lambda i, f: (0, f)),
                  pl.BlockSpec((tf, d), lambda i, f: (f, 0))],
        out_specs=pl.BlockSpec((tm, d), lambda i, f: (i, 0)),
        out_shape=jax.ShapeDtypeStruct((n, d), F32),
        scratch_shapes=[pltpu.VMEM((tm, d), BF16)],
        compiler_params=_params(("parallel", "arbitrary")),
        name="mlp",
    )(h, g, w1, w2)


def _final_norm_kernel(x_ref, g_ref, o_ref):
    o_ref[...] = _rms(x_ref[...], g_ref[...])


def _final_norm(h, g):
    n, d = h.shape
    tm = min(512, n)
    return pl.pallas_call(
        _final_norm_kernel,
        grid=(n // tm,),
        in_specs=[pl.BlockSpec((tm, d), lambda i: (i, 0)), pl.BlockSpec((1, d), lambda i: (0, 0))],
        out_specs=pl.BlockSpec((tm, d), lambda i: (i, 0)),
        out_shape=jax.ShapeDtypeStruct((n, d), F32),
        compiler_params=_params(("parallel",)),
        name="final_norm",
    )(h, g)


def _pack_w_in(w_in):
    widths = (GROUP_W, GROUP_W, GROUP_W, MLA_Q_RANK, MLA_KV_RANK, MLA_ROPE,
              GROUP_W, GROUP_W, GROUP_W, GROUP_W, N_HEADS, N_HEADS,
              GROUP_W, GROUP_W, GROUP_W, GROUP_W)
    cols, start = [], 0
    for wd in widths:
        cols.append(w_in[..., start:start + wd])
        start += wd
    (sc_x, sc_c, sc_b, m_cq, m_ckv, m_kr, g_q, g_k, g_v, g_z, g_a, g_b, r_q, r_f, r_i, r_z) = cols
    lead = w_in.shape[:-1]

    def zeros(wd):
        return jnp.zeros(lead + (wd,), w_in.dtype)

    half = MLA_ROPE // 2
    kr_swapped = jnp.concatenate([m_kr[..., half:], m_kr[..., :half]], axis=-1)
    seg_q = jnp.concatenate([m_cq, g_a, g_b, zeros(SEG_W - MLA_Q_RANK - 2 * N_HEADS)], axis=-1)
    seg_kv = jnp.concatenate([m_ckv, m_kr, zeros(LANES - MLA_ROPE), kr_swapped, zeros(LANES - MLA_ROPE)], axis=-1)
    segs = [sc_x, sc_c, sc_b, g_q, g_k, g_v, g_z, r_q, r_f, r_i, r_z, seg_q, seg_kv]
    return jnp.stack(segs, axis=1).astype(BF16)


def _pack_mla_weights(w_uq, w_ukv):
    l = w_uq.shape[0]
    half = MLA_ROPE // 2
    wq = w_uq.reshape(l, MLA_Q_RANK, N_HEADS, MLA_NOPE + MLA_ROPE)
    nope, rp = wq[..., :MLA_NOPE], wq[..., MLA_NOPE:]
    z64 = jnp.zeros(rp.shape, rp.dtype)
    wqa = jnp.concatenate([nope, rp, z64], axis=-1).reshape(l, MLA_Q_RANK, N_HEADS * QK_PAD)
    rp_sw = jnp.concatenate([rp[..., half:], rp[..., :half]], axis=-1)
    wqb = jnp.concatenate([rp_sw, z64], axis=-1).reshape(l, MLA_Q_RANK, N_HEADS * LANES)
    wkv = w_ukv.reshape(l, MLA_KV_RANK, N_HEADS, MLA_NOPE + HEAD_DIM)
    wk = wkv[..., :MLA_NOPE].reshape(l, MLA_KV_RANK, N_HEADS * MLA_NOPE)
    wv = wkv[..., MLA_NOPE:].reshape(l, MLA_KV_RANK, N_HEADS * HEAD_DIM)
    return wqa.astype(BF16), wqb.astype(BF16), wk.astype(BF16), wv.astype(BF16)


def _pad_lanes(x):
    l, k = x.shape
    return jnp.concatenate([x, jnp.zeros((l, LANES - k), x.dtype)], axis=-1).reshape(l, 1, LANES)


def kernel(x, positions, norm1_g, w_in, sconv_w, sconv_out_g, mla_q_g, mla_kv_g, mla_w_uq, mla_w_ukv,
           mla_out_g, gdn_conv_w, gdn_a_log, gdn_dt_bias, gdn_norm_g, hgrn_lb_logits, hgrn_norm_g,
           w_o, norm2_g, w_ff1, w_ff2, final_g):
    b, s, d = x.shape
    n = b * s
    depth = w_in.shape[0]

    w_in_p = _pack_w_in(w_in)
    wqa, wqb, wk, wv = _pack_mla_weights(mla_w_uq, mla_w_ukv)
    w_o_b = w_o.astype(BF16)
    w1_b = w_ff1.astype(BF16)
    w2_b = w_ff2.astype(BF16)
    alog_p = _pad_lanes(gdn_a_log)
    dtb_p = _pad_lanes(gdn_dt_bias)

    half = MLA_ROPE // 2
    inv_freq = ROPE_THETA ** (-jnp.arange(half, dtype=F32) / half)
    invf = jnp.concatenate([inv_freq, inv_freq, jnp.zeros((LANES - MLA_ROPE,), F32)]).reshape(1, LANES)
    cos, sin = _rope_tables(positions.astype(F32).reshape(n, 1), invf)

    h = x.reshape(n, d)
    for l in range(depth):
        proj = _in_proj(h, norm1_g[l].reshape(1, d), w_in_p[l])
        y_sc = _sconv(proj, sconv_w[l], sconv_out_g[l].reshape(1, GROUP_W), b, s)
        q, k, v = _mla_up(proj, cos, sin, mla_q_g[l].reshape(1, -1), mla_kv_g[l].reshape(1, -1),
                          wqa[l], wqb[l], wk[l], wv[l])
        y_mla = _attention(q, k, v, mla_out_g[l].reshape(1, GROUP_W), b, s)
        gq, gk, gv, gg = _gdn_prep(proj, gdn_conv_w[l], alog_p[l], dtb_p[l], b, s)
        y_gdn = _gdn_chunk(gq, gk, gv, gg, proj, gdn_norm_g[l].reshape(1, HEAD_DIM), b, s)
        y_hg = _hgrn(proj, hgrn_lb_logits, hgrn_norm_g[l].reshape(1, HEAD_DIM), l, b, s)
        h = _out_proj((y_sc, y_mla, y_gdn, y_hg), w_o_b[l], h)
        h = _mlp(h, norm2_g[l].reshape(1, d), w1_b[l], w2_b[l])
    return _final_norm(h, final_g.reshape(1, d)).reshape(b, s, d)
```

```python
import functools

import jax
import jax.numpy as jnp
from jax import lax
from jax.experimental import pallas as pl
from jax.experimental.pallas import tpu as pltpu

F32 = jnp.float32
BF16 = jnp.bfloat16
HIGHEST = lax.Precision.HIGHEST

D_MODEL = 2048
GROUP_W = 512
HEAD_DIM = 128
N_HEADS = 4
MLA_Q_RANK = 384
MLA_KV_RANK = 256
MLA_NOPE = 128
MLA_ROPE = 64
ROPE_THETA = 10000.0
CHUNK = 64
D_FF = 4 * D_MODEL
EPS = 1e-6

SEG_W = 512
(SEG_SC_X, SEG_SC_C, SEG_SC_B, SEG_G_Q, SEG_G_K, SEG_G_V, SEG_G_Z,
 SEG_R_Q, SEG_R_F, SEG_R_I, SEG_R_Z, SEG_MLA_Q, SEG_MLA_KV) = range(13)
N_SEG = 13
QK_PAD = 256
LANES = 128
SUBLANES = 8
VMEM_LIMIT = 56 * 1024 * 1024


def _params(sem, vmem=VMEM_LIMIT):
    return pltpu.CompilerParams(dimension_semantics=sem, vmem_limit_bytes=vmem)


def _dot(a, b, precision=None):
    return jnp.dot(a, b, preferred_element_type=F32, precision=precision)


def _dot_nt(a, b, precision=None):
    return lax.dot_general(a, b, (((1,), (1,)), ((), ())), preferred_element_type=F32, precision=precision)


def _dot_tn(a, b, precision=None):
    return lax.dot_general(a, b, (((0,), (0,)), ((), ())), preferred_element_type=F32, precision=precision)


def _rms(x, g):
    return x * lax.rsqrt(jnp.mean(x * x, axis=-1, keepdims=True) + EPS) * g


def _sigmoid(x):
    return jax.nn.sigmoid(x)


def _inproj_kernel(x_ref, g_ref, w_ref, o_ref, xn_ref):
    @pl.when(pl.program_id(1) == 0)
    def _():
        xn_ref[...] = _rms(x_ref[...], g_ref[...]).astype(BF16)

    o_ref[...] = _dot(xn_ref[...], w_ref[...])


def _in_proj(h, g, w):
    n, d = h.shape
    tm = min(1024, n)
    return pl.pallas_call(
        _inproj_kernel,
        grid=(n // tm, N_SEG),
        in_specs=[pl.BlockSpec((tm, d), lambda i, j: (i, 0)),
                  pl.BlockSpec((1, d), lambda i, j: (0, 0)),
                  pl.BlockSpec((None, d, SEG_W), lambda i, j: (j, 0, 0))],
        out_specs=pl.BlockSpec((None, tm, SEG_W), lambda i, j: (j, i, 0)),
        out_shape=jax.ShapeDtypeStruct((N_SEG, n, SEG_W), F32),
        scratch_shapes=[pltpu.VMEM((tm, d), BF16)],
        compiler_params=_params(("parallel", "arbitrary")),
        name="in_proj",
    )(h, g, w)


def _sconv_kernel(x_ref, c_ref, b_ref, w_ref, g_ref, o_ref, zbuf):
    s = pl.program_id(1)
    ts = x_ref.shape[0]

    @pl.when(s == 0)
    def _():
        zbuf[0:SUBLANES, :] = jnp.zeros((SUBLANES, GROUP_W), F32)

    @pl.when(s > 0)
    def _():
        zbuf[0:SUBLANES, :] = zbuf[ts:ts + SUBLANES, :]

    zbuf[SUBLANES:ts + SUBLANES, :] = c_ref[...] * x_ref[...]
    w = w_ref[...]
    y = (w[2:3] * zbuf[SUBLANES:ts + SUBLANES, :]
         + w[1:2] * zbuf[SUBLANES - 1:ts + SUBLANES - 1, :]
         + w[0:1] * zbuf[SUBLANES - 2:ts + SUBLANES - 2, :])
    o_ref[...] = _rms(b_ref[...] * y, g_ref[...]).astype(BF16)


def _seq_tile(s):
    return min(512, s)


def _sconv(proj, w, g, b, s):
    ts = _seq_tile(s)
    ns = s // ts

    def seg(k):
        return pl.BlockSpec((None, ts, SEG_W), lambda bi, si: (k, bi * ns + si, 0))

    return pl.pallas_call(
        _sconv_kernel,
        grid=(b, ns),
        in_specs=[seg(SEG_SC_X), seg(SEG_SC_C), seg(SEG_SC_B),
                  pl.BlockSpec((3, GROUP_W), lambda bi, si: (0, 0)),
                  pl.BlockSpec((1, GROUP_W), lambda bi, si: (0, 0))],
        out_specs=pl.BlockSpec((ts, GROUP_W), lambda bi, si: (bi * ns + si, 0)),
        out_shape=jax.ShapeDtypeStruct((b * s, GROUP_W), BF16),
        scratch_shapes=[pltpu.VMEM((ts + SUBLANES, GROUP_W), F32)],
        compiler_params=_params(("parallel", "arbitrary")),
        name="sconv",
    )(proj, proj, proj, w, g)


def _rope_kernel(pos_ref, invf_ref, cos_ref, sin_ref):
    ang = pos_ref[...] * invf_ref[...]
    lane = lax.broadcasted_iota(jnp.int32, ang.shape, 1)
    half = MLA_ROPE // 2
    cos_ref[...] = jnp.where(lane < MLA_ROPE, jnp.cos(ang), 0.0)
    sn = jnp.sin(ang)
    sin_ref[...] = jnp.where(lane < half, -sn, jnp.where(lane < MLA_ROPE, sn, 0.0))


def _rope_tables(pos_col, invf):
    n = pos_col.shape[0]
    tm = min(2048, n)
    return pl.pallas_call(
        _rope_kernel,
        grid=(n // tm,),
        in_specs=[pl.BlockSpec((tm, 1), lambda i: (i, 0)),
                  pl.BlockSpec((1, LANES), lambda i: (0, 0))],
        out_specs=[pl.BlockSpec((tm, LANES), lambda i: (i, 0))] * 2,
        out_shape=[jax.ShapeDtypeStruct((n, LANES), F32)] * 2,
        compiler_params=_params(("parallel",)),
        name="rope_tables",
    )(pos_col, invf)


def _mla_up_kernel(a_ref, b_ref, cos_ref, sin_ref, gq_ref, gkv_ref,
                   wqa_ref, wqb_ref, wk_ref, wvt_ref, q_ref, k_ref, vt_ref):
    scale = (MLA_NOPE + MLA_ROPE) ** -0.5
    cos = cos_ref[...]
    sin = sin_ref[...]
    a = a_ref[...]
    cqn = _rms(a[:, :MLA_Q_RANK], gq_ref[...]).astype(BF16)
    qa = _dot(cqn, wqa_ref[...])
    qb = _dot(cqn, wqb_ref[...])
    for h in range(N_HEADS):
        o = h * QK_PAD
        q_ref[:, o:o + LANES] = (qa[:, o:o + LANES] * scale).astype(BF16)
        roped = qa[:, o + LANES:o + QK_PAD] * cos + qb[:, h * LANES:(h + 1) * LANES] * sin
        q_ref[:, o + LANES:o + QK_PAD] = (roped * scale).astype(BF16)
    bb = b_ref[...]
    ckvn = _rms(bb[:, :MLA_KV_RANK], gkv_ref[...]).astype(BF16)
    kk = _dot(ckvn, wk_ref[...])
    vt_ref[...] = _dot_nt(wvt_ref[...], ckvn).astype(BF16)
    kr = (bb[:, MLA_KV_RANK:MLA_KV_RANK + LANES] * cos
          + bb[:, MLA_KV_RANK + LANES:MLA_KV_RANK + 2 * LANES] * sin).astype(BF16)
    for h in range(N_HEADS):
        o = h * QK_PAD
        k_ref[:, o:o + LANES] = kk[:, h * LANES:(h + 1) * LANES].astype(BF16)
        k_ref[:, o + LANES:o + QK_PAD] = kr


def _mla_up(proj, cos, sin, gq, gkv, wqa, wqb, wk, wvt, b, s):
    n = proj.shape[1]
    tm = min(512, s)
    ns = s // tm

    def seg(k):
        return pl.BlockSpec((None, tm, SEG_W), lambda i: (k, i, 0))

    def full(x):
        return pl.BlockSpec(x.shape, lambda i: (0,) * x.ndim)

    def rows(w):
        return pl.BlockSpec((tm, w), lambda i: (i, 0))

    return pl.pallas_call(
        _mla_up_kernel,
        grid=(n // tm,),
        in_specs=[seg(SEG_MLA_Q), seg(SEG_MLA_KV), rows(LANES), rows(LANES),
                  full(gq), full(gkv), full(wqa), full(wqb), full(wk), full(wvt)],
        out_specs=[rows(N_HEADS * QK_PAD), rows(N_HEADS * QK_PAD),
                   pl.BlockSpec((None, GROUP_W, tm), lambda i: (i // ns, 0, i % ns))],
        out_shape=[jax.ShapeDtypeStruct((n, N_HEADS * QK_PAD), BF16),
                   jax.ShapeDtypeStruct((n, N_HEADS * QK_PAD), BF16),
                   jax.ShapeDtypeStruct((b, GROUP_W, s), BF16)],
        compiler_params=_params(("parallel",)),
        name="mla_up",
    )(proj, proj, cos, sin, gq, gkv, wqa, wqb, wk, wvt)


def _attn_kernel(q_ref, k_ref, vt_ref, g_ref, o_ref, acc_ref):
    i = pl.program_id(1)
    tq = q_ref.shape[0]
    kv_i = lax.broadcasted_iota(jnp.int32, (tq, tq), 0)
    q_i = lax.broadcasted_iota(jnp.int32, (tq, tq), 1)
    causal = kv_i <= q_i
    heads = range(N_HEADS)

    def block(j, carry, masked):
        start = pl.multiple_of(j * tq, tq)
        sc = [_dot_nt(k_ref[pl.ds(start, tq), h * QK_PAD:(h + 1) * QK_PAD],
                      q_ref[:, h * QK_PAD:(h + 1) * QK_PAD]) for h in heads]
        if masked:
            sc = [jnp.where(causal, x, -jnp.inf) for x in sc]
        m_old = carry[:N_HEADS]
        l_old = carry[N_HEADS:]
        m_new = [jnp.maximum(m, jnp.max(x, axis=0, keepdims=True)) for m, x in zip(m_old, sc)]
        alpha = [jnp.exp(mo - mn) for mo, mn in zip(m_old, m_new)]
        p = [jnp.exp(x - mn) for x, mn in zip(sc, m_new)]
        l_new = [a * l + jnp.sum(x, axis=0, keepdims=True) for a, l, x in zip(alpha, l_old, p)]
        pv = [_dot(vt_ref[h * HEAD_DIM:(h + 1) * HEAD_DIM, pl.ds(start, tq)], x.astype(BF16))
              for h, x in zip(heads, p)]
        for h in heads:
            acc_ref[h] = alpha[h] * acc_ref[h] + pv[h]
        return tuple(m_new) + tuple(l_new)

    acc_ref[...] = jnp.zeros(acc_ref.shape, F32)
    init = (jnp.full((1, tq), -jnp.inf, F32),) * N_HEADS + (jnp.zeros((1, tq), F32),) * N_HEADS
    carry = lax.fori_loop(0, i, functools.partial(block, masked=False), init)
    carry = block(i, carry, True)
    ot = jnp.concatenate([acc_ref[h] / carry[N_HEADS + h] for h in heads], axis=0)
    ot = ot * lax.rsqrt(jnp.mean(ot * ot, axis=0, keepdims=True) + EPS)
    o_ref[...] = (ot.T * g_ref[...]).astype(BF16)


def _attention(q, k, vt, g, b, s):
    tq = min(256, s)
    nq = s // tq
    q3 = q.reshape(b, s, N_HEADS * QK_PAD)
    k3 = k.reshape(b, s, N_HEADS * QK_PAD)
    out = pl.pallas_call(
        _attn_kernel,
        grid=(b, nq),
        in_specs=[pl.BlockSpec((None, tq, N_HEADS * QK_PAD), lambda bi, i: (bi, i, 0)),
                  pl.BlockSpec((None, s, N_HEADS * QK_PAD), lambda bi, i: (bi, 0, 0)),
                  pl.BlockSpec((None, GROUP_W, s), lambda bi, i: (bi, 0, 0)),
                  pl.BlockSpec((1, GROUP_W), lambda bi, i: (0, 0))],
        out_specs=pl.BlockSpec((None, tq, GROUP_W), lambda bi, i: (bi, i, 0)),
        out_shape=jax.ShapeDtypeStruct((b, s, GROUP_W), BF16),
        scratch_shapes=[pltpu.VMEM((N_HEADS, HEAD_DIM, tq), F32)],
        compiler_params=_params(("parallel", "arbitrary")),
        name="mla_attention",
    )(q3, k3, vt, g)
    return out.reshape(b * s, GROUP_W)


GDN_CONV_K = 4


def _gdn_prep_kernel(q_ref, k_ref, v_ref, a_ref, w_ref, alog_ref, dtb_ref,
                     qo_ref, ko_ref, kbo_ref, vbo_ref, la128_ref, la64_ref, buf):
    s = pl.program_id(1)
    ts = q_ref.shape[0]
    w = w_ref[...]
    gx = a_ref[:, MLA_Q_RANK:MLA_Q_RANK + LANES]
    xs = gx + dtb_ref[...]
    softplus = jnp.maximum(xs, 0.0) + jnp.log1p(jnp.exp(-jnp.abs(xs)))
    log_a = -jnp.exp(alog_ref[...]) * softplus
    beta = _sigmoid(gx)
    lane = lax.broadcasted_iota(jnp.int32, (ts, LANES), 1)
    la_b = [jnp.broadcast_to(log_a[:, h:h + 1], (ts, LANES)) for h in range(N_HEADS)]
    beta_b = [jnp.broadcast_to(beta[:, N_HEADS + h:N_HEADS + h + 1], (ts, HEAD_DIM)) for h in range(N_HEADS)]
    for h in range(N_HEADS):
        la128_ref[:, h * HEAD_DIM:(h + 1) * HEAD_DIM] = la_b[h]
    for pr in range(N_HEADS // 2):
        la64_ref[:, pr * LANES:(pr + 1) * LANES] = jnp.where(lane < CHUNK, la_b[2 * pr], la_b[2 * pr + 1])
    for idx, src in enumerate((q_ref, k_ref, v_ref)):
        @pl.when(s == 0)
        def _(idx=idx):
            buf[idx, 0:SUBLANES, :] = jnp.zeros((SUBLANES, GROUP_W), F32)

        @pl.when(s > 0)
        def _(idx=idx):
            buf[idx, 0:SUBLANES, :] = buf[idx, ts:ts + SUBLANES, :]

        buf[idx, SUBLANES:ts + SUBLANES, :] = src[...]
        wi = w[:, idx * GROUP_W:(idx + 1) * GROUP_W]
        y = wi[3:4] * buf[idx, SUBLANES:ts + SUBLANES, :]
        for j in range(1, GDN_CONV_K):
            y = y + wi[3 - j:4 - j] * buf[idx, SUBLANES - j:ts + SUBLANES - j, :]
        y = y * _sigmoid(y)
        for h in range(N_HEADS):
            hs = slice(h * HEAD_DIM, (h + 1) * HEAD_DIM)
            yh = y[:, hs]
            if idx < 2:
                yh = yh * lax.rsqrt(jnp.sum(yh * yh, axis=-1, keepdims=True) + EPS)
            if idx == 0:
                qo_ref[:, hs] = (yh * (HEAD_DIM ** -0.5)).astype(BF16)
            elif idx == 1:
                ko_ref[:, hs] = yh.astype(BF16)
                kbo_ref[:, hs] = yh * beta_b[h]
            else:
                vbo_ref[:, hs] = yh * beta_b[h]


def _gdn_prep(proj, w, alog, dtb, b, s):
    ts = _seq_tile(s)
    ns = s // ts

    def seg(k):
        return pl.BlockSpec((None, ts, SEG_W), lambda bi, si: (k, bi * ns + si, 0))

    def rows(wd):
        return pl.BlockSpec((ts, wd), lambda bi, si: (bi * ns + si, 0))

    def full(x):
        return pl.BlockSpec(x.shape, lambda bi, si: (0,) * x.ndim)

    n = b * s
    return pl.pallas_call(
        _gdn_prep_kernel,
        grid=(b, ns),
        in_specs=[seg(SEG_G_Q), seg(SEG_G_K), seg(SEG_G_V), seg(SEG_MLA_Q),
                  full(w), full(alog), full(dtb)],
        out_specs=[rows(GROUP_W)] * 5 + [rows(N_HEADS * CHUNK)],
        out_shape=[jax.ShapeDtypeStruct((n, GROUP_W), BF16)] * 2 + [jax.ShapeDtypeStruct((n, GROUP_W), F32)] * 3
        + [jax.ShapeDtypeStruct((n, N_HEADS * CHUNK), F32)],
        scratch_shapes=[pltpu.VMEM((3, ts + SUBLANES, GROUP_W), F32)],
        compiler_params=_params(("parallel", "arbitrary")),
        name="gdn_prep",
    )(proj, proj, proj, proj, w, alog, dtb)


GDN_GROUP = 4


def _split3(x):
    hi = x.astype(BF16)
    r1 = x - hi.astype(F32)
    mid = r1.astype(BF16)
    lo = (r1 - mid.astype(F32)).astype(BF16)
    return jnp.concatenate([hi, mid, lo], axis=0)


def _gdn_chunk_kernel(q_ref, k_ref, kb_ref, vb_ref, la128_ref, la64_ref, z_ref, ng_ref, o_ref,
                      st_ref, l_s, at_s, rhs_s, wq_s, u_s, kd_s, egl_s):
    @pl.when(pl.program_id(1) == 0)
    def _():
        st_ref[...] = jnp.zeros(st_ref.shape, F32)

    c = CHUNK
    n_chunks = q_ref.shape[0] // c
    pw = N_HEADS * c
    row = lax.broadcasted_iota(jnp.int32, (c, pw), 0)
    col = lax.broadcasted_iota(jnp.int32, (c, pw), 1) % c
    tri = col <= row
    strict = col < row
    eye_f = (col == row).astype(F32)
    diag8 = (col // SUBLANES) == (row // SUBLANES)
    merge_levels = (8, 16, 32)
    merge_masks = [((row & m) != 0) & ((col & m) == 0) & ((row // (2 * m)) == (col // (2 * m)))
                   for m in merge_levels]
    lane_head_p = lax.broadcasted_iota(jnp.int32, (c, pw), 1) // c
    head_sel_p = [(lane_head_p == h).astype(BF16) for h in range(N_HEADS)]
    lane_head_w = lax.broadcasted_iota(jnp.int32, (c, GROUP_W), 1) // HEAD_DIM
    head_sel_w = [(lane_head_w == h).astype(BF16) for h in range(N_HEADS)]
    row3 = lax.broadcasted_iota(jnp.int32, (c, 3 * c), 0)
    col3 = lax.broadcasted_iota(jnp.int32, (c, 3 * c), 1)
    tri3 = ((col3 % c) <= row3).astype(BF16)
    zeros_half = jnp.zeros((c, 2 * HEAD_DIM), BF16)
    ng = ng_ref[...]

    def block_diag(x_b):
        return jnp.concatenate([x_b * head_sel_p[h] for h in range(N_HEADS)], axis=0)

    def pad_rows(y_b, second):
        z = zeros_half[:, :y_b.shape[1]]
        return jnp.concatenate([z, y_b] if second else [y_b, z], axis=0)

    def prepare(ci, carry):
        r0 = pl.multiple_of(ci * c, c)
        rows_c = pl.ds(r0, c)
        dlt = _dot(tri3, _split3(jnp.where(strict, la64_ref[rows_c, :], 0.0)))
        gc = _dot(tri3, _split3(la128_ref[rows_c, :]))
        decay = jnp.exp(jnp.where(tri, dlt, -jnp.inf))
        eg = jnp.exp(gc)
        kb = kb_ref[rows_c, :]
        k_b = k_ref[rows_c, :]
        q_b = q_ref[rows_c, :]
        k_rows = jnp.concatenate([k_b * head_sel_w[h] for h in range(N_HEADS)], axis=0)
        qk = _dot_nt(jnp.concatenate([kb.astype(BF16), q_b], axis=0), k_rows)
        l_s[ci] = jnp.where(strict, qk[:c] * decay, 0.0).astype(BF16)
        at_s[ci] = jnp.where(tri, qk[c:] * decay, 0.0).astype(BF16)
        kbe = kb * eg
        qg = (q_b.astype(F32) * eg).astype(BF16)
        gl = gc[c - 1:c, :]
        kd = (k_b.astype(F32) * jnp.exp(gl - gc)).astype(BF16)
        vb = vb_ref[rows_c, :]
        for h in range(N_HEADS):
            hs = slice(h * HEAD_DIM, (h + 1) * HEAD_DIM)
            rhs_s[ci, h, :, 0:HEAD_DIM] = vb[:, hs].astype(BF16)
            rhs_s[ci, h, :, HEAD_DIM:2 * HEAD_DIM] = kbe[:, hs].astype(BF16)
            wq_s[ci, h, c:2 * c, :] = qg[:, hs]
            kd_s[ci, h] = kd[:, hs]
        egl_s[ci] = jnp.broadcast_to(jnp.exp(gl), (SUBLANES, GROUP_W))
        return carry

    def invert(ci):
        lf = l_s[ci].astype(F32)
        n8 = jnp.where(diag8, -lf, 0.0)
        t = eye_f + n8
        p = n8.astype(BF16)
        p = _dot(p, block_diag(p)).astype(BF16)
        yield
        pt = _dot(jnp.concatenate([p, t.astype(BF16)], axis=0), block_diag(p))
        t = t + pt[c:]
        p = pt[:c].astype(BF16)
        yield
        t = t + _dot(t.astype(BF16), block_diag(p))
        yield
        for mask in merge_masks:
            off = jnp.where(mask, lf, 0.0).astype(BF16)
            a = _dot(t.astype(BF16), block_diag(off)).astype(BF16)
            yield
            t = t - _dot(a, block_diag(t.astype(BF16)))
            yield
        tb = t.astype(BF16)
        for h in range(N_HEADS):
            pr = slice((h // 2) * LANES, (h // 2 + 1) * LANES)
            x = _dot(tb[:, pr], pad_rows(rhs_s[ci, h], h % 2))
            u_s[ci, h] = x[:, :HEAD_DIM]
            wq_s[ci, h, 0:c, :] = x[:, HEAD_DIM:].astype(BF16)
        yield

    def recur(cis):
        for ci in cis:
            r0 = pl.multiple_of(ci * c, c)
            at = at_s[ci]
            egl = egl_s[ci]
            sts = [st_ref[h] for h in range(N_HEADS)]
            wss = [_dot(wq_s[ci, h], sts[h].astype(BF16)) for h in range(N_HEADS)]
            yield
            vnbs = [(u_s[ci, h] - wss[h][:c]).astype(BF16) for h in range(N_HEADS)]
            for h in range(N_HEADS):
                hs = slice(h * HEAD_DIM, (h + 1) * HEAD_DIM)
                pr = slice((h // 2) * LANES, (h // 2 + 1) * LANES)
                out = wss[h][c:] + _dot(at[:, pr], pad_rows(vnbs[h], h % 2))
                st_ref[h] = sts[h] * egl[0:1, hs] + _dot_tn(kd_s[ci, h], vnbs[h])
                zh = z_ref[pl.ds(r0, c), hs]
                o_ref[pl.ds(r0, c), hs] = (_rms(out, ng) * (zh * _sigmoid(zh))).astype(BF16)
            yield

    def interleave(gens):
        gens = list(gens)
        while gens:
            alive = []
            for g in gens:
                try:
                    next(g)
                    alive.append(g)
                except StopIteration:
                    pass
            gens = alive

    lax.fori_loop(0, n_chunks, prepare, 0)
    interleave(invert(i) for i in range(GDN_GROUP))

    def group(j, carry):
        c0 = j * GDN_GROUP
        interleave([recur([c0 + i for i in range(GDN_GROUP)])]
                   + [invert(c0 + GDN_GROUP + i) for i in range(GDN_GROUP)])
        return carry

    lax.fori_loop(0, n_chunks // GDN_GROUP - 1, group, 0)
    interleave([recur([n_chunks - GDN_GROUP + i for i in range(GDN_GROUP)])])


def _gdn_chunk(q, k, kb, vb, la128, la64, proj, ng, b, s):
    ts = _seq_tile(s)
    ns = s // ts
    nc = ts // CHUNK
    assert nc % GDN_GROUP == 0

    def rows(wd):
        return pl.BlockSpec((ts, wd), lambda bi, si: (bi * ns + si, 0))

    return pl.pallas_call(
        _gdn_chunk_kernel,
        grid=(b, ns),
        in_specs=[rows(GROUP_W)] * 5 + [rows(N_HEADS * CHUNK),
                  pl.BlockSpec((None, ts, SEG_W), lambda bi, si: (SEG_G_Z, bi * ns + si, 0)),
                  pl.BlockSpec((1, HEAD_DIM), lambda bi, si: (0, 0))],
        out_specs=rows(GROUP_W),
        out_shape=jax.ShapeDtypeStruct((b * s, GROUP_W), BF16),
        scratch_shapes=[pltpu.VMEM((N_HEADS, HEAD_DIM, HEAD_DIM), F32),
                        pltpu.VMEM((nc, CHUNK, N_HEADS * CHUNK), BF16),
                        pltpu.VMEM((nc, CHUNK, N_HEADS * CHUNK), BF16),
                        pltpu.VMEM((nc, N_HEADS, CHUNK, 2 * HEAD_DIM), BF16),
                        pltpu.VMEM((nc, N_HEADS, 2 * CHUNK, HEAD_DIM), BF16),
                        pltpu.VMEM((nc, N_HEADS, CHUNK, HEAD_DIM), F32),
                        pltpu.VMEM((nc, N_HEADS, CHUNK, HEAD_DIM), BF16),
                        pltpu.VMEM((nc, SUBLANES, GROUP_W), F32)],
        compiler_params=_params(("parallel", "arbitrary")),
        name="gdn_chunk",
    )(q, k, kb, vb, la128, la64, proj, ng)


def _hgrn_kernel(q_ref, f_ref, i_ref, z_ref, lbl_ref, ng_ref, o_ref, st_ref, *, layer):
    @pl.when(pl.program_id(1) == 0)
    def _():
        st_ref[...] = jnp.zeros(st_ref.shape, F32)

    c = CHUNK
    n_chunks = q_ref.shape[0] // c
    logits = lbl_ref[...]
    e = jnp.exp(logits - jnp.max(logits, axis=0, keepdims=True))
    p = e / jnp.sum(e, axis=0, keepdims=True)
    lb = jnp.sum(p[0:layer + 1], axis=0, keepdims=True) - p[0:1]
    ng = ng_ref[...]

    tt = lax.broadcasted_iota(jnp.int32, (c, c), 0)
    ss = lax.broadcasted_iota(jnp.int32, (c, c), 1)
    trow = lax.broadcasted_iota(jnp.int32, (c, HEAD_DIM), 0)
    levels = (1, 2, 4, 8, 16, 32)
    pair_masks = [((tt & m) != 0) & ((ss & m) == 0) & ((tt // (2 * m)) == (ss // (2 * m))) for m in levels]
    odd_rows = [(trow & m) != 0 for m in levels]
    eye = tt == ss

    def chunk(ci, carry):
        r0 = pl.multiple_of(ci * c, c)

        def head(h):
            hs = slice(h * HEAD_DIM, (h + 1) * HEAD_DIM)
            fr = f_ref[pl.ds(r0, c), hs]
            rq = q_ref[pl.ds(r0, c), hs]
            ii = i_ref[pl.ds(r0, c), hs].astype(BF16)
            lbh = lb[:, hs]
            lf = jnp.log(lbh + (1.0 - lbh) * _sigmoid(fr))
            kk = (1.0 - lbh) * _sigmoid(-fr)
            qq = rq * _sigmoid(rq)
            attn = jnp.where(eye, _dot_nt(qq.astype(BF16), kk.astype(BF16)), 0.0)
            yield
            cs = lf
            tot = lf
            for m, pmask, odd in zip(levels, pair_masks, odd_rows):
                qm = (qq * jnp.exp(cs)).astype(BF16)
                km = (kk * jnp.exp(tot - cs)).astype(BF16)
                attn = jnp.where(pmask, _dot_nt(qm, km), attn)
                prev = pltpu.roll(tot, m, axis=0)
                nxt = pltpu.roll(tot, c - m, axis=0)
                cs = cs + jnp.where(odd, prev, 0.0)
                tot = tot + jnp.where(odd, prev, nxt)
                yield
            st = st_ref[h]
            out = _dot_nt((qq * jnp.exp(cs)).astype(BF16), st.astype(BF16)) + _dot(attn.astype(BF16), ii)
            kd = (kk * jnp.exp(tot - cs)).astype(BF16)
            st_ref[h] = st * jnp.exp(tot[0:1, :]) + _dot_tn(ii, kd)
            yield
            zh = z_ref[pl.ds(r0, c), hs]
            o_ref[pl.ds(r0, c), hs] = (_rms(out, ng) * _sigmoid(zh)).astype(BF16)

        gens = [head(h) for h in range(N_HEADS)]
        while gens:
            alive = []
            for g in gens:
                try:
                    next(g)
                    alive.append(g)
                except StopIteration:
                    pass
            gens = alive
        return carry

    lax.fori_loop(0, n_chunks, chunk, 0)


def _hgrn(proj, lbl, ng, layer, b, s):
    ts = _seq_tile(s)
    ns = s // ts

    def seg(k):
        return pl.BlockSpec((None, ts, SEG_W), lambda bi, si: (k, bi * ns + si, 0))

    return pl.pallas_call(
        functools.partial(_hgrn_kernel, layer=layer),
        grid=(b, ns),
        in_specs=[seg(SEG_R_Q), seg(SEG_R_F), seg(SEG_R_I), seg(SEG_R_Z),
                  pl.BlockSpec(lbl.shape, lambda bi, si: (0, 0)),
                  pl.BlockSpec((1, HEAD_DIM), lambda bi, si: (0, 0))],
        out_specs=pl.BlockSpec((ts, GROUP_W), lambda bi, si: (bi * ns + si, 0)),
        out_shape=jax.ShapeDtypeStruct((b * s, GROUP_W), BF16),
        scratch_shapes=[pltpu.VMEM((N_HEADS, HEAD_DIM, HEAD_DIM), F32)],
        compiler_params=_params(("parallel", "arbitrary")),
        name="hgrn2",
    )(proj, proj, proj, proj, lbl, ng)


def _outproj_kernel(ya_ref, yb_ref, yc_ref, yd_ref, w_ref, h_ref, o_ref):
    acc = h_ref[...]
    for g, y_ref in enumerate((ya_ref, yb_ref, yc_ref, yd_ref)):
        acc = acc + _dot(y_ref[...], w_ref[g * GROUP_W:(g + 1) * GROUP_W, :])
    o_ref[...] = acc


def _out_proj(ys, w, h):
    n, d = h.shape
    tm = min(1024, n)
    tn = 512
    ysp = pl.BlockSpec((tm, GROUP_W), lambda i, j: (i, 0))
    return pl.pallas_call(
        _outproj_kernel,
        grid=(n // tm, d // tn),
        in_specs=[ysp, ysp, ysp, ysp,
                  pl.BlockSpec((4 * GROUP_W, tn), lambda i, j: (0, j)),
                  pl.BlockSpec((tm, tn), lambda i, j: (i, j))],
        out_specs=pl.BlockSpec((tm, tn), lambda i, j: (i, j)),
        out_shape=jax.ShapeDtypeStruct((n, d), F32),
        compiler_params=_params(("parallel", "arbitrary")),
        name="out_proj",
    )(*ys, w, h)


def _mlp_kernel(x_ref, g_ref, w1_ref, w2_ref, o_ref, xn_ref):
    f = pl.program_id(1)

    @pl.when(f == 0)
    def _():
        x = x_ref[...]
        xn_ref[...] = _rms(x, g_ref[...]).astype(BF16)
        o_ref[...] = x

    a = jnp.maximum(_dot(xn_ref[...], w1_ref[...]), 0.0)
    o_ref[...] += _dot((a * a).astype(BF16), w2_ref[...])


def _mlp(h, g, w1, w2):
    n, d = h.shape
    dff = w1.shape[1]
    tm = min(512, n)
    tf = 512
    return pl.pallas_call(
        _mlp_kernel,
        grid=(n // tm, dff // tf),
        in_specs=[pl.BlockSpec((tm, d), lambda i, f: (i, 0)),
                  pl.BlockSpec((1, d), lambda i, f: (0, 0)),
                  pl.BlockSpec((d, tf), lambda i, f: (0, f)),
                  pl.BlockSpec((tf, d), lambda i, f: (f, 0))],
        out_specs=pl.BlockSpec((tm, d), lambda i, f: (i, 0)),
        out_shape=jax.ShapeDtypeStruct((n, d), F32),
        scratch_shapes=[pltpu.VMEM((tm, d), BF16)],
        compiler_params=_params(("parallel", "arbitrary")),
        name="mlp",
    )(h, g, w1, w2)


def _final_norm_kernel(x_ref, g_ref, o_ref):
    o_ref[...] = _rms(x_ref[...], g_ref[...])


def _final_norm(h, g):
    n, d = h.shape
    tm = min(512, n)
    return pl.pallas_call(
        _final_norm_kernel,
        grid=(n // tm,),
        in_specs=[pl.BlockSpec((tm, d), lambda i: (i, 0)), pl.BlockSpec((1, d), lambda i: (0, 0))],
        out_specs=pl.BlockSpec((tm, d), lambda i: (i, 0)),
        out_shape=jax.ShapeDtypeStruct((n, d), F32),
        compiler_params=_params(("parallel",)),
        name="final_norm",
    )(h, g)


def _pack_w_in(w_in):
    widths = (GROUP_W, GROUP_W, GROUP_W, MLA_Q_RANK, MLA_KV_RANK, MLA_ROPE,
              GROUP_W, GROUP_W, GROUP_W, GROUP_W, N_HEADS, N_HEADS,
              GROUP_W, GROUP_W, GROUP_W, GROUP_W)
    cols, start = [], 0
    for wd in widths:
        cols.append(w_in[..., start:start + wd])
        start += wd
    (sc_x, sc_c, sc_b, m_cq, m_ckv, m_kr, g_q, g_k, g_v, g_z, g_a, g_b, r_q, r_f, r_i, r_z) = cols
    lead = w_in.shape[:-1]

    def zeros(wd):
        return jnp.zeros(lead + (wd,), w_in.dtype)

    half = MLA_ROPE // 2
    kr_swapped = jnp.concatenate([m_kr[..., half:], m_kr[..., :half]], axis=-1)
    seg_q = jnp.concatenate([m_cq, g_a, g_b, zeros(SEG_W - MLA_Q_RANK - 2 * N_HEADS)], axis=-1)
    seg_kv = jnp.concatenate([m_ckv, m_kr, zeros(LANES - MLA_ROPE), kr_swapped, zeros(LANES - MLA_ROPE)], axis=-1)
    segs = [sc_x, sc_c, sc_b, g_q, g_k, g_v, g_z, r_q, r_f, r_i, r_z, seg_q, seg_kv]
    return jnp.stack(segs, axis=1).astype(BF16)


def _pack_mla_weights(w_uq, w_ukv):
    l = w_uq.shape[0]
    half = MLA_ROPE // 2
    wq = w_uq.reshape(l, MLA_Q_RANK, N_HEADS, MLA_NOPE + MLA_ROPE)
    nope, rp = wq[..., :MLA_NOPE], wq[..., MLA_NOPE:]
    z64 = jnp.zeros(rp.shape, rp.dtype)
    wqa = jnp.concatenate([nope, rp, z64], axis=-1).reshape(l, MLA_Q_RANK, N_HEADS * QK_PAD)
    rp_sw = jnp.concatenate([rp[..., half:], rp[..., :half]], axis=-1)
    wqb = jnp.concatenate([rp_sw, z64], axis=-1).reshape(l, MLA_Q_RANK, N_HEADS * LANES)
    wkv = w_ukv.reshape(l, MLA_KV_RANK, N_HEADS, MLA_NOPE + HEAD_DIM)
    wk = wkv[..., :MLA_NOPE].reshape(l, MLA_KV_RANK, N_HEADS * MLA_NOPE)
    wv = wkv[..., MLA_NOPE:].reshape(l, MLA_KV_RANK, N_HEADS * HEAD_DIM)
    return wqa.astype(BF16), wqb.astype(BF16), wk.astype(BF16), jnp.swapaxes(wv, 1, 2).astype(BF16)


def _pad_lanes(x):
    l, k = x.shape
    return jnp.concatenate([x, jnp.zeros((l, LANES - k), x.dtype)], axis=-1).reshape(l, 1, LANES)


def kernel(x, positions, norm1_g, w_in, sconv_w, sconv_out_g, mla_q_g, mla_kv_g, mla_w_uq, mla_w_ukv,
           mla_out_g, gdn_conv_w, gdn_a_log, gdn_dt_bias, gdn_norm_g, hgrn_lb_logits, hgrn_norm_g,
           w_o, norm2_g, w_ff1, w_ff2, final_g):
    b, s, d = x.shape
    n = b * s
    depth = w_in.shape[0]

    w_in_p = _pack_w_in(w_in)
    wqa, wqb, wk, wvt = _pack_mla_weights(mla_w_uq, mla_w_ukv)
    w_o_b = w_o.astype(BF16)
    w1_b = w_ff1.astype(BF16)
    w2_b = w_ff2.astype(BF16)
    alog_p = _pad_lanes(gdn_a_log)
    dtb_p = _pad_lanes(gdn_dt_bias)

    half = MLA_ROPE // 2
    inv_freq = ROPE_THETA ** (-jnp.arange(half, dtype=F32) / half)
    invf = jnp.concatenate([inv_freq, inv_freq, jnp.zeros((LANES - MLA_ROPE,), F32)]).reshape(1, LANES)
    cos, sin = _rope_tables(positions.astype(F32).reshape(n, 1), invf)

    h = x.reshape(n, d)
    for l in range(depth):
        proj = _in_proj(h, norm1_g[l].reshape(1, d), w_in_p[l])
        y_sc = _sconv(proj, sconv_w[l], sconv_out_g[l].reshape(1, GROUP_W), b, s)
        q, k, vt = _mla_up(proj, cos, sin, mla_q_g[l].reshape(1, -1), mla_kv_g[l].reshape(1, -1),
                           wqa[l], wqb[l], wk[l], wvt[l], b, s)
        y_mla = _attention(q, k, vt, mla_out_g[l].reshape(1, GROUP_W), b, s)
        gdn_in = _gdn_prep(proj, gdn_conv_w[l], alog_p[l], dtb_p[l], b, s)
        y_gdn = _gdn_chunk(*gdn_in, proj, gdn_norm_g[l].reshape(1, HEAD_DIM), b, s)
        y_hg = _hgrn(proj, hgrn_lb_logits, hgrn_norm_g[l].reshape(1, HEAD_DIM), l, b, s)
        h = _out_proj((y_sc, y_mla, y_gdn, y_hg), w_o_b[l], h)
        h = _mlp(h, norm2_g[l].reshape(1, d), w1_b[l], w2_b[l])
    return _final_norm(h, final_g.reshape(1, d)).reshape(b, s, d)
```

```python
import functools

import jax
import jax.numpy as jnp
from jax import lax
from jax.experimental import pallas as pl
from jax.experimental.pallas import tpu as pltpu

F32 = jnp.float32
BF16 = jnp.bfloat16
HIGHEST = lax.Precision.HIGHEST

D_MODEL = 2048
GROUP_W = 512
HEAD_DIM = 128
N_HEADS = 4
MLA_Q_RANK = 384
MLA_KV_RANK = 256
MLA_NOPE = 128
MLA_ROPE = 64
ROPE_THETA = 10000.0
CHUNK = 64
D_FF = 4 * D_MODEL
EPS = 1e-6

SEG_W = 512
(SEG_SC_X, SEG_SC_C, SEG_SC_B, SEG_G_Q, SEG_G_K, SEG_G_V, SEG_G_Z,
 SEG_R_Q, SEG_R_F, SEG_R_I, SEG_R_Z, SEG_MLA_Q, SEG_MLA_KV) = range(13)
N_SEG = 13
QK_PAD = 256
LANES = 128
SUBLANES = 8
VMEM_LIMIT = 56 * 1024 * 1024


def _params(sem, vmem=VMEM_LIMIT):
    return pltpu.CompilerParams(dimension_semantics=sem, vmem_limit_bytes=vmem)


def _dot(a, b, precision=None):
    return jnp.dot(a, b, preferred_element_type=F32, precision=precision)


def _dot_nt(a, b, precision=None):
    return lax.dot_general(a, b, (((1,), (1,)), ((), ())), preferred_element_type=F32, precision=precision)


def _dot_tn(a, b, precision=None):
    return lax.dot_general(a, b, (((0,), (0,)), ((), ())), preferred_element_type=F32, precision=precision)


def _rms(x, g):
    return x * lax.rsqrt(jnp.mean(x * x, axis=-1, keepdims=True) + EPS) * g


def _sigmoid(x):
    return jax.nn.sigmoid(x)


def _inproj_kernel(x_ref, g_ref, w_ref, o_ref, xn_ref):
    @pl.when(pl.program_id(1) == 0)
    def _():
        xn_ref[...] = _rms(x_ref[...], g_ref[...]).astype(BF16)

    o_ref[...] = _dot(xn_ref[...], w_ref[...]).astype(o_ref.dtype)


def _in_proj(h, g, w):
    n, d = h.shape
    tm = min(1024, n)
    return pl.pallas_call(
        _inproj_kernel,
        grid=(n // tm, N_SEG),
        in_specs=[pl.BlockSpec((tm, d), lambda i, j: (i, 0)),
                  pl.BlockSpec((1, d), lambda i, j: (0, 0)),
                  pl.BlockSpec((d, SEG_W), lambda i, j: (0, j))],
        out_specs=pl.BlockSpec((None, tm, SEG_W), lambda i, j: (j, i, 0)),
        out_shape=jax.ShapeDtypeStruct((N_SEG, n, SEG_W), BF16),
        scratch_shapes=[pltpu.VMEM((tm, d), BF16)],
        compiler_params=_params(("parallel", "arbitrary")),
        name="in_proj",
    )(h, g, w)


def _sconv_kernel(x_ref, c_ref, b_ref, w_ref, g_ref, o_ref, zbuf):
    s = pl.program_id(1)
    ts = x_ref.shape[0]

    @pl.when(s == 0)
    def _():
        zbuf[0:SUBLANES, :] = jnp.zeros((SUBLANES, GROUP_W), F32)

    @pl.when(s > 0)
    def _():
        zbuf[0:SUBLANES, :] = zbuf[ts:ts + SUBLANES, :]

    zbuf[SUBLANES:ts + SUBLANES, :] = c_ref[...].astype(F32) * x_ref[...].astype(F32)
    w = w_ref[...]
    y = (w[2:3] * zbuf[SUBLANES:ts + SUBLANES, :]
         + w[1:2] * zbuf[SUBLANES - 1:ts + SUBLANES - 1, :]
         + w[0:1] * zbuf[SUBLANES - 2:ts + SUBLANES - 2, :])
    o_ref[...] = _rms(b_ref[...].astype(F32) * y, g_ref[...]).astype(BF16)


def _seq_tile(s):
    return min(512, s)


def _sconv(proj, w, g, b, s):
    ts = _seq_tile(s)
    ns = s // ts

    def seg(k):
        return pl.BlockSpec((None, ts, SEG_W), lambda bi, si: (k, bi * ns + si, 0))

    return pl.pallas_call(
        _sconv_kernel,
        grid=(b, ns),
        in_specs=[seg(SEG_SC_X), seg(SEG_SC_C), seg(SEG_SC_B),
                  pl.BlockSpec((3, GROUP_W), lambda bi, si: (0, 0)),
                  pl.BlockSpec((1, GROUP_W), lambda bi, si: (0, 0))],
        out_specs=pl.BlockSpec((ts, GROUP_W), lambda bi, si: (bi * ns + si, 0)),
        out_shape=jax.ShapeDtypeStruct((b * s, GROUP_W), BF16),
        scratch_shapes=[pltpu.VMEM((ts + SUBLANES, GROUP_W), F32)],
        compiler_params=_params(("parallel", "arbitrary")),
        name="sconv",
    )(proj, proj, proj, w, g)


def _rope_kernel(pos_ref, invf_ref, cos_ref, sin_ref):
    ang = pos_ref[...] * invf_ref[...]
    lane = lax.broadcasted_iota(jnp.int32, ang.shape, 1)
    half = MLA_ROPE // 2
    cos_ref[...] = jnp.where(lane < MLA_ROPE, jnp.cos(ang), 0.0)
    sn = jnp.sin(ang)
    sin_ref[...] = jnp.where(lane < half, -sn, jnp.where(lane < MLA_ROPE, sn, 0.0))


def _rope_tables(pos_col, invf):
    n = pos_col.shape[0]
    tm = min(2048, n)
    return pl.pallas_call(
        _rope_kernel,
        grid=(n // tm,),
        in_specs=[pl.BlockSpec((tm, 1), lambda i: (i, 0)),
                  pl.BlockSpec((1, LANES), lambda i: (0, 0))],
        out_specs=[pl.BlockSpec((tm, LANES), lambda i: (i, 0))] * 2,
        out_shape=[jax.ShapeDtypeStruct((n, LANES), F32)] * 2,
        compiler_params=_params(("parallel",)),
        name="rope_tables",
    )(pos_col, invf)


def _mla_up_kernel(a_ref, b_ref, cos_ref, sin_ref, gq_ref, gkv_ref,
                   wqa_ref, wqb_ref, wk_ref, wvt_ref, q_ref, k_ref, vt_ref):
    scale = (MLA_NOPE + MLA_ROPE) ** -0.5
    cos = cos_ref[...]
    sin = sin_ref[...]
    a = a_ref[...].astype(F32)
    cqn = _rms(a[:, :MLA_Q_RANK], gq_ref[...]).astype(BF16)
    qa = _dot(cqn, wqa_ref[...])
    qb = _dot(cqn, wqb_ref[...])
    for h in range(N_HEADS):
        o = h * QK_PAD
        q_ref[:, o:o + LANES] = (qa[:, o:o + LANES] * scale).astype(BF16)
        roped = qa[:, o + LANES:o + QK_PAD] * cos + qb[:, h * LANES:(h + 1) * LANES] * sin
        q_ref[:, o + LANES:o + QK_PAD] = (roped * scale).astype(BF16)
    bb = b_ref[...].astype(F32)
    ckvn = _rms(bb[:, :MLA_KV_RANK], gkv_ref[...]).astype(BF16)
    kk = _dot(ckvn, wk_ref[...])
    vt_ref[...] = _dot_nt(wvt_ref[...], ckvn).astype(BF16)
    kr = (bb[:, MLA_KV_RANK:MLA_KV_RANK + LANES] * cos
          + bb[:, MLA_KV_RANK + LANES:MLA_KV_RANK + 2 * LANES] * sin).astype(BF16)
    for h in range(N_HEADS):
        o = h * QK_PAD
        k_ref[:, o:o + LANES] = kk[:, h * LANES:(h + 1) * LANES].astype(BF16)
        k_ref[:, o + LANES:o + QK_PAD] = kr


def _mla_up(proj, cos, sin, gq, gkv, wqa, wqb, wk, wvt, b, s):
    n = proj.shape[1]
    tm = min(512, s)
    ns = s // tm

    def seg(k):
        return pl.BlockSpec((None, tm, SEG_W), lambda i: (k, i, 0))

    def full(x):
        return pl.BlockSpec(x.shape, lambda i: (0,) * x.ndim)

    def rows(w):
        return pl.BlockSpec((tm, w), lambda i: (i, 0))

    return pl.pallas_call(
        _mla_up_kernel,
        grid=(n // tm,),
        in_specs=[seg(SEG_MLA_Q), seg(SEG_MLA_KV), rows(LANES), rows(LANES),
                  full(gq), full(gkv), full(wqa), full(wqb), full(wk), full(wvt)],
        out_specs=[rows(N_HEADS * QK_PAD), rows(N_HEADS * QK_PAD),
                   pl.BlockSpec((None, GROUP_W, tm), lambda i: (i // ns, 0, i % ns))],
        out_shape=[jax.ShapeDtypeStruct((n, N_HEADS * QK_PAD), BF16),
                   jax.ShapeDtypeStruct((n, N_HEADS * QK_PAD), BF16),
                   jax.ShapeDtypeStruct((b, GROUP_W, s), BF16)],
        compiler_params=_params(("parallel",)),
        name="mla_up",
    )(proj, proj, cos, sin, gq, gkv, wqa, wqb, wk, wvt)


def _attn_kernel(q_ref, k_ref, vt_ref, g_ref, o_ref, acc_ref):
    i = pl.program_id(1)
    tq = q_ref.shape[0]
    kv_i = lax.broadcasted_iota(jnp.int32, (tq, tq), 0)
    q_i = lax.broadcasted_iota(jnp.int32, (tq, tq), 1)
    causal = kv_i <= q_i
    heads = range(N_HEADS)

    def block(j, carry, masked):
        start = pl.multiple_of(j * tq, tq)
        sc = [_dot_nt(k_ref[pl.ds(start, tq), h * QK_PAD:(h + 1) * QK_PAD],
                      q_ref[:, h * QK_PAD:(h + 1) * QK_PAD]) for h in heads]
        if masked:
            sc = [jnp.where(causal, x, -jnp.inf) for x in sc]
        m_old = carry[:N_HEADS]
        l_old = carry[N_HEADS:]
        m_new = [jnp.maximum(m, jnp.max(x, axis=0, keepdims=True)) for m, x in zip(m_old, sc)]
        alpha = [jnp.exp(mo - mn) for mo, mn in zip(m_old, m_new)]
        p = [jnp.exp(x - mn) for x, mn in zip(sc, m_new)]
        l_new = [a * l + jnp.sum(x, axis=0, keepdims=True) for a, l, x in zip(alpha, l_old, p)]
        pv = [_dot(vt_ref[h * HEAD_DIM:(h + 1) * HEAD_DIM, pl.ds(start, tq)], x.astype(BF16))
              for h, x in zip(heads, p)]
        for h in heads:
            acc_ref[h] = alpha[h] * acc_ref[h] + pv[h]
        return tuple(m_new) + tuple(l_new)

    acc_ref[...] = jnp.zeros(acc_ref.shape, F32)
    init = (jnp.full((1, tq), -jnp.inf, F32),) * N_HEADS + (jnp.zeros((1, tq), F32),) * N_HEADS
    carry = lax.fori_loop(0, i, functools.partial(block, masked=False), init)
    carry = block(i, carry, True)
    ot = jnp.concatenate([acc_ref[h] / carry[N_HEADS + h] for h in heads], axis=0)
    ot = ot * lax.rsqrt(jnp.mean(ot * ot, axis=0, keepdims=True) + EPS)
    o_ref[...] = (ot.T * g_ref[...]).astype(BF16)


def _attention(q, k, vt, g, b, s):
    tq = min(256, s)
    nq = s // tq
    q3 = q.reshape(b, s, N_HEADS * QK_PAD)
    k3 = k.reshape(b, s, N_HEADS * QK_PAD)
    out = pl.pallas_call(
        _attn_kernel,
        grid=(b, nq),
        in_specs=[pl.BlockSpec((None, tq, N_HEADS * QK_PAD), lambda bi, i: (bi, i, 0)),
                  pl.BlockSpec((None, s, N_HEADS * QK_PAD), lambda bi, i: (bi, 0, 0)),
                  pl.BlockSpec((None, GROUP_W, s), lambda bi, i: (bi, 0, 0)),
                  pl.BlockSpec((1, GROUP_W), lambda bi, i: (0, 0))],
        out_specs=pl.BlockSpec((None, tq, GROUP_W), lambda bi, i: (bi, i, 0)),
        out_shape=jax.ShapeDtypeStruct((b, s, GROUP_W), BF16),
        scratch_shapes=[pltpu.VMEM((N_HEADS, HEAD_DIM, tq), F32)],
        compiler_params=_params(("parallel", "arbitrary")),
        name="mla_attention",
    )(q3, k3, vt, g)
    return out.reshape(b * s, GROUP_W)


GDN_CONV_K = 4


def _gdn_prep_kernel(q_ref, k_ref, v_ref, a_ref, w_ref, alog_ref, dtb_ref,
                     qo_ref, ko_ref, kbo_ref, vbo_ref, la128_ref, la64_ref, buf):
    s = pl.program_id(1)
    ts = q_ref.shape[0]
    w = w_ref[...]
    gx = a_ref[:, MLA_Q_RANK:MLA_Q_RANK + LANES].astype(F32)
    xs = gx + dtb_ref[...]
    softplus = jnp.maximum(xs, 0.0) + jnp.log1p(jnp.exp(-jnp.abs(xs)))
    log_a = -jnp.exp(alog_ref[...]) * softplus
    beta = _sigmoid(gx)
    lane = lax.broadcasted_iota(jnp.int32, (ts, LANES), 1)
    la_b = [jnp.broadcast_to(log_a[:, h:h + 1], (ts, LANES)) for h in range(N_HEADS)]
    beta_b = [jnp.broadcast_to(beta[:, N_HEADS + h:N_HEADS + h + 1], (ts, HEAD_DIM)) for h in range(N_HEADS)]
    for h in range(N_HEADS):
        la128_ref[:, h * HEAD_DIM:(h + 1) * HEAD_DIM] = la_b[h]
    for pr in range(N_HEADS // 2):
        la64_ref[:, pr * LANES:(pr + 1) * LANES] = jnp.where(lane < CHUNK, la_b[2 * pr], la_b[2 * pr + 1])
    for idx, src in enumerate((q_ref, k_ref, v_ref)):
        @pl.when(s == 0)
        def _(idx=idx):
            buf[idx, 0:SUBLANES, :] = jnp.zeros((SUBLANES, GROUP_W), F32)

        @pl.when(s > 0)
        def _(idx=idx):
            buf[idx, 0:SUBLANES, :] = buf[idx, ts:ts + SUBLANES, :]

        buf[idx, SUBLANES:ts + SUBLANES, :] = src[...].astype(F32)
        wi = w[:, idx * GROUP_W:(idx + 1) * GROUP_W]
        y = wi[3:4] * buf[idx, SUBLANES:ts + SUBLANES, :]
        for j in range(1, GDN_CONV_K):
            y = y + wi[3 - j:4 - j] * buf[idx, SUBLANES - j:ts + SUBLANES - j, :]
        y = y * _sigmoid(y)
        for h in range(N_HEADS):
            hs = slice(h * HEAD_DIM, (h + 1) * HEAD_DIM)
            yh = y[:, hs]
            if idx < 2:
                yh = yh * lax.rsqrt(jnp.sum(yh * yh, axis=-1, keepdims=True) + EPS)
            if idx == 0:
                qo_ref[:, hs] = (yh * (HEAD_DIM ** -0.5)).astype(BF16)
            elif idx == 1:
                ko_ref[:, hs] = yh.astype(BF16)
                kbo_ref[:, hs] = (yh * beta_b[h]).astype(BF16)
            else:
                vbo_ref[:, hs] = (yh * beta_b[h]).astype(BF16)


def _gdn_prep(proj, w, alog, dtb, b, s):
    ts = _seq_tile(s)
    ns = s // ts

    def seg(k):
        return pl.BlockSpec((None, ts, SEG_W), lambda bi, si: (k, bi * ns + si, 0))

    def rows(wd):
        return pl.BlockSpec((ts, wd), lambda bi, si: (bi * ns + si, 0))

    def full(x):
        return pl.BlockSpec(x.shape, lambda bi, si: (0,) * x.ndim)

    n = b * s
    return pl.pallas_call(
        _gdn_prep_kernel,
        grid=(b, ns),
        in_specs=[seg(SEG_G_Q), seg(SEG_G_K), seg(SEG_G_V), seg(SEG_MLA_Q),
                  full(w), full(alog), full(dtb)],
        out_specs=[rows(GROUP_W)] * 5 + [rows(N_HEADS * CHUNK)],
        out_shape=[jax.ShapeDtypeStruct((n, GROUP_W), BF16)] * 4 + [jax.ShapeDtypeStruct((n, GROUP_W), F32)]
        + [jax.ShapeDtypeStruct((n, N_HEADS * CHUNK), F32)],
        scratch_shapes=[pltpu.VMEM((3, ts + SUBLANES, GROUP_W), F32)],
        compiler_params=_params(("parallel", "arbitrary")),
        name="gdn_prep",
    )(proj, proj, proj, proj, w, alog, dtb)


GDN_GROUP = 4


def _split3(x):
    hi = x.astype(BF16)
    r1 = x - hi.astype(F32)
    mid = r1.astype(BF16)
    lo = (r1 - mid.astype(F32)).astype(BF16)
    return jnp.concatenate([hi, mid, lo], axis=0)


def _gdn_chunk_kernel(q_ref, k_ref, kb_ref, vb_ref, la128_ref, la64_ref, z_ref, ng_ref, o_ref,
                      st_ref, l_s, at_s, rhs_s, wq_s, u_s, kd_s, egl_s):
    @pl.when(pl.program_id(1) == 0)
    def _():
        st_ref[...] = jnp.zeros(st_ref.shape, F32)

    c = CHUNK
    n_chunks = q_ref.shape[0] // c
    pw = N_HEADS * c
    row = lax.broadcasted_iota(jnp.int32, (c, pw), 0)
    col = lax.broadcasted_iota(jnp.int32, (c, pw), 1) % c
    tri = col <= row
    strict = col < row
    eye_f = (col == row).astype(F32)
    diag8 = (col // SUBLANES) == (row // SUBLANES)
    merge_levels = (8, 16, 32)
    merge_masks = [((row & m) != 0) & ((col & m) == 0) & ((row // (2 * m)) == (col // (2 * m)))
                   for m in merge_levels]
    lane_head_p = lax.broadcasted_iota(jnp.int32, (c, pw), 1) // c
    head_sel_p = [(lane_head_p == h).astype(BF16) for h in range(N_HEADS)]
    lane_head_w = lax.broadcasted_iota(jnp.int32, (c, GROUP_W), 1) // HEAD_DIM
    head_sel_w = [(lane_head_w == h).astype(BF16) for h in range(N_HEADS)]
    row3 = lax.broadcasted_iota(jnp.int32, (c, 3 * c), 0)
    col3 = lax.broadcasted_iota(jnp.int32, (c, 3 * c), 1)
    tri3 = ((col3 % c) <= row3).astype(BF16)
    zeros_half = jnp.zeros((c, 2 * HEAD_DIM), BF16)
    ng = ng_ref[...]

    def block_diag(x_b):
        return jnp.concatenate([x_b * head_sel_p[h] for h in range(N_HEADS)], axis=0)

    def pad_rows(y_b, second):
        z = zeros_half[:, :y_b.shape[1]]
        return jnp.concatenate([z, y_b] if second else [y_b, z], axis=0)

    def prepare(ci, carry):
        r0 = pl.multiple_of(ci * c, c)
        rows_c = pl.ds(r0, c)
        dlt = _dot(tri3, _split3(jnp.where(strict, la64_ref[rows_c, :], 0.0)))
        gc = _dot(tri3, _split3(la128_ref[rows_c, :]))
        decay = jnp.exp(jnp.where(tri, dlt, -jnp.inf))
        eg = jnp.exp(gc)
        kb = kb_ref[rows_c, :]
        k_b = k_ref[rows_c, :]
        q_b = q_ref[rows_c, :]
        k_rows = jnp.concatenate([k_b * head_sel_w[h] for h in range(N_HEADS)], axis=0)
        qk = _dot_nt(jnp.concatenate([kb, q_b], axis=0), k_rows)
        l_s[ci] = jnp.where(strict, qk[:c] * decay, 0.0).astype(BF16)
        at_s[ci] = jnp.where(tri, qk[c:] * decay, 0.0).astype(BF16)
        kbe = kb.astype(F32) * eg
        qg = (q_b.astype(F32) * eg).astype(BF16)
        gl = gc[c - 1:c, :]
        kd = (k_b.astype(F32) * jnp.exp(gl - gc)).astype(BF16)
        vb = vb_ref[rows_c, :]
        for h in range(N_HEADS):
            hs = slice(h * HEAD_DIM, (h + 1) * HEAD_DIM)
            rhs_s[ci, h, :, 0:HEAD_DIM] = vb[:, hs]
            rhs_s[ci, h, :, HEAD_DIM:2 * HEAD_DIM] = kbe[:, hs].astype(BF16)
            wq_s[ci, h, c:2 * c, :] = qg[:, hs]
            kd_s[ci, h] = kd[:, hs]
        egl_s[ci] = jnp.broadcast_to(jnp.exp(gl), (SUBLANES, GROUP_W))
        return carry

    def invert(ci):
        lf = l_s[ci].astype(F32)
        n8 = jnp.where(diag8, -lf, 0.0)
        t = eye_f + n8
        p = n8.astype(BF16)
        p = _dot(p, block_diag(p)).astype(BF16)
        yield
        pt = _dot(jnp.concatenate([p, t.astype(BF16)], axis=0), block_diag(p))
        t = t + pt[c:]
        p = pt[:c].astype(BF16)
        yield
        t = t + _dot(t.astype(BF16), block_diag(p))
        yield
        for mask in merge_masks:
            off = jnp.where(mask, lf, 0.0).astype(BF16)
            a = _dot(t.astype(BF16), block_diag(off)).astype(BF16)
            yield
            t = t - _dot(a, block_diag(t.astype(BF16)))
            yield
        tb = t.astype(BF16)
        for h in range(N_HEADS):
            pr = slice((h // 2) * LANES, (h // 2 + 1) * LANES)
            x = _dot(tb[:, pr], pad_rows(rhs_s[ci, h], h % 2))
            u_s[ci, h] = x[:, :HEAD_DIM]
            wq_s[ci, h, 0:c, :] = x[:, HEAD_DIM:].astype(BF16)
        yield

    def recur(cis):
        for ci in cis:
            r0 = pl.multiple_of(ci * c, c)
            at = at_s[ci]
            egl = egl_s[ci]
            sts = [st_ref[h] for h in range(N_HEADS)]
            wss = [_dot(wq_s[ci, h], sts[h].astype(BF16)) for h in range(N_HEADS)]
            yield
            vnbs = [(u_s[ci, h] - wss[h][:c]).astype(BF16) for h in range(N_HEADS)]
            for h in range(N_HEADS):
                hs = slice(h * HEAD_DIM, (h + 1) * HEAD_DIM)
                pr = slice((h // 2) * LANES, (h // 2 + 1) * LANES)
                out = wss[h][c:] + _dot(at[:, pr], pad_rows(vnbs[h], h % 2))
                st_ref[h] = sts[h] * egl[0:1, hs] + _dot_tn(kd_s[ci, h], vnbs[h])
                zh = z_ref[pl.ds(r0, c), hs].astype(F32)
                o_ref[pl.ds(r0, c), hs] = (_rms(out, ng) * (zh * _sigmoid(zh))).astype(BF16)
            yield

    def interleave(gens):
        gens = list(gens)
        while gens:
            alive = []
            for g in gens:
                try:
                    next(g)
                    alive.append(g)
                except StopIteration:
                    pass
            gens = alive

    lax.fori_loop(0, n_chunks, prepare, 0)
    interleave(invert(i) for i in range(GDN_GROUP))

    def group(j, carry):
        c0 = j * GDN_GROUP
        interleave([recur([c0 + i for i in range(GDN_GROUP)])]
                   + [invert(c0 + GDN_GROUP + i) for i in range(GDN_GROUP)])
        return carry

    lax.fori_loop(0, n_chunks // GDN_GROUP - 1, group, 0)
    interleave([recur([n_chunks - GDN_GROUP + i for i in range(GDN_GROUP)])])


def _gdn_chunk(q, k, kb, vb, la128, la64, proj, ng, b, s):
    ts = _seq_tile(s)
    ns = s // ts
    nc = ts // CHUNK
    assert nc % GDN_GROUP == 0

    def rows(wd):
        return pl.BlockSpec((ts, wd), lambda bi, si: (bi * ns + si, 0))

    return pl.pallas_call(
        _gdn_chunk_kernel,
        grid=(b, ns),
        in_specs=[rows(GROUP_W)] * 5 + [rows(N_HEADS * CHUNK),
                  pl.BlockSpec((None, ts, SEG_W), lambda bi, si: (SEG_G_Z, bi * ns + si, 0)),
                  pl.BlockSpec((1, HEAD_DIM), lambda bi, si: (0, 0))],
        out_specs=rows(GROUP_W),
        out_shape=jax.ShapeDtypeStruct((b * s, GROUP_W), BF16),
        scratch_shapes=[pltpu.VMEM((N_HEADS, HEAD_DIM, HEAD_DIM), F32),
                        pltpu.VMEM((nc, CHUNK, N_HEADS * CHUNK), BF16),
                        pltpu.VMEM((nc, CHUNK, N_HEADS * CHUNK), BF16),
                        pltpu.VMEM((nc, N_HEADS, CHUNK, 2 * HEAD_DIM), BF16),
                        pltpu.VMEM((nc, N_HEADS, 2 * CHUNK, HEAD_DIM), BF16),
                        pltpu.VMEM((nc, N_HEADS, CHUNK, HEAD_DIM), F32),
                        pltpu.VMEM((nc, N_HEADS, CHUNK, HEAD_DIM), BF16),
                        pltpu.VMEM((nc, SUBLANES, GROUP_W), F32)],
        compiler_params=_params(("parallel", "arbitrary")),
        name="gdn_chunk",
    )(q, k, kb, vb, la128, la64, proj, ng)


def _hgrn_kernel(q_ref, f_ref, i_ref, z_ref, lbl_ref, ng_ref, o_ref, st_ref, *, layer):
    @pl.when(pl.program_id(1) == 0)
    def _():
        st_ref[...] = jnp.zeros(st_ref.shape, F32)

    c = CHUNK
    n_chunks = q_ref.shape[0] // c
    logits = lbl_ref[...]
    e = jnp.exp(logits - jnp.max(logits, axis=0, keepdims=True))
    p = e / jnp.sum(e, axis=0, keepdims=True)
    lb = jnp.sum(p[0:layer + 1], axis=0, keepdims=True) - p[0:1]
    ng = ng_ref[...]

    tt = lax.broadcasted_iota(jnp.int32, (c, c), 0)
    ss = lax.broadcasted_iota(jnp.int32, (c, c), 1)
    trow = lax.broadcasted_iota(jnp.int32, (c, HEAD_DIM), 0)
    levels = (1, 2, 4, 8, 16, 32)
    pair_masks = [((tt & m) != 0) & ((ss & m) == 0) & ((tt // (2 * m)) == (ss // (2 * m))) for m in levels]
    odd_rows = [(trow & m) != 0 for m in levels]
    eye = tt == ss

    def chunk(ci, carry):
        r0 = pl.multiple_of(ci * c, c)

        def head(h):
            hs = slice(h * HEAD_DIM, (h + 1) * HEAD_DIM)
            fr = f_ref[pl.ds(r0, c), hs].astype(F32)
            rq = q_ref[pl.ds(r0, c), hs].astype(F32)
            ii = i_ref[pl.ds(r0, c), hs]
            lbh = lb[:, hs]
            lf = jnp.log(lbh + (1.0 - lbh) * _sigmoid(fr))
            kk = (1.0 - lbh) * _sigmoid(-fr)
            qq = rq * _sigmoid(rq)
            attn = jnp.where(eye, _dot_nt(qq.astype(BF16), kk.astype(BF16)), 0.0)
            yield
            cs = lf
            tot = lf
            for m, pmask, odd in zip(levels, pair_masks, odd_rows):
                qm = (qq * jnp.exp(cs)).astype(BF16)
                km = (kk * jnp.exp(tot - cs)).astype(BF16)
                attn = jnp.where(pmask, _dot_nt(qm, km), attn)
                prev = pltpu.roll(tot, m, axis=0)
                nxt = pltpu.roll(tot, c - m, axis=0)
                cs = cs + jnp.where(odd, prev, 0.0)
                tot = tot + jnp.where(odd, prev, nxt)
                yield
            st = st_ref[h]
            out = _dot_nt((qq * jnp.exp(cs)).astype(BF16), st.astype(BF16)) + _dot(attn.astype(BF16), ii)
            kd = (kk * jnp.exp(tot - cs)).astype(BF16)
            st_ref[h] = st * jnp.exp(tot[0:1, :]) + _dot_tn(ii, kd)
            yield
            zh = z_ref[pl.ds(r0, c), hs].astype(F32)
            o_ref[pl.ds(r0, c), hs] = (_rms(out, ng) * _sigmoid(zh)).astype(BF16)

        gens = [head(h) for h in range(N_HEADS)]
        while gens:
            alive = []
            for g in gens:
                try:
                    next(g)
                    alive.append(g)
                except StopIteration:
                    pass
            gens = alive
        return carry

    lax.fori_loop(0, n_chunks, chunk, 0)


def _hgrn(proj, lbl, ng, layer, b, s):
    ts = _seq_tile(s)
    ns = s // ts

    def seg(k):
        return pl.BlockSpec((None, ts, SEG_W), lambda bi, si: (k, bi * ns + si, 0))

    return pl.pallas_call(
        functools.partial(_hgrn_kernel, layer=layer),
        grid=(b, ns),
        in_specs=[seg(SEG_R_Q), seg(SEG_R_F), seg(SEG_R_I), seg(SEG_R_Z),
                  pl.BlockSpec(lbl.shape, lambda bi, si: (0, 0)),
                  pl.BlockSpec((1, HEAD_DIM), lambda bi, si: (0, 0))],
        out_specs=pl.BlockSpec((ts, GROUP_W), lambda bi, si: (bi * ns + si, 0)),
        out_shape=jax.ShapeDtypeStruct((b * s, GROUP_W), BF16),
        scratch_shapes=[pltpu.VMEM((N_HEADS, HEAD_DIM, HEAD_DIM), F32)],
        compiler_params=_params(("parallel", "arbitrary")),
        name="hgrn2",
    )(proj, proj, proj, proj, lbl, ng)


def _mix_mlp_kernel(ya_ref, yb_ref, yc_ref, yd_ref, wo_ref, h_ref, g_ref, w1_ref, w2_ref, fg_ref,
                    o_ref, xn_ref, *, final_norm):
    f = pl.program_id(1)

    @pl.when(f == 0)
    def _():
        y = jnp.concatenate([ya_ref[...], yb_ref[...], yc_ref[...], yd_ref[...]], axis=1)
        h1 = h_ref[...] + _dot(y, wo_ref[...])
        xn_ref[...] = _rms(h1, g_ref[...]).astype(BF16)
        o_ref[...] = h1

    a = jnp.maximum(_dot(xn_ref[...], w1_ref[...]), 0.0)
    o_ref[...] += _dot((a * a).astype(BF16), w2_ref[...])

    if final_norm:
        @pl.when(f == pl.num_programs(1) - 1)
        def _():
            o_ref[...] = _rms(o_ref[...], fg_ref[...])


def _mix_mlp(ys, wo, h, g, w1, w2, fg, final_norm):
    n, d = h.shape
    dff = w1.shape[1]
    tm = min(512, n)
    tf = 512
    ysp = pl.BlockSpec((tm, GROUP_W), lambda i, f: (i, 0))
    rows = pl.BlockSpec((tm, d), lambda i, f: (i, 0))
    vec = pl.BlockSpec((1, d), lambda i, f: (0, 0))
    return pl.pallas_call(
        functools.partial(_mix_mlp_kernel, final_norm=final_norm),
        grid=(n // tm, dff // tf),
        in_specs=[ysp, ysp, ysp, ysp,
                  pl.BlockSpec((d, d), lambda i, f: (0, 0), pipeline_mode=pl.Buffered(1)),
                  rows, vec,
                  pl.BlockSpec((d, tf), lambda i, f: (0, f)),
                  pl.BlockSpec((tf, d), lambda i, f: (f, 0)),
                  vec],
        out_specs=rows,
        out_shape=jax.ShapeDtypeStruct((n, d), F32),
        scratch_shapes=[pltpu.VMEM((tm, d), BF16)],
        compiler_params=_params(("parallel", "arbitrary")),
        name="mix_mlp",
    )(*ys, wo, h, g, w1, w2, fg)


def _pack_w_in(w_in):
    w_in = w_in.astype(BF16)
    widths = (GROUP_W, GROUP_W, GROUP_W, MLA_Q_RANK, MLA_KV_RANK, MLA_ROPE,
              GROUP_W, GROUP_W, GROUP_W, GROUP_W, N_HEADS, N_HEADS,
              GROUP_W, GROUP_W, GROUP_W, GROUP_W)
    cols, start = [], 0
    for wd in widths:
        cols.append(w_in[..., start:start + wd])
        start += wd
    (sc_x, sc_c, sc_b, m_cq, m_ckv, m_kr, g_q, g_k, g_v, g_z, g_a, g_b, r_q, r_f, r_i, r_z) = cols
    lead = w_in.shape[:-1]

    def zeros(wd):
        return jnp.zeros(lead + (wd,), w_in.dtype)

    half = MLA_ROPE // 2
    kr_swapped = jnp.concatenate([m_kr[..., half:], m_kr[..., :half]], axis=-1)
    seg_q = jnp.concatenate([m_cq, g_a, g_b, zeros(SEG_W - MLA_Q_RANK - 2 * N_HEADS)], axis=-1)
    seg_kv = jnp.concatenate([m_ckv, m_kr, zeros(LANES - MLA_ROPE), kr_swapped, zeros(LANES - MLA_ROPE)], axis=-1)
    segs = [sc_x, sc_c, sc_b, g_q, g_k, g_v, g_z, r_q, r_f, r_i, r_z, seg_q, seg_kv]
    return jnp.concatenate(segs, axis=-1)


def _pack_mla_weights(w_uq, w_ukv):
    l = w_uq.shape[0]
    half = MLA_ROPE // 2
    wq = w_uq.reshape(l, MLA_Q_RANK, N_HEADS, MLA_NOPE + MLA_ROPE)
    nope, rp = wq[..., :MLA_NOPE], wq[..., MLA_NOPE:]
    z64 = jnp.zeros(rp.shape, rp.dtype)
    wqa = jnp.concatenate([nope, rp, z64], axis=-1).reshape(l, MLA_Q_RANK, N_HEADS * QK_PAD)
    rp_sw = jnp.concatenate([rp[..., half:], rp[..., :half]], axis=-1)
    wqb = jnp.concatenate([rp_sw, z64], axis=-1).reshape(l, MLA_Q_RANK, N_HEADS * LANES)
    wkv = w_ukv.reshape(l, MLA_KV_RANK, N_HEADS, MLA_NOPE + HEAD_DIM)
    wk = wkv[..., :MLA_NOPE].reshape(l, MLA_KV_RANK, N_HEADS * MLA_NOPE)
    wv = wkv[..., MLA_NOPE:].reshape(l, MLA_KV_RANK, N_HEADS * HEAD_DIM)
    return wqa.astype(BF16), wqb.astype(BF16), wk.astype(BF16), jnp.swapaxes(wv, 1, 2).astype(BF16)


def _pad_lanes(x):
    l, k = x.shape
    return jnp.concatenate([x, jnp.zeros((l, LANES - k), x.dtype)], axis=-1).reshape(l, 1, LANES)


def kernel(x, positions, norm1_g, w_in, sconv_w, sconv_out_g, mla_q_g, mla_kv_g, mla_w_uq, mla_w_ukv,
           mla_out_g, gdn_conv_w, gdn_a_log, gdn_dt_bias, gdn_norm_g, hgrn_lb_logits, hgrn_norm_g,
           w_o, norm2_g, w_ff1, w_ff2, final_g):
    b, s, d = x.shape
    n = b * s
    depth = w_in.shape[0]

    w_in_p = _pack_w_in(w_in)
    wqa, wqb, wk, wvt = _pack_mla_weights(mla_w_uq, mla_w_ukv)
    w_o_b = w_o.astype(BF16)
    w1_b = w_ff1.astype(BF16)
    w2_b = w_ff2.astype(BF16)
    alog_p = _pad_lanes(gdn_a_log)
    dtb_p = _pad_lanes(gdn_dt_bias)

    half = MLA_ROPE // 2
    inv_freq = ROPE_THETA ** (-jnp.arange(half, dtype=F32) / half)
    invf = jnp.concatenate([inv_freq, inv_freq, jnp.zeros((LANES - MLA_ROPE,), F32)]).reshape(1, LANES)
    cos, sin = _rope_tables(positions.astype(F32).reshape(n, 1), invf)

    h = x.reshape(n, d)
    for l in range(depth):
        proj = _in_proj(h, norm1_g[l].reshape(1, d), w_in_p[l])
        y_sc = _sconv(proj, sconv_w[l], sconv_out_g[l].reshape(1, GROUP_W), b, s)
        q, k, vt = _mla_up(proj, cos, sin, mla_q_g[l].reshape(1, -1), mla_kv_g[l].reshape(1, -1),
                           wqa[l], wqb[l], wk[l], wvt[l], b, s)
        y_mla = _attention(q, k, vt, mla_out_g[l].reshape(1, GROUP_W), b, s)
        gdn_in = _gdn_prep(proj, gdn_conv_w[l], alog_p[l], dtb_p[l], b, s)
        y_gdn = _gdn_chunk(*gdn_in, proj, gdn_norm_g[l].reshape(1, HEAD_DIM), b, s)
        y_hg = _hgrn(proj, hgrn_lb_logits, hgrn_norm_g[l].reshape(1, HEAD_DIM), l, b, s)
        h = _mix_mlp((y_sc, y_mla, y_gdn, y_hg), w_o_b[l], h, norm2_g[l].reshape(1, d), w1_b[l], w2_b[l],
                     final_g.reshape(1, d), final_norm=(l == depth - 1))
    return h.reshape(b, s, d)
```

```python
import functools

import jax
import jax.numpy as jnp
from jax import lax
from jax.experimental import pallas as pl
from jax.experimental.pallas import tpu as pltpu

F32 = jnp.float32
BF16 = jnp.bfloat16
HIGHEST = lax.Precision.HIGHEST

D_MODEL = 2048
GROUP_W = 512
HEAD_DIM = 128
N_HEADS = 4
MLA_Q_RANK = 384
MLA_KV_RANK = 256
MLA_NOPE = 128
MLA_ROPE = 64
ROPE_THETA = 10000.0
CHUNK = 64
D_FF = 4 * D_MODEL
EPS = 1e-6
LOG2_E = 1.4426950408889634

SEG_W = 512
(SEG_SC_X, SEG_SC_C, SEG_SC_B, SEG_G_Q, SEG_G_K, SEG_G_V, SEG_G_Z,
 SEG_R_Q, SEG_R_F, SEG_R_I, SEG_R_Z, SEG_MLA_Q, SEG_MLA_KV) = range(13)
N_SEG = 13
QK_PAD = 256
LANES = 128
SUBLANES = 8
VMEM_LIMIT = 56 * 1024 * 1024


def _params(sem, vmem=VMEM_LIMIT):
    return pltpu.CompilerParams(dimension_semantics=sem, vmem_limit_bytes=vmem)


def _dot(a, b, precision=None):
    return jnp.dot(a, b, preferred_element_type=F32, precision=precision)


def _dot_nt(a, b, precision=None):
    return lax.dot_general(a, b, (((1,), (1,)), ((), ())), preferred_element_type=F32, precision=precision)


def _dot_tn(a, b, precision=None):
    return lax.dot_general(a, b, (((0,), (0,)), ((), ())), preferred_element_type=F32, precision=precision)


def _rms(x, g):
    return x * lax.rsqrt(jnp.mean(x * x, axis=-1, keepdims=True) + EPS) * g


def _sigmoid(x):
    return jax.nn.sigmoid(x)


def _inproj_kernel(x_ref, g_ref, w_ref, o_ref, xn_ref):
    @pl.when(pl.program_id(1) == 0)
    def _():
        xn_ref[...] = _rms(x_ref[...], g_ref[...]).astype(BF16)

    o_ref[...] = _dot(xn_ref[...], w_ref[...]).astype(o_ref.dtype)


def _in_proj(h, g, w, layer):
    n, d = h.shape
    tm = min(1024, n)
    return pl.pallas_call(
        _inproj_kernel,
        grid=(n // tm, N_SEG),
        in_specs=[pl.BlockSpec((tm, d), lambda i, j: (i, 0)),
                  pl.BlockSpec((1, d), lambda i, j: (0, 0)),
                  pl.BlockSpec((None, d, SEG_W), lambda i, j: (layer, 0, j))],
        out_specs=pl.BlockSpec((None, tm, SEG_W), lambda i, j: (j, i, 0)),
        out_shape=jax.ShapeDtypeStruct((N_SEG, n, SEG_W), BF16),
        scratch_shapes=[pltpu.VMEM((tm, d), BF16)],
        compiler_params=_params(("parallel", "arbitrary")),
        name="in_proj",
    )(h, g, w)


def _sconv_kernel(x_ref, c_ref, b_ref, w_ref, g_ref, o_ref, zbuf):
    s = pl.program_id(1)
    ts = x_ref.shape[0]

    @pl.when(s == 0)
    def _():
        zbuf[0:SUBLANES, :] = jnp.zeros((SUBLANES, GROUP_W), F32)

    @pl.when(s > 0)
    def _():
        zbuf[0:SUBLANES, :] = zbuf[ts:ts + SUBLANES, :]

    zbuf[SUBLANES:ts + SUBLANES, :] = c_ref[...].astype(F32) * x_ref[...].astype(F32)
    w = w_ref[...]
    y = (w[2:3] * zbuf[SUBLANES:ts + SUBLANES, :]
         + w[1:2] * zbuf[SUBLANES - 1:ts + SUBLANES - 1, :]
         + w[0:1] * zbuf[SUBLANES - 2:ts + SUBLANES - 2, :])
    o_ref[...] = _rms(b_ref[...].astype(F32) * y, g_ref[...]).astype(BF16)


def _seq_tile(s):
    return min(512, s)


def _sconv(proj, w, g, b, s):
    ts = _seq_tile(s)
    ns = s // ts

    def seg(k):
        return pl.BlockSpec((None, ts, SEG_W), lambda bi, si: (k, bi * ns + si, 0))

    return pl.pallas_call(
        _sconv_kernel,
        grid=(b, ns),
        in_specs=[seg(SEG_SC_X), seg(SEG_SC_C), seg(SEG_SC_B),
                  pl.BlockSpec((3, GROUP_W), lambda bi, si: (0, 0)),
                  pl.BlockSpec((1, GROUP_W), lambda bi, si: (0, 0))],
        out_specs=pl.BlockSpec((ts, GROUP_W), lambda bi, si: (bi * ns + si, 0)),
        out_shape=jax.ShapeDtypeStruct((b * s, GROUP_W), BF16),
        scratch_shapes=[pltpu.VMEM((ts + SUBLANES, GROUP_W), F32)],
        compiler_params=_params(("parallel", "arbitrary")),
        name="sconv",
    )(proj, proj, proj, w, g)


def _rope_kernel(pos_ref, invf_ref, cos_ref, sin_ref):
    ang = pos_ref[...] * invf_ref[...]
    lane = lax.broadcasted_iota(jnp.int32, ang.shape, 1)
    half = MLA_ROPE // 2
    cos_ref[...] = jnp.where(lane < MLA_ROPE, jnp.cos(ang), 0.0)
    sn = jnp.sin(ang)
    sin_ref[...] = jnp.where(lane < half, -sn, jnp.where(lane < MLA_ROPE, sn, 0.0))


def _rope_tables(pos_col, invf):
    n = pos_col.shape[0]
    tm = min(2048, n)
    return pl.pallas_call(
        _rope_kernel,
        grid=(n // tm,),
        in_specs=[pl.BlockSpec((tm, 1), lambda i: (i, 0)),
                  pl.BlockSpec((1, LANES), lambda i: (0, 0))],
        out_specs=[pl.BlockSpec((tm, LANES), lambda i: (i, 0))] * 2,
        out_shape=[jax.ShapeDtypeStruct((n, LANES), F32)] * 2,
        compiler_params=_params(("parallel",)),
        name="rope_tables",
    )(pos_col, invf)


def _mla_up_kernel(a_ref, b_ref, cos_ref, sin_ref, gq_ref, gkv_ref,
                   wqa_ref, wqb_ref, wk_ref, wvt_ref, q_ref, k_ref, vt_ref):
    scale = (MLA_NOPE + MLA_ROPE) ** -0.5 * LOG2_E
    cos = cos_ref[...]
    sin = sin_ref[...]
    a = a_ref[...].astype(F32)
    cqn = _rms(a[:, :MLA_Q_RANK], gq_ref[...]).astype(BF16)
    qa = _dot(cqn, wqa_ref[...])
    qb = _dot(cqn, wqb_ref[...])
    for h in range(N_HEADS):
        o = h * QK_PAD
        q_ref[:, o:o + LANES] = (qa[:, o:o + LANES] * scale).astype(BF16)
        roped = qa[:, o + LANES:o + QK_PAD] * cos + qb[:, h * LANES:(h + 1) * LANES] * sin
        q_ref[:, o + LANES:o + QK_PAD] = (roped * scale).astype(BF16)
    bb = b_ref[...].astype(F32)
    ckvn = _rms(bb[:, :MLA_KV_RANK], gkv_ref[...]).astype(BF16)
    kk = _dot(ckvn, wk_ref[...])
    vt_ref[...] = _dot_nt(wvt_ref[...], ckvn).astype(BF16)
    kr = (bb[:, MLA_KV_RANK:MLA_KV_RANK + LANES] * cos
          + bb[:, MLA_KV_RANK + LANES:MLA_KV_RANK + 2 * LANES] * sin).astype(BF16)
    for h in range(N_HEADS):
        o = h * QK_PAD
        k_ref[:, o:o + LANES] = kk[:, h * LANES:(h + 1) * LANES].astype(BF16)
        k_ref[:, o + LANES:o + QK_PAD] = kr


def _mla_up(proj, cos, sin, gq, gkv, wqa, wqb, wk, wvt, b, s):
    n = proj.shape[1]
    tm = min(512, s)
    ns = s // tm

    def seg(k):
        return pl.BlockSpec((None, tm, SEG_W), lambda i: (k, i, 0))

    def full(x):
        return pl.BlockSpec(x.shape, lambda i: (0,) * x.ndim)

    def rows(w):
        return pl.BlockSpec((tm, w), lambda i: (i, 0))

    return pl.pallas_call(
        _mla_up_kernel,
        grid=(n // tm,),
        in_specs=[seg(SEG_MLA_Q), seg(SEG_MLA_KV), rows(LANES), rows(LANES),
                  full(gq), full(gkv), full(wqa), full(wqb), full(wk), full(wvt)],
        out_specs=[rows(N_HEADS * QK_PAD), rows(N_HEADS * QK_PAD),
                   pl.BlockSpec((None, GROUP_W, tm), lambda i: (i // ns, 0, i % ns))],
        out_shape=[jax.ShapeDtypeStruct((n, N_HEADS * QK_PAD), BF16),
                   jax.ShapeDtypeStruct((n, N_HEADS * QK_PAD), BF16),
                   jax.ShapeDtypeStruct((b, GROUP_W, s), BF16)],
        compiler_params=_params(("parallel",)),
        name="mla_up",
    )(proj, proj, cos, sin, gq, gkv, wqa, wqb, wk, wvt)


def _attn_kernel(q_ref, k_ref, vt_ref, g_ref, o_ref, acc_ref):
    i = pl.program_id(1)
    tq = q_ref.shape[0]
    kv_i = lax.broadcasted_iota(jnp.int32, (tq, tq), 0)
    q_i = lax.broadcasted_iota(jnp.int32, (tq, tq), 1)
    causal = kv_i <= q_i
    heads = range(N_HEADS)

    def block(j, carry, masked):
        start = pl.multiple_of(j * tq, tq)
        sc = [_dot_nt(k_ref[pl.ds(start, tq), h * QK_PAD:(h + 1) * QK_PAD],
                      q_ref[:, h * QK_PAD:(h + 1) * QK_PAD]) for h in heads]
        if masked:
            sc = [jnp.where(causal, x, -jnp.inf) for x in sc]
        m_old = carry[:N_HEADS]
        l_old = carry[N_HEADS:]
        m_new = [jnp.maximum(m, jnp.max(x, axis=0, keepdims=True)) for m, x in zip(m_old, sc)]
        alpha = [jnp.exp2(mo - mn) for mo, mn in zip(m_old, m_new)]
        p = [jnp.exp2(x - mn) for x, mn in zip(sc, m_new)]
        l_new = [a * l + jnp.sum(x, axis=0, keepdims=True) for a, l, x in zip(alpha, l_old, p)]
        pv = [_dot(vt_ref[h * HEAD_DIM:(h + 1) * HEAD_DIM, pl.ds(start, tq)], x.astype(BF16))
              for h, x in zip(heads, p)]
        for h in heads:
            acc_ref[h] = alpha[h] * acc_ref[h] + pv[h]
        return tuple(m_new) + tuple(l_new)

    acc_ref[...] = jnp.zeros(acc_ref.shape, F32)
    init = (jnp.full((1, tq), -jnp.inf, F32),) * N_HEADS + (jnp.zeros((1, tq), F32),) * N_HEADS
    carry = lax.fori_loop(0, i, functools.partial(block, masked=False), init)
    carry = block(i, carry, True)
    ot = jnp.concatenate([acc_ref[h] / carry[N_HEADS + h] for h in heads], axis=0)
    ot = ot * lax.rsqrt(jnp.mean(ot * ot, axis=0, keepdims=True) + EPS)
    o_ref[...] = (ot.T * g_ref[...]).astype(BF16)


def _attention(q, k, vt, g, b, s):
    tq = min(512, s)
    nq = s // tq
    q3 = q.reshape(b, s, N_HEADS * QK_PAD)
    k3 = k.reshape(b, s, N_HEADS * QK_PAD)
    out = pl.pallas_call(
        _attn_kernel,
        grid=(b, nq),
        in_specs=[pl.BlockSpec((None, tq, N_HEADS * QK_PAD), lambda bi, i: (bi, i, 0)),
                  pl.BlockSpec((None, s, N_HEADS * QK_PAD), lambda bi, i: (bi, 0, 0)),
                  pl.BlockSpec((None, GROUP_W, s), lambda bi, i: (bi, 0, 0)),
                  pl.BlockSpec((1, GROUP_W), lambda bi, i: (0, 0))],
        out_specs=pl.BlockSpec((None, tq, GROUP_W), lambda bi, i: (bi, i, 0)),
        out_shape=jax.ShapeDtypeStruct((b, s, GROUP_W), BF16),
        scratch_shapes=[pltpu.VMEM((N_HEADS, HEAD_DIM, tq), F32)],
        compiler_params=_params(("parallel", "arbitrary")),
        name="mla_attention",
    )(q3, k3, vt, g)
    return out.reshape(b * s, GROUP_W)


GDN_CONV_K = 4


def _gdn_prep_kernel(q_ref, k_ref, v_ref, a_ref, w_ref, alog_ref, dtb_ref,
                     qo_ref, ko_ref, kbo_ref, vbo_ref, la128_ref, la64_ref, buf):
    s = pl.program_id(1)
    ts = q_ref.shape[0]
    w = w_ref[...]
    gx = a_ref[:, MLA_Q_RANK:MLA_Q_RANK + LANES].astype(F32)
    xs = gx + dtb_ref[...]
    softplus = jnp.maximum(xs, 0.0) + jnp.log1p(jnp.exp(-jnp.abs(xs)))
    log_a = -jnp.exp(alog_ref[...]) * softplus
    beta = _sigmoid(gx)
    lane = lax.broadcasted_iota(jnp.int32, (ts, LANES), 1)
    la_b = [jnp.broadcast_to(log_a[:, h:h + 1], (ts, LANES)) for h in range(N_HEADS)]
    beta_b = [jnp.broadcast_to(beta[:, N_HEADS + h:N_HEADS + h + 1], (ts, HEAD_DIM)) for h in range(N_HEADS)]
    for h in range(N_HEADS):
        la128_ref[:, h * HEAD_DIM:(h + 1) * HEAD_DIM] = la_b[h]
    for pr in range(N_HEADS // 2):
        la64_ref[:, pr * LANES:(pr + 1) * LANES] = jnp.where(lane < CHUNK, la_b[2 * pr], la_b[2 * pr + 1])
    for idx, src in enumerate((q_ref, k_ref, v_ref)):
        @pl.when(s == 0)
        def _(idx=idx):
            buf[idx, 0:SUBLANES, :] = jnp.zeros((SUBLANES, GROUP_W), F32)

        @pl.when(s > 0)
        def _(idx=idx):
            buf[idx, 0:SUBLANES, :] = buf[idx, ts:ts + SUBLANES, :]

        buf[idx, SUBLANES:ts + SUBLANES, :] = src[...].astype(F32)
        wi = w[:, idx * GROUP_W:(idx + 1) * GROUP_W]
        y = wi[3:4] * buf[idx, SUBLANES:ts + SUBLANES, :]
        for j in range(1, GDN_CONV_K):
            y = y + wi[3 - j:4 - j] * buf[idx, SUBLANES - j:ts + SUBLANES - j, :]
        y = y * _sigmoid(y)
        for h in range(N_HEADS):
            hs = slice(h * HEAD_DIM, (h + 1) * HEAD_DIM)
            yh = y[:, hs]
            if idx < 2:
                yh = yh * lax.rsqrt(jnp.sum(yh * yh, axis=-1, keepdims=True) + EPS)
            if idx == 0:
                qo_ref[:, hs] = (yh * (HEAD_DIM ** -0.5)).astype(BF16)
            elif idx == 1:
                ko_ref[:, hs] = yh.astype(BF16)
                kbo_ref[:, hs] = (yh * beta_b[h]).astype(BF16)
            else:
                vbo_ref[:, hs] = (yh * beta_b[h]).astype(BF16)


def _gdn_prep(proj, w, alog, dtb, b, s):
    ts = _seq_tile(s)
    ns = s // ts

    def seg(k):
        return pl.BlockSpec((None, ts, SEG_W), lambda bi, si: (k, bi * ns + si, 0))

    def rows(wd):
        return pl.BlockSpec((ts, wd), lambda bi, si: (bi * ns + si, 0))

    def full(x):
        return pl.BlockSpec(x.shape, lambda bi, si: (0,) * x.ndim)

    n = b * s
    return pl.pallas_call(
        _gdn_prep_kernel,
        grid=(b, ns),
        in_specs=[seg(SEG_G_Q), seg(SEG_G_K), seg(SEG_G_V), seg(SEG_MLA_Q),
                  full(w), full(alog), full(dtb)],
        out_specs=[rows(GROUP_W)] * 5 + [rows(N_HEADS * CHUNK)],
        out_shape=[jax.ShapeDtypeStruct((n, GROUP_W), BF16)] * 4 + [jax.ShapeDtypeStruct((n, GROUP_W), F32)]
        + [jax.ShapeDtypeStruct((n, N_HEADS * CHUNK), F32)],
        scratch_shapes=[pltpu.VMEM((3, ts + SUBLANES, GROUP_W), F32)],
        compiler_params=_params(("parallel", "arbitrary")),
        name="gdn_prep",
    )(proj, proj, proj, proj, w, alog, dtb)


GDN_GROUP = 4


def _split3(x):
    hi = x.astype(BF16)
    r1 = x - hi.astype(F32)
    mid = r1.astype(BF16)
    lo = (r1 - mid.astype(F32)).astype(BF16)
    return jnp.concatenate([hi, mid, lo], axis=0)


def _gdn_chunk_kernel(q_ref, k_ref, kb_ref, vb_ref, la128_ref, la64_ref, z_ref, ng_ref, o_ref,
                      st_ref, l_s, at_s, rhs_s, wq_s, u_s, kd_s, egl_s):
    @pl.when(pl.program_id(1) == 0)
    def _():
        st_ref[...] = jnp.zeros(st_ref.shape, F32)

    c = CHUNK
    n_chunks = q_ref.shape[0] // c
    pw = N_HEADS * c
    row = lax.broadcasted_iota(jnp.int32, (c, pw), 0)
    col = lax.broadcasted_iota(jnp.int32, (c, pw), 1) % c
    tri = col <= row
    strict = col < row
    eye_f = (col == row).astype(F32)
    diag8 = (col // SUBLANES) == (row // SUBLANES)
    merge_levels = (8, 16, 32)
    merge_masks = [((row & m) != 0) & ((col & m) == 0) & ((row // (2 * m)) == (col // (2 * m)))
                   for m in merge_levels]
    lane_head_p = lax.broadcasted_iota(jnp.int32, (c, pw), 1) // c
    head_sel_p = [(lane_head_p == h).astype(BF16) for h in range(N_HEADS)]
    lane_head_w = lax.broadcasted_iota(jnp.int32, (c, GROUP_W), 1) // HEAD_DIM
    head_sel_w = [(lane_head_w == h).astype(BF16) for h in range(N_HEADS)]
    row3 = lax.broadcasted_iota(jnp.int32, (c, 3 * c), 0)
    col3 = lax.broadcasted_iota(jnp.int32, (c, 3 * c), 1)
    tri3 = ((col3 % c) <= row3).astype(BF16)
    zeros_half = jnp.zeros((c, 2 * HEAD_DIM), BF16)
    ng = ng_ref[...]

    def block_diag(x_b):
        return jnp.concatenate([x_b * head_sel_p[h] for h in range(N_HEADS)], axis=0)

    def pad_rows(y_b, second):
        z = zeros_half[:, :y_b.shape[1]]
        return jnp.concatenate([z, y_b] if second else [y_b, z], axis=0)

    def interleave(gens):
        gens = list(gens)
        while gens:
            alive = []
            for g in gens:
                try:
                    next(g)
                    alive.append(g)
                except StopIteration:
                    pass
            gens = alive

    def prepare(ci):
        r0 = pl.multiple_of(ci * c, c)
        rows_c = pl.ds(r0, c)
        dlt = _dot(tri3, _split3(jnp.where(strict, la64_ref[rows_c, :], 0.0)))
        gc = _dot(tri3, _split3(la128_ref[rows_c, :]))
        yield
        decay = jnp.exp(jnp.where(tri, dlt, -jnp.inf))
        eg = jnp.exp(gc)
        kb = kb_ref[rows_c, :]
        k_b = k_ref[rows_c, :]
        q_b = q_ref[rows_c, :]
        k_rows = jnp.concatenate([k_b * head_sel_w[h] for h in range(N_HEADS)], axis=0)
        qk = _dot_nt(jnp.concatenate([kb, q_b], axis=0), k_rows)
        yield
        l_s[ci] = jnp.where(strict, qk[:c] * decay, 0.0).astype(BF16)
        at_s[ci] = jnp.where(tri, qk[c:] * decay, 0.0).astype(BF16)
        kbe = kb.astype(F32) * eg
        qg = (q_b.astype(F32) * eg).astype(BF16)
        gl = gc[c - 1:c, :]
        kd = (k_b.astype(F32) * jnp.exp(gl - gc)).astype(BF16)
        vb = vb_ref[rows_c, :]
        for h in range(N_HEADS):
            hs = slice(h * HEAD_DIM, (h + 1) * HEAD_DIM)
            rhs_s[ci, h, :, 0:HEAD_DIM] = vb[:, hs]
            rhs_s[ci, h, :, HEAD_DIM:2 * HEAD_DIM] = kbe[:, hs].astype(BF16)
            wq_s[ci, h, c:2 * c, :] = qg[:, hs]
            kd_s[ci, h] = kd[:, hs]
        egl_s[ci] = jnp.broadcast_to(jnp.exp(gl), (SUBLANES, GROUP_W))

    def invert(ci):
        lf = l_s[ci].astype(F32)
        n8 = jnp.where(diag8, -lf, 0.0)
        t = eye_f + n8
        p = n8.astype(BF16)
        p = _dot(p, block_diag(p)).astype(BF16)
        yield
        pt = _dot(jnp.concatenate([p, t.astype(BF16)], axis=0), block_diag(p))
        t = t + pt[c:]
        p = pt[:c].astype(BF16)
        yield
        t = t + _dot(t.astype(BF16), block_diag(p))
        yield
        for mask in merge_masks:
            off = jnp.where(mask, lf, 0.0).astype(BF16)
            a = _dot(t.astype(BF16), block_diag(off)).astype(BF16)
            yield
            t = t - _dot(a, block_diag(t.astype(BF16)))
            yield
        tb = t.astype(BF16)
        for h in range(N_HEADS):
            pr = slice((h // 2) * LANES, (h // 2 + 1) * LANES)
            x = _dot(tb[:, pr], pad_rows(rhs_s[ci, h], h % 2))
            u_s[ci, h] = x[:, :HEAD_DIM]
            wq_s[ci, h, 0:c, :] = x[:, HEAD_DIM:].astype(BF16)
        yield

    def recur(cis):
        for ci in cis:
            r0 = pl.multiple_of(ci * c, c)
            at = at_s[ci]
            egl = egl_s[ci]
            sts = [st_ref[h] for h in range(N_HEADS)]
            wss = [_dot(wq_s[ci, h], sts[h].astype(BF16)) for h in range(N_HEADS)]
            yield
            vnbs = [(u_s[ci, h] - wss[h][:c]).astype(BF16) for h in range(N_HEADS)]
            for h in range(N_HEADS):
                hs = slice(h * HEAD_DIM, (h + 1) * HEAD_DIM)
                pr = slice((h // 2) * LANES, (h // 2 + 1) * LANES)
                out = wss[h][c:] + _dot(at[:, pr], pad_rows(vnbs[h], h % 2))
                st_ref[h] = sts[h] * egl[0:1, hs] + _dot_tn(kd_s[ci, h], vnbs[h])
                zh = z_ref[pl.ds(r0, c), hs].astype(F32)
                o_ref[pl.ds(r0, c), hs] = (_rms(out, ng) * (zh * _sigmoid(zh))).astype(BF16)
            yield

    def prepare_pair(j, carry):
        interleave([prepare(2 * j), prepare(2 * j + 1)])
        return carry

    lax.fori_loop(0, n_chunks // 2, prepare_pair, 0)
    interleave(invert(i) for i in range(GDN_GROUP))

    def group(j, carry):
        c0 = j * GDN_GROUP
        interleave([recur([c0 + i for i in range(GDN_GROUP)])]
                   + [invert(c0 + GDN_GROUP + i) for i in range(GDN_GROUP)])
        return carry

    lax.fori_loop(0, n_chunks // GDN_GROUP - 1, group, 0)
    interleave([recur([n_chunks - GDN_GROUP + i for i in range(GDN_GROUP)])])


def _gdn_chunk(q, k, kb, vb, la128, la64, proj, ng, b, s):
    ts = _seq_tile(s)
    ns = s // ts
    nc = ts // CHUNK
    assert nc % GDN_GROUP == 0

    def rows(wd):
        return pl.BlockSpec((ts, wd), lambda bi, si: (bi * ns + si, 0))

    return pl.pallas_call(
        _gdn_chunk_kernel,
        grid=(b, ns),
        in_specs=[rows(GROUP_W)] * 5 + [rows(N_HEADS * CHUNK),
                  pl.BlockSpec((None, ts, SEG_W), lambda bi, si: (SEG_G_Z, bi * ns + si, 0)),
                  pl.BlockSpec((1, HEAD_DIM), lambda bi, si: (0, 0))],
        out_specs=rows(GROUP_W),
        out_shape=jax.ShapeDtypeStruct((b * s, GROUP_W), BF16),
        scratch_shapes=[pltpu.VMEM((N_HEADS, HEAD_DIM, HEAD_DIM), F32),
                        pltpu.VMEM((nc, CHUNK, N_HEADS * CHUNK), BF16),
                        pltpu.VMEM((nc, CHUNK, N_HEADS * CHUNK), BF16),
                        pltpu.VMEM((nc, N_HEADS, CHUNK, 2 * HEAD_DIM), BF16),
                        pltpu.VMEM((nc, N_HEADS, 2 * CHUNK, HEAD_DIM), BF16),
                        pltpu.VMEM((nc, N_HEADS, CHUNK, HEAD_DIM), F32),
                        pltpu.VMEM((nc, N_HEADS, CHUNK, HEAD_DIM), BF16),
                        pltpu.VMEM((nc, SUBLANES, GROUP_W), F32)],
        compiler_params=_params(("parallel", "arbitrary")),
        name="gdn_chunk",
    )(q, k, kb, vb, la128, la64, proj, ng)


def _hgrn_kernel(q_ref, f_ref, i_ref, z_ref, lbl_ref, ng_ref, o_ref, st_ref, *, layer):
    @pl.when(pl.program_id(1) == 0)
    def _():
        st_ref[...] = jnp.zeros(st_ref.shape, F32)

    c = CHUNK
    n_chunks = q_ref.shape[0] // c
    logits = lbl_ref[...]
    e = jnp.exp(logits - jnp.max(logits, axis=0, keepdims=True))
    p = e / jnp.sum(e, axis=0, keepdims=True)
    lb = jnp.sum(p[0:layer + 1], axis=0, keepdims=True) - p[0:1]
    ng = ng_ref[...]

    tt = lax.broadcasted_iota(jnp.int32, (c, c), 0)
    ss = lax.broadcasted_iota(jnp.int32, (c, c), 1)
    trow = lax.broadcasted_iota(jnp.int32, (c, HEAD_DIM), 0)
    levels = (1, 2, 4, 8, 16, 32)
    pair_masks = [((tt & m) != 0) & ((ss & m) == 0) & ((tt // (2 * m)) == (ss // (2 * m))) for m in levels]
    odd_rows = [(trow & m) != 0 for m in levels]
    eye = tt == ss

    def chunk(ci, carry):
        r0 = pl.multiple_of(ci * c, c)

        def head(h):
            hs = slice(h * HEAD_DIM, (h + 1) * HEAD_DIM)
            fr = f_ref[pl.ds(r0, c), hs].astype(F32)
            rq = q_ref[pl.ds(r0, c), hs].astype(F32)
            ii = i_ref[pl.ds(r0, c), hs]
            lbh = lb[:, hs]
            lf = jnp.log(lbh + (1.0 - lbh) * _sigmoid(fr))
            kk = (1.0 - lbh) * _sigmoid(-fr)
            qq = rq * _sigmoid(rq)
            attn = jnp.where(eye, _dot_nt(qq.astype(BF16), kk.astype(BF16)), 0.0)
            yield
            cs = lf
            tot = lf
            for m, pmask, odd in zip(levels, pair_masks, odd_rows):
                qm = (qq * jnp.exp(cs)).astype(BF16)
                km = (kk * jnp.exp(tot - cs)).astype(BF16)
                attn = jnp.where(pmask, _dot_nt(qm, km), attn)
                prev = pltpu.roll(tot, m, axis=0)
                nxt = pltpu.roll(tot, c - m, axis=0)
                cs = cs + jnp.where(odd, prev, 0.0)
                tot = tot + jnp.where(odd, prev, nxt)
                yield
            st = st_ref[h]
            out = _dot_nt((qq * jnp.exp(cs)).astype(BF16), st.astype(BF16)) + _dot(attn.astype(BF16), ii)
            kd = (kk * jnp.exp(tot - cs)).astype(BF16)
            st_ref[h] = st * jnp.exp(tot[0:1, :]) + _dot_tn(ii, kd)
            yield
            zh = z_ref[pl.ds(r0, c), hs].astype(F32)
            o_ref[pl.ds(r0, c), hs] = (_rms(out, ng) * _sigmoid(zh)).astype(BF16)

        gens = [head(h) for h in range(N_HEADS)]
        while gens:
            alive = []
            for g in gens:
                try:
                    next(g)
                    alive.append(g)
                except StopIteration:
                    pass
            gens = alive
        return carry

    lax.fori_loop(0, n_chunks, chunk, 0)


def _hgrn(proj, lbl, ng, layer, b, s):
    ts = _seq_tile(s)
    ns = s // ts

    def seg(k):
        return pl.BlockSpec((None, ts, SEG_W), lambda bi, si: (k, bi * ns + si, 0))

    return pl.pallas_call(
        functools.partial(_hgrn_kernel, layer=layer),
        grid=(b, ns),
        in_specs=[seg(SEG_R_Q), seg(SEG_R_F), seg(SEG_R_I), seg(SEG_R_Z),
                  pl.BlockSpec(lbl.shape, lambda bi, si: (0, 0)),
                  pl.BlockSpec((1, HEAD_DIM), lambda bi, si: (0, 0))],
        out_specs=pl.BlockSpec((ts, GROUP_W), lambda bi, si: (bi * ns + si, 0)),
        out_shape=jax.ShapeDtypeStruct((b * s, GROUP_W), BF16),
        scratch_shapes=[pltpu.VMEM((N_HEADS, HEAD_DIM, HEAD_DIM), F32)],
        compiler_params=_params(("parallel", "arbitrary")),
        name="hgrn2",
    )(proj, proj, proj, proj, lbl, ng)


def _mix_mlp_kernel(ya_ref, yb_ref, yc_ref, yd_ref, wo_ref, h_ref, g_ref, w1_ref, w2_ref, fg_ref,
                    o_ref, xn_ref, *, final_norm):
    f = pl.program_id(1)

    @pl.when(f == 0)
    def _():
        y = jnp.concatenate([ya_ref[...], yb_ref[...], yc_ref[...], yd_ref[...]], axis=1)
        h1 = h_ref[...] + _dot(y, wo_ref[...])
        xn_ref[...] = _rms(h1, g_ref[...]).astype(BF16)
        o_ref[...] = h1

    a = jnp.maximum(_dot(xn_ref[...], w1_ref[...]), 0.0)
    o_ref[...] += _dot((a * a).astype(BF16), w2_ref[...])

    if final_norm:
        @pl.when(f == pl.num_programs(1) - 1)
        def _():
            o_ref[...] = _rms(o_ref[...], fg_ref[...])


def _mix_mlp(ys, wo, h, g, w1, w2, fg, layer, final_norm):
    n, d = h.shape
    dff = w1.shape[2]
    tm = min(512, n)
    tf = 512
    ysp = pl.BlockSpec((tm, GROUP_W), lambda i, f: (i, 0))
    rows = pl.BlockSpec((tm, d), lambda i, f: (i, 0))
    vec = pl.BlockSpec((1, d), lambda i, f: (0, 0))
    return pl.pallas_call(
        functools.partial(_mix_mlp_kernel, final_norm=final_norm),
        grid=(n // tm, dff // tf),
        in_specs=[ysp, ysp, ysp, ysp,
                  pl.BlockSpec((None, d, d), lambda i, f: (layer, 0, 0), pipeline_mode=pl.Buffered(1)),
                  rows, vec,
                  pl.BlockSpec((None, d, tf), lambda i, f: (layer, 0, f)),
                  pl.BlockSpec((None, tf, d), lambda i, f: (layer, f, 0)),
                  vec],
        out_specs=rows,
        out_shape=jax.ShapeDtypeStruct((n, d), F32),
        scratch_shapes=[pltpu.VMEM((tm, d), BF16)],
        compiler_params=_params(("parallel", "arbitrary")),
        name="mix_mlp",
    )(*ys, wo, h, g, w1, w2, fg)


def _pack_w_in(w_in):
    w_in = w_in.astype(BF16)
    widths = (GROUP_W, GROUP_W, GROUP_W, MLA_Q_RANK, MLA_KV_RANK, MLA_ROPE,
              GROUP_W, GROUP_W, GROUP_W, GROUP_W, N_HEADS, N_HEADS,
              GROUP_W, GROUP_W, GROUP_W, GROUP_W)
    cols, start = [], 0
    for wd in widths:
        cols.append(w_in[..., start:start + wd])
        start += wd
    (sc_x, sc_c, sc_b, m_cq, m_ckv, m_kr, g_q, g_k, g_v, g_z, g_a, g_b, r_q, r_f, r_i, r_z) = cols
    lead = w_in.shape[:-1]

    def zeros(wd):
        return jnp.zeros(lead + (wd,), w_in.dtype)

    half = MLA_ROPE // 2
    kr_swapped = jnp.concatenate([m_kr[..., half:], m_kr[..., :half]], axis=-1)
    seg_q = jnp.concatenate([m_cq, g_a, g_b, zeros(SEG_W - MLA_Q_RANK - 2 * N_HEADS)], axis=-1)
    seg_kv = jnp.concatenate([m_ckv, m_kr, zeros(LANES - MLA_ROPE), kr_swapped, zeros(LANES - MLA_ROPE)], axis=-1)
    segs = [sc_x, sc_c, sc_b, g_q, g_k, g_v, g_z, r_q, r_f, r_i, r_z, seg_q, seg_kv]
    return jnp.concatenate(segs, axis=-1)


def _pack_mla_weights(w_uq, w_ukv):
    l = w_uq.shape[0]
    half = MLA_ROPE // 2
    wq = w_uq.reshape(l, MLA_Q_RANK, N_HEADS, MLA_NOPE + MLA_ROPE)
    nope, rp = wq[..., :MLA_NOPE], wq[..., MLA_NOPE:]
    z64 = jnp.zeros(rp.shape, rp.dtype)
    wqa = jnp.concatenate([nope, rp, z64], axis=-1).reshape(l, MLA_Q_RANK, N_HEADS * QK_PAD)
    rp_sw = jnp.concatenate([rp[..., half:], rp[..., :half]], axis=-1)
    wqb = jnp.concatenate([rp_sw, z64], axis=-1).reshape(l, MLA_Q_RANK, N_HEADS * LANES)
    wkv = w_ukv.reshape(l, MLA_KV_RANK, N_HEADS, MLA_NOPE + HEAD_DIM)
    wk = wkv[..., :MLA_NOPE].reshape(l, MLA_KV_RANK, N_HEADS * MLA_NOPE)
    wv = wkv[..., MLA_NOPE:].reshape(l, MLA_KV_RANK, N_HEADS * HEAD_DIM)
    return wqa.astype(BF16), wqb.astype(BF16), wk.astype(BF16), jnp.swapaxes(wv, 1, 2).astype(BF16)


def _pad_lanes(x):
    l, k = x.shape
    return jnp.concatenate([x, jnp.zeros((l, LANES - k), x.dtype)], axis=-1).reshape(l, 1, LANES)


def kernel(x, positions, norm1_g, w_in, sconv_w, sconv_out_g, mla_q_g, mla_kv_g, mla_w_uq, mla_w_ukv,
           mla_out_g, gdn_conv_w, gdn_a_log, gdn_dt_bias, gdn_norm_g, hgrn_lb_logits, hgrn_norm_g,
           w_o, norm2_g, w_ff1, w_ff2, final_g):
    b, s, d = x.shape
    n = b * s
    depth = w_in.shape[0]

    w_in_p = _pack_w_in(w_in)
    wqa, wqb, wk, wvt = _pack_mla_weights(mla_w_uq, mla_w_ukv)
    w_o_b = w_o.astype(BF16)
    w1_b = w_ff1.astype(BF16)
    w2_b = w_ff2.astype(BF16)
    alog_p = _pad_lanes(gdn_a_log)
    dtb_p = _pad_lanes(gdn_dt_bias)

    half = MLA_ROPE // 2
    inv_freq = ROPE_THETA ** (-jnp.arange(half, dtype=F32) / half)
    invf = jnp.concatenate([inv_freq, inv_freq, jnp.zeros((LANES - MLA_ROPE,), F32)]).reshape(1, LANES)
    cos, sin = _rope_tables(positions.astype(F32).reshape(n, 1), invf)

    h = x.reshape(n, d)
    for l in range(depth):
        proj = _in_proj(h, norm1_g[l].reshape(1, d), w_in_p, l)
        y_sc = _sconv(proj, sconv_w[l], sconv_out_g[l].reshape(1, GROUP_W), b, s)
        q, k, vt = _mla_up(proj, cos, sin, mla_q_g[l].reshape(1, -1), mla_kv_g[l].reshape(1, -1),
                           wqa[l], wqb[l], wk[l], wvt[l], b, s)
        y_mla = _attention(q, k, vt, mla_out_g[l].reshape(1, GROUP_W), b, s)
        gdn_in = _gdn_prep(proj, gdn_conv_w[l], alog_p[l], dtb_p[l], b, s)
        y_gdn = _gdn_chunk(*gdn_in, proj, gdn_norm_g[l].reshape(1, HEAD_DIM), b, s)
        y_hg = _hgrn(proj, hgrn_lb_logits, hgrn_norm_g[l].reshape(1, HEAD_DIM), l, b, s)
        h = _mix_mlp((y_sc, y_mla, y_gdn, y_hg), w_o_b, h, norm2_g[l].reshape(1, d), w1_b, w2_b,
                     final_g.reshape(1, d), l, final_norm=(l == depth - 1))
    return h.reshape(b, s, d)
```

```python
import functools

import jax
import jax.numpy as jnp
from jax import lax
from jax.experimental import pallas as pl
from jax.experimental.pallas import tpu as pltpu

F32 = jnp.float32
BF16 = jnp.bfloat16
HIGHEST = lax.Precision.HIGHEST

D_MODEL = 2048
GROUP_W = 512
HEAD_DIM = 128
N_HEADS = 4
MLA_Q_RANK = 384
MLA_KV_RANK = 256
MLA_NOPE = 128
MLA_ROPE = 64
ROPE_THETA = 10000.0
CHUNK = 64
D_FF = 4 * D_MODEL
EPS = 1e-6
LOG2_E = 1.4426950408889634

SEG_W = 512
(SEG_SC_X, SEG_SC_C, SEG_SC_B, SEG_G_Q, SEG_G_K, SEG_G_V, SEG_G_Z,
 SEG_R_Q, SEG_R_F, SEG_R_I, SEG_R_Z, SEG_MLA_Q, SEG_MLA_KV) = range(13)
N_SEG = 13
QK_PAD = 256
LANES = 128
SUBLANES = 8
VMEM_LIMIT = 56 * 1024 * 1024


def _params(sem, vmem=VMEM_LIMIT):
    return pltpu.CompilerParams(dimension_semantics=sem, vmem_limit_bytes=vmem)


def _dot(a, b, precision=None):
    return jnp.dot(a, b, preferred_element_type=F32, precision=precision)


def _dot_nt(a, b, precision=None):
    return lax.dot_general(a, b, (((1,), (1,)), ((), ())), preferred_element_type=F32, precision=precision)


def _dot_tn(a, b, precision=None):
    return lax.dot_general(a, b, (((0,), (0,)), ((), ())), preferred_element_type=F32, precision=precision)


def _rms(x, g):
    return x * lax.rsqrt(jnp.mean(x * x, axis=-1, keepdims=True) + EPS) * g


def _sigmoid(x):
    return jax.nn.sigmoid(x)


def _inproj_kernel(x_ref, g_ref, w_ref, o_ref, xn_ref):
    @pl.when(pl.program_id(1) == 0)
    def _():
        xn_ref[...] = _rms(x_ref[...], g_ref[...]).astype(BF16)

    o_ref[...] = _dot(xn_ref[...], w_ref[...]).astype(o_ref.dtype)


def _in_proj(h, g, w, layer):
    n, d = h.shape
    tm = min(1024, n)
    return pl.pallas_call(
        _inproj_kernel,
        grid=(n // tm, N_SEG),
        in_specs=[pl.BlockSpec((tm, d), lambda i, j: (i, 0)),
                  pl.BlockSpec((1, d), lambda i, j: (0, 0)),
                  pl.BlockSpec((None, d, SEG_W), lambda i, j: (layer, 0, j))],
        out_specs=pl.BlockSpec((None, tm, SEG_W), lambda i, j: (j, i, 0)),
        out_shape=jax.ShapeDtypeStruct((N_SEG, n, SEG_W), BF16),
        scratch_shapes=[pltpu.VMEM((tm, d), BF16)],
        compiler_params=_params(("parallel", "arbitrary")),
        name="in_proj",
    )(h, g, w)


def _sconv_kernel(x_ref, c_ref, b_ref, w_ref, g_ref, o_ref, zbuf):
    s = pl.program_id(1)
    ts = x_ref.shape[0]

    @pl.when(s == 0)
    def _():
        zbuf[0:SUBLANES, :] = jnp.zeros((SUBLANES, GROUP_W), F32)

    @pl.when(s > 0)
    def _():
        zbuf[0:SUBLANES, :] = zbuf[ts:ts + SUBLANES, :]

    zbuf[SUBLANES:ts + SUBLANES, :] = c_ref[...].astype(F32) * x_ref[...].astype(F32)
    w = w_ref[...]
    y = (w[2:3] * zbuf[SUBLANES:ts + SUBLANES, :]
         + w[1:2] * zbuf[SUBLANES - 1:ts + SUBLANES - 1, :]
         + w[0:1] * zbuf[SUBLANES - 2:ts + SUBLANES - 2, :])
    o_ref[...] = _rms(b_ref[...].astype(F32) * y, g_ref[...]).astype(BF16)


def _seq_tile(s):
    return min(512, s)


def _sconv(proj, w, g, b, s):
    ts = _seq_tile(s)
    ns = s // ts

    def seg(k):
        return pl.BlockSpec((None, ts, SEG_W), lambda bi, si: (k, bi * ns + si, 0))

    return pl.pallas_call(
        _sconv_kernel,
        grid=(b, ns),
        in_specs=[seg(SEG_SC_X), seg(SEG_SC_C), seg(SEG_SC_B),
                  pl.BlockSpec((3, GROUP_W), lambda bi, si: (0, 0)),
                  pl.BlockSpec((1, GROUP_W), lambda bi, si: (0, 0))],
        out_specs=pl.BlockSpec((ts, GROUP_W), lambda bi, si: (bi * ns + si, 0)),
        out_shape=jax.ShapeDtypeStruct((b * s, GROUP_W), BF16),
        scratch_shapes=[pltpu.VMEM((ts + SUBLANES, GROUP_W), F32)],
        compiler_params=_params(("parallel", "arbitrary")),
        name="sconv",
    )(proj, proj, proj, w, g)


def _rope_kernel(pos_ref, invf_ref, cos_ref, sin_ref):
    ang = pos_ref[...] * invf_ref[...]
    lane = lax.broadcasted_iota(jnp.int32, ang.shape, 1)
    half = MLA_ROPE // 2
    cos_ref[...] = jnp.where(lane < MLA_ROPE, jnp.cos(ang), 0.0)
    sn = jnp.sin(ang)
    sin_ref[...] = jnp.where(lane < half, -sn, jnp.where(lane < MLA_ROPE, sn, 0.0))


def _rope_tables(pos_col, invf):
    n = pos_col.shape[0]
    tm = min(2048, n)
    return pl.pallas_call(
        _rope_kernel,
        grid=(n // tm,),
        in_specs=[pl.BlockSpec((tm, 1), lambda i: (i, 0)),
                  pl.BlockSpec((1, LANES), lambda i: (0, 0))],
        out_specs=[pl.BlockSpec((tm, LANES), lambda i: (i, 0))] * 2,
        out_shape=[jax.ShapeDtypeStruct((n, LANES), F32)] * 2,
        compiler_params=_params(("parallel",)),
        name="rope_tables",
    )(pos_col, invf)


def _mla_up_kernel(a_ref, b_ref, cos_ref, sin_ref, gq_ref, gkv_ref,
                   wqa_ref, wqb_ref, wk_ref, wvt_ref, q_ref, k_ref, vt_ref):
    scale = (MLA_NOPE + MLA_ROPE) ** -0.5 * LOG2_E
    cos = cos_ref[...]
    sin = sin_ref[...]
    a = a_ref[...].astype(F32)
    cqn = _rms(a[:, :MLA_Q_RANK], gq_ref[...]).astype(BF16)
    qa = _dot(cqn, wqa_ref[...])
    qb = _dot(cqn, wqb_ref[...])
    for h in range(N_HEADS):
        o = h * QK_PAD
        q_ref[:, o:o + LANES] = (qa[:, o:o + LANES] * scale).astype(BF16)
        roped = qa[:, o + LANES:o + QK_PAD] * cos + qb[:, h * LANES:(h + 1) * LANES] * sin
        q_ref[:, o + LANES:o + QK_PAD] = (roped * scale).astype(BF16)
    bb = b_ref[...].astype(F32)
    ckvn = _rms(bb[:, :MLA_KV_RANK], gkv_ref[...]).astype(BF16)
    kk = _dot(ckvn, wk_ref[...])
    vt_ref[...] = _dot_nt(wvt_ref[...], ckvn).astype(BF16)
    kr = (bb[:, MLA_KV_RANK:MLA_KV_RANK + LANES] * cos
          + bb[:, MLA_KV_RANK + LANES:MLA_KV_RANK + 2 * LANES] * sin).astype(BF16)
    for h in range(N_HEADS):
        o = h * QK_PAD
        k_ref[:, o:o + LANES] = kk[:, h * LANES:(h + 1) * LANES].astype(BF16)
        k_ref[:, o + LANES:o + QK_PAD] = kr


def _mla_up(proj, cos, sin, gq, gkv, wqa, wqb, wk, wvt, b, s):
    n = proj.shape[1]
    tm = min(512, s)
    ns = s // tm

    def seg(k):
        return pl.BlockSpec((None, tm, SEG_W), lambda i: (k, i, 0))

    def full(x):
        return pl.BlockSpec(x.shape, lambda i: (0,) * x.ndim)

    def rows(w):
        return pl.BlockSpec((tm, w), lambda i: (i, 0))

    return pl.pallas_call(
        _mla_up_kernel,
        grid=(n // tm,),
        in_specs=[seg(SEG_MLA_Q), seg(SEG_MLA_KV), rows(LANES), rows(LANES),
                  full(gq), full(gkv), full(wqa), full(wqb), full(wk), full(wvt)],
        out_specs=[rows(N_HEADS * QK_PAD), rows(N_HEADS * QK_PAD),
                   pl.BlockSpec((None, GROUP_W, tm), lambda i: (i // ns, 0, i % ns))],
        out_shape=[jax.ShapeDtypeStruct((n, N_HEADS * QK_PAD), BF16),
                   jax.ShapeDtypeStruct((n, N_HEADS * QK_PAD), BF16),
                   jax.ShapeDtypeStruct((b, GROUP_W, s), BF16)],
        compiler_params=_params(("parallel",)),
        name="mla_up",
    )(proj, proj, cos, sin, gq, gkv, wqa, wqb, wk, wvt)


def _attn_kernel(q_ref, k_ref, vt_ref, g_ref, o_ref, acc_ref):
    i = pl.program_id(1)
    tq = q_ref.shape[0]
    kv_i = lax.broadcasted_iota(jnp.int32, (tq, tq), 0)
    q_i = lax.broadcasted_iota(jnp.int32, (tq, tq), 1)
    causal = kv_i <= q_i
    heads = range(N_HEADS)

    def block(j, carry, masked):
        start = pl.multiple_of(j * tq, tq)
        sc = [_dot_nt(k_ref[pl.ds(start, tq), h * QK_PAD:(h + 1) * QK_PAD],
                      q_ref[:, h * QK_PAD:(h + 1) * QK_PAD]) for h in heads]
        if masked:
            sc = [jnp.where(causal, x, -jnp.inf) for x in sc]
        m_old = carry[:N_HEADS]
        l_old = carry[N_HEADS:]
        m_new = [jnp.maximum(m, jnp.max(x, axis=0, keepdims=True)) for m, x in zip(m_old, sc)]
        alpha = [jnp.exp2(mo - mn) for mo, mn in zip(m_old, m_new)]
        p = [jnp.exp2(x - mn) for x, mn in zip(sc, m_new)]
        l_new = [a * l + jnp.sum(x, axis=0, keepdims=True) for a, l, x in zip(alpha, l_old, p)]
        pv = [_dot(vt_ref[h * HEAD_DIM:(h + 1) * HEAD_DIM, pl.ds(start, tq)], x.astype(BF16))
              for h, x in zip(heads, p)]
        for h in heads:
            acc_ref[h] = alpha[h] * acc_ref[h] + pv[h]
        return tuple(m_new) + tuple(l_new)

    acc_ref[...] = jnp.zeros(acc_ref.shape, F32)
    init = (jnp.full((1, tq), -jnp.inf, F32),) * N_HEADS + (jnp.zeros((1, tq), F32),) * N_HEADS
    carry = lax.fori_loop(0, i, functools.partial(block, masked=False), init)
    carry = block(i, carry, True)
    ot = jnp.concatenate([acc_ref[h] / carry[N_HEADS + h] for h in heads], axis=0)
    ot = ot * lax.rsqrt(jnp.mean(ot * ot, axis=0, keepdims=True) + EPS)
    o_ref[...] = (ot.T * g_ref[...]).astype(BF16)


def _attention(q, k, vt, g, b, s):
    tq = min(512, s)
    nq = s // tq
    q3 = q.reshape(b, s, N_HEADS * QK_PAD)
    k3 = k.reshape(b, s, N_HEADS * QK_PAD)
    out = pl.pallas_call(
        _attn_kernel,
        grid=(b, nq),
        in_specs=[pl.BlockSpec((None, tq, N_HEADS * QK_PAD), lambda bi, i: (bi, i, 0)),
                  pl.BlockSpec((None, s, N_HEADS * QK_PAD), lambda bi, i: (bi, 0, 0)),
                  pl.BlockSpec((None, GROUP_W, s), lambda bi, i: (bi, 0, 0)),
                  pl.BlockSpec((1, GROUP_W), lambda bi, i: (0, 0))],
        out_specs=pl.BlockSpec((None, tq, GROUP_W), lambda bi, i: (bi, i, 0)),
        out_shape=jax.ShapeDtypeStruct((b, s, GROUP_W), BF16),
        scratch_shapes=[pltpu.VMEM((N_HEADS, HEAD_DIM, tq), F32)],
        compiler_params=_params(("parallel", "arbitrary")),
        name="mla_attention",
    )(q3, k3, vt, g)
    return out.reshape(b * s, GROUP_W)


GDN_CONV_K = 4


def _gdn_prep_kernel(q_ref, k_ref, v_ref, a_ref, w_ref, alog_ref, dtb_ref,
                     qo_ref, ko_ref, kbo_ref, vbo_ref, la128_ref, la64_ref, buf):
    s = pl.program_id(1)
    ts = q_ref.shape[0]
    w = w_ref[...]
    gx = a_ref[:, MLA_Q_RANK:MLA_Q_RANK + LANES].astype(F32)
    xs = gx + dtb_ref[...]
    softplus = jnp.maximum(xs, 0.0) + jnp.log1p(jnp.exp(-jnp.abs(xs)))
    log_a = -jnp.exp(alog_ref[...]) * softplus
    beta = _sigmoid(gx)
    lane = lax.broadcasted_iota(jnp.int32, (ts, LANES), 1)
    la_b = [jnp.broadcast_to(log_a[:, h:h + 1], (ts, LANES)) for h in range(N_HEADS)]
    beta_b = [jnp.broadcast_to(beta[:, N_HEADS + h:N_HEADS + h + 1], (ts, HEAD_DIM)) for h in range(N_HEADS)]
    for h in range(N_HEADS):
        la128_ref[:, h * HEAD_DIM:(h + 1) * HEAD_DIM] = la_b[h]
    for pr in range(N_HEADS // 2):
        la64_ref[:, pr * LANES:(pr + 1) * LANES] = jnp.where(lane < CHUNK, la_b[2 * pr], la_b[2 * pr + 1])
    for idx, src in enumerate((q_ref, k_ref, v_ref)):
        @pl.when(s == 0)
        def _(idx=idx):
            buf[idx, 0:SUBLANES, :] = jnp.zeros((SUBLANES, GROUP_W), F32)

        @pl.when(s > 0)
        def _(idx=idx):
            buf[idx, 0:SUBLANES, :] = buf[idx, ts:ts + SUBLANES, :]

        buf[idx, SUBLANES:ts + SUBLANES, :] = src[...].astype(F32)
        wi = w[:, idx * GROUP_W:(idx + 1) * GROUP_W]
        y = wi[3:4] * buf[idx, SUBLANES:ts + SUBLANES, :]
        for j in range(1, GDN_CONV_K):
            y = y + wi[3 - j:4 - j] * buf[idx, SUBLANES - j:ts + SUBLANES - j, :]
        y = y * _sigmoid(y)
        for h in range(N_HEADS):
            hs = slice(h * HEAD_DIM, (h + 1) * HEAD_DIM)
            yh = y[:, hs]
            if idx < 2:
                yh = yh * lax.rsqrt(jnp.sum(yh * yh, axis=-1, keepdims=True) + EPS)
            if idx == 0:
                qo_ref[:, hs] = (yh * (HEAD_DIM ** -0.5)).astype(BF16)
            elif idx == 1:
                ko_ref[:, hs] = yh.astype(BF16)
                kbo_ref[:, hs] = (yh * beta_b[h]).astype(BF16)
            else:
                vbo_ref[:, hs] = (yh * beta_b[h]).astype(BF16)


def _gdn_prep(proj, w, alog, dtb, b, s):
    ts = _seq_tile(s)
    ns = s // ts

    def seg(k):
        return pl.BlockSpec((None, ts, SEG_W), lambda bi, si: (k, bi * ns + si, 0))

    def rows(wd):
        return pl.BlockSpec((ts, wd), lambda bi, si: (bi * ns + si, 0))

    def full(x):
        return pl.BlockSpec(x.shape, lambda bi, si: (0,) * x.ndim)

    n = b * s
    return pl.pallas_call(
        _gdn_prep_kernel,
        grid=(b, ns),
        in_specs=[seg(SEG_G_Q), seg(SEG_G_K), seg(SEG_G_V), seg(SEG_MLA_Q),
                  full(w), full(alog), full(dtb)],
        out_specs=[rows(GROUP_W)] * 5 + [rows(N_HEADS * CHUNK)],
        out_shape=[jax.ShapeDtypeStruct((n, GROUP_W), BF16)] * 4 + [jax.ShapeDtypeStruct((n, GROUP_W), F32)]
        + [jax.ShapeDtypeStruct((n, N_HEADS * CHUNK), F32)],
        scratch_shapes=[pltpu.VMEM((3, ts + SUBLANES, GROUP_W), F32)],
        compiler_params=_params(("parallel", "arbitrary")),
        name="gdn_prep",
    )(proj, proj, proj, proj, w, alog, dtb)


def _split3(x):
    hi = x.astype(BF16)
    r1 = x - hi.astype(F32)
    mid = r1.astype(BF16)
    lo = (r1 - mid.astype(F32)).astype(BF16)
    return jnp.concatenate([hi, mid, lo], axis=0)


def _gdn_stages(q_ref, k_ref, kb_ref, vb_ref, la128_ref, la64_ref, z_ref, ng_ref, o_ref,
                st_ref, l_s, at_s, rhs_s, wq_s, u_s, kd_s, egl_s):
    c = CHUNK
    pw = N_HEADS * c
    row = lax.broadcasted_iota(jnp.int32, (c, pw), 0)
    col = lax.broadcasted_iota(jnp.int32, (c, pw), 1) % c
    tri = col <= row
    strict = col < row
    eye_f = (col == row).astype(F32)
    diag8 = (col // SUBLANES) == (row // SUBLANES)
    merge_levels = (8, 16, 32)
    merge_masks = [((row & m) != 0) & ((col & m) == 0) & ((row // (2 * m)) == (col // (2 * m)))
                   for m in merge_levels]
    lane_head_p = lax.broadcasted_iota(jnp.int32, (c, pw), 1) // c
    head_sel_p = [(lane_head_p == h).astype(BF16) for h in range(N_HEADS)]
    lane_head_w = lax.broadcasted_iota(jnp.int32, (c, GROUP_W), 1) // HEAD_DIM
    head_sel_w = [(lane_head_w == h).astype(BF16) for h in range(N_HEADS)]
    row3 = lax.broadcasted_iota(jnp.int32, (c, 3 * c), 0)
    col3 = lax.broadcasted_iota(jnp.int32, (c, 3 * c), 1)
    tri3 = ((col3 % c) <= row3).astype(BF16)
    zeros_half = jnp.zeros((c, 2 * HEAD_DIM), BF16)
    ng = ng_ref[...]

    def block_diag(x_b):
        return jnp.concatenate([x_b * head_sel_p[h] for h in range(N_HEADS)], axis=0)

    def pad_rows(y_b, second):
        z = zeros_half[:, :y_b.shape[1]]
        return jnp.concatenate([z, y_b] if second else [y_b, z], axis=0)

    def prepare(ci):
        r0 = ci * c
        rows_c = pl.ds(r0, c)
        dlt = _dot(tri3, _split3(jnp.where(strict, la64_ref[rows_c, :], 0.0)))
        gc = _dot(tri3, _split3(la128_ref[rows_c, :]))
        yield
        decay = jnp.exp(jnp.where(tri, dlt, -jnp.inf))
        eg = jnp.exp(gc)
        kb = kb_ref[rows_c, :]
        k_b = k_ref[rows_c, :]
        q_b = q_ref[rows_c, :]
        k_rows = jnp.concatenate([k_b * head_sel_w[h] for h in range(N_HEADS)], axis=0)
        qk = _dot_nt(jnp.concatenate([kb, q_b], axis=0), k_rows)
        yield
        l_s[ci] = jnp.where(strict, qk[:c] * decay, 0.0).astype(BF16)
        at_s[ci] = jnp.where(tri, qk[c:] * decay, 0.0).astype(BF16)
        kbe = kb.astype(F32) * eg
        qg = (q_b.astype(F32) * eg).astype(BF16)
        gl = gc[c - 1:c, :]
        kd = (k_b.astype(F32) * jnp.exp(gl - gc)).astype(BF16)
        vb = vb_ref[rows_c, :]
        for h in range(N_HEADS):
            hs = slice(h * HEAD_DIM, (h + 1) * HEAD_DIM)
            rhs_s[ci, h, :, 0:HEAD_DIM] = vb[:, hs]
            rhs_s[ci, h, :, HEAD_DIM:2 * HEAD_DIM] = kbe[:, hs].astype(BF16)
            wq_s[ci, h, c:2 * c, :] = qg[:, hs]
            kd_s[ci, h] = kd[:, hs]
        egl_s[ci] = jnp.broadcast_to(jnp.exp(gl), (SUBLANES, GROUP_W))

    def invert(ci):
        lf = l_s[ci].astype(F32)
        n8 = jnp.where(diag8, -lf, 0.0)
        t = eye_f + n8
        p = n8.astype(BF16)
        p = _dot(p, block_diag(p)).astype(BF16)
        yield
        pt = _dot(jnp.concatenate([p, t.astype(BF16)], axis=0), block_diag(p))
        t = t + pt[c:]
        p = pt[:c].astype(BF16)
        yield
        t = t + _dot(t.astype(BF16), block_diag(p))
        yield
        for mask in merge_masks:
            off = jnp.where(mask, lf, 0.0).astype(BF16)
            a = _dot(t.astype(BF16), block_diag(off)).astype(BF16)
            yield
            t = t - _dot(a, block_diag(t.astype(BF16)))
            yield
        tb = t.astype(BF16)
        for h in range(N_HEADS):
            pr = slice((h // 2) * LANES, (h // 2 + 1) * LANES)
            x = _dot(tb[:, pr], pad_rows(rhs_s[ci, h], h % 2))
            u_s[ci, h] = x[:, :HEAD_DIM]
            wq_s[ci, h, 0:c, :] = x[:, HEAD_DIM:].astype(BF16)
        yield

    def recur(cis):
        for ci in cis:
            r0 = ci * c
            at = at_s[ci]
            egl = egl_s[ci]
            sts = [st_ref[h] for h in range(N_HEADS)]
            wss = [_dot(wq_s[ci, h], sts[h].astype(BF16)) for h in range(N_HEADS)]
            yield
            vnbs = [(u_s[ci, h] - wss[h][:c]).astype(BF16) for h in range(N_HEADS)]
            for h in range(N_HEADS):
                hs = slice(h * HEAD_DIM, (h + 1) * HEAD_DIM)
                pr = slice((h // 2) * LANES, (h // 2 + 1) * LANES)
                out = wss[h][c:] + _dot(at[:, pr], pad_rows(vnbs[h], h % 2))
                st_ref[h] = sts[h] * egl[0:1, hs] + _dot_tn(kd_s[ci, h], vnbs[h])
                zh = z_ref[pl.ds(r0, c), hs].astype(F32)
                o_ref[pl.ds(r0, c), hs] = (_rms(out, ng) * (zh * _sigmoid(zh))).astype(BF16)
            yield

    return prepare, invert, recur


def _hgrn_stages(q_ref, f_ref, i_ref, z_ref, lbl_ref, ng_ref, o_ref, st_ref, at_s, qe_s, kd_s, dec_s, layer):
    c = CHUNK
    logits = lbl_ref[...]
    e = jnp.exp(logits - jnp.max(logits, axis=0, keepdims=True))
    p = e / jnp.sum(e, axis=0, keepdims=True)
    lb = jnp.sum(p[0:layer + 1], axis=0, keepdims=True) - p[0:1]
    ng = ng_ref[...]

    tt = lax.broadcasted_iota(jnp.int32, (c, c), 0)
    ss = lax.broadcasted_iota(jnp.int32, (c, c), 1)
    trow = lax.broadcasted_iota(jnp.int32, (c, HEAD_DIM), 0)
    levels = (1, 2, 4, 8, 16, 32)
    pair_masks = [((tt & m) != 0) & ((ss & m) == 0) & ((tt // (2 * m)) == (ss // (2 * m))) for m in levels]
    odd_rows = [(trow & m) != 0 for m in levels]
    eye = tt == ss

    def local(ci, h):
        r0 = ci * c
        slot = ci % 2
        hs = slice(h * HEAD_DIM, (h + 1) * HEAD_DIM)
        fr = f_ref[pl.ds(r0, c), hs].astype(F32)
        rq = q_ref[pl.ds(r0, c), hs].astype(F32)
        lbh = lb[:, hs]
        lf = jnp.log2(lbh + (1.0 - lbh) * _sigmoid(fr))
        kk = (1.0 - lbh) * _sigmoid(-fr)
        qq = rq * _sigmoid(rq)
        attn = jnp.where(eye, _dot_nt(qq.astype(BF16), kk.astype(BF16)), 0.0)
        yield
        cs = lf
        tot = lf
        for m, pmask, odd in zip(levels, pair_masks, odd_rows):
            qm = (qq * jnp.exp2(cs)).astype(BF16)
            km = (kk * jnp.exp2(tot - cs)).astype(BF16)
            attn = jnp.where(pmask, _dot_nt(qm, km), attn)
            if m < SUBLANES:
                prev = pltpu.roll(tot, m, axis=0)
                nxt = pltpu.roll(tot, c - m, axis=0)
                cs = cs + jnp.where(odd, prev, 0.0)
                tot = tot + jnp.where(odd, prev, nxt)
            else:
                cs_parts, tot_parts = [], []
                for b0 in range(0, c, 2 * m):
                    t_even = tot[b0:b0 + m]
                    t_both = t_even + tot[b0 + m:b0 + 2 * m]
                    cs_parts += [cs[b0:b0 + m], cs[b0 + m:b0 + 2 * m] + t_even]
                    tot_parts += [t_both, t_both]
                cs = jnp.concatenate(cs_parts, axis=0)
                tot = jnp.concatenate(tot_parts, axis=0)
            yield
        at_s[slot, h] = attn.astype(BF16)
        qe_s[slot, h] = (qq * jnp.exp2(cs)).astype(BF16)
        kd_s[slot, h] = (kk * jnp.exp2(tot - cs)).astype(BF16)
        dec_s[slot, h] = jnp.exp2(tot[0:SUBLANES, :])

    def tail(ci, h):
        r0 = ci * c
        slot = ci % 2
        hs = slice(h * HEAD_DIM, (h + 1) * HEAD_DIM)
        ii = i_ref[pl.ds(r0, c), hs]
        st = st_ref[h]
        out = _dot_nt(qe_s[slot, h], st.astype(BF16)) + _dot(at_s[slot, h], ii)
        st_ref[h] = st * dec_s[slot, h, 0:1, :] + _dot_tn(ii, kd_s[slot, h])
        yield
        zh = z_ref[pl.ds(r0, c), hs].astype(F32)
        o_ref[pl.ds(r0, c), hs] = (_rms(out, ng) * _sigmoid(zh)).astype(BF16)

    return local, tail


def _round_robin(gens):
    gens = list(gens)
    while gens:
        alive = []
        for g in gens:
            try:
                next(g)
                alive.append(g)
            except StopIteration:
                pass
        gens = alive
        yield


def _chain(gens):
    for g in gens:
        yield from g


def _recurrent_kernel(gq_ref, gk_ref, gkb_ref, gvb_ref, la128_ref, la64_ref, gz_ref, gng_ref,
                      rq_ref, rf_ref, ri_ref, rz_ref, lbl_ref, hng_ref, og_ref, oh_ref,
                      gst_ref, l_s, at_s, rhs_s, wq_s, u_s, kd_s, egl_s,
                      hst_ref, hat_s, hqe_s, hkd_s, hdec_s, *, layer):
    @pl.when(pl.program_id(1) == 0)
    def _():
        gst_ref[...] = jnp.zeros(gst_ref.shape, F32)
        hst_ref[...] = jnp.zeros(hst_ref.shape, F32)

    prepare, invert, recur = _gdn_stages(gq_ref, gk_ref, gkb_ref, gvb_ref, la128_ref, la64_ref, gz_ref, gng_ref,
                                         og_ref, gst_ref, l_s, at_s, rhs_s, wq_s, u_s, kd_s, egl_s)
    local, tail = _hgrn_stages(rq_ref, rf_ref, ri_ref, rz_ref, lbl_ref, hng_ref, oh_ref,
                               hst_ref, hat_s, hqe_s, hkd_s, hdec_s, layer)
    n = gq_ref.shape[0] // CHUNK
    heads = range(N_HEADS)

    def hgrn_step(ci):
        gens = [tail(ci - 1, h) for h in heads] if ci > 0 else []
        gens += [local(ci, h) for h in heads] if ci < n else []
        return _round_robin(gens)

    for k in range(n // 2 + 2):
        gens = []
        if k < n // 2:
            gens += [prepare(2 * k), prepare(2 * k + 1)]
        if 1 <= k <= n // 2:
            gens += [invert(2 * k - 2), invert(2 * k - 1)]
        if 2 <= k:
            gens += [recur([2 * k - 4, 2 * k - 3])]
        steps = [hgrn_step(ci) for ci in (2 * k, 2 * k + 1) if ci <= n]
        if steps:
            gens += [_chain(steps)]
        for _ in _round_robin(gens):
            pass


def _recurrent(gdn_in, proj, gng, lbl, hng, layer, b, s):
    ts = _seq_tile(s)
    ns = s // ts
    nc = ts // CHUNK
    assert nc % 2 == 0

    def rows(wd):
        return pl.BlockSpec((ts, wd), lambda bi, si: (bi * ns + si, 0))

    def seg(k):
        return pl.BlockSpec((None, ts, SEG_W), lambda bi, si: (k, bi * ns + si, 0))

    head_vec = pl.BlockSpec((1, HEAD_DIM), lambda bi, si: (0, 0))
    n = b * s
    return pl.pallas_call(
        functools.partial(_recurrent_kernel, layer=layer),
        grid=(b, ns),
        in_specs=[rows(GROUP_W)] * 5 + [rows(N_HEADS * CHUNK), seg(SEG_G_Z), head_vec,
                  seg(SEG_R_Q), seg(SEG_R_F), seg(SEG_R_I), seg(SEG_R_Z),
                  pl.BlockSpec(lbl.shape, lambda bi, si: (0, 0)), head_vec],
        out_specs=[rows(GROUP_W), rows(GROUP_W)],
        out_shape=[jax.ShapeDtypeStruct((n, GROUP_W), BF16)] * 2,
        scratch_shapes=[pltpu.VMEM((N_HEADS, HEAD_DIM, HEAD_DIM), F32),
                        pltpu.VMEM((nc, CHUNK, N_HEADS * CHUNK), BF16),
                        pltpu.VMEM((nc, CHUNK, N_HEADS * CHUNK), BF16),
                        pltpu.VMEM((nc, N_HEADS, CHUNK, 2 * HEAD_DIM), BF16),
                        pltpu.VMEM((nc, N_HEADS, 2 * CHUNK, HEAD_DIM), BF16),
                        pltpu.VMEM((nc, N_HEADS, CHUNK, HEAD_DIM), F32),
                        pltpu.VMEM((nc, N_HEADS, CHUNK, HEAD_DIM), BF16),
                        pltpu.VMEM((nc, SUBLANES, GROUP_W), F32),
                        pltpu.VMEM((N_HEADS, HEAD_DIM, HEAD_DIM), F32),
                        pltpu.VMEM((2, N_HEADS, CHUNK, CHUNK), BF16),
                        pltpu.VMEM((2, N_HEADS, CHUNK, HEAD_DIM), BF16),
                        pltpu.VMEM((2, N_HEADS, CHUNK, HEAD_DIM), BF16),
                        pltpu.VMEM((2, N_HEADS, SUBLANES, HEAD_DIM), F32)],
        compiler_params=_params(("parallel", "arbitrary")),
        name="recurrent",
    )(*gdn_in, proj, gng, proj, proj, proj, proj, lbl, hng)


def _mix_mlp_kernel(ya_ref, yb_ref, yc_ref, yd_ref, wo_ref, h_ref, g_ref, w1_ref, w2_ref, fg_ref,
                    o_ref, xn_ref, *, final_norm):
    f = pl.program_id(1)

    @pl.when(f == 0)
    def _():
        y = jnp.concatenate([ya_ref[...], yb_ref[...], yc_ref[...], yd_ref[...]], axis=1)
        h1 = h_ref[...] + _dot(y, wo_ref[...])
        xn_ref[...] = _rms(h1, g_ref[...]).astype(BF16)
        o_ref[...] = h1

    a = jnp.maximum(_dot(xn_ref[...], w1_ref[...]), 0.0)
    o_ref[...] += _dot((a * a).astype(BF16), w2_ref[...])

    if final_norm:
        @pl.when(f == pl.num_programs(1) - 1)
        def _():
            o_ref[...] = _rms(o_ref[...], fg_ref[...])


def _mix_mlp(ys, wo, h, g, w1, w2, fg, layer, final_norm):
    n, d = h.shape
    dff = w1.shape[2]
    tm = min(512, n)
    tf = 1024
    ysp = pl.BlockSpec((tm, GROUP_W), lambda i, f: (i, 0))
    rows = pl.BlockSpec((tm, d), lambda i, f: (i, 0))
    vec = pl.BlockSpec((1, d), lambda i, f: (0, 0))
    return pl.pallas_call(
        functools.partial(_mix_mlp_kernel, final_norm=final_norm),
        grid=(n // tm, dff // tf),
        in_specs=[ysp, ysp, ysp, ysp,
                  pl.BlockSpec((None, d, d), lambda i, f: (layer, 0, 0), pipeline_mode=pl.Buffered(1)),
                  rows, vec,
                  pl.BlockSpec((None, d, tf), lambda i, f: (layer, 0, f)),
                  pl.BlockSpec((None, tf, d), lambda i, f: (layer, f, 0)),
                  vec],
        out_specs=rows,
        out_shape=jax.ShapeDtypeStruct((n, d), F32),
        scratch_shapes=[pltpu.VMEM((tm, d), BF16)],
        compiler_params=_params(("parallel", "arbitrary")),
        name="mix_mlp",
    )(*ys, wo, h, g, w1, w2, fg)


def _pack_w_in(w_in):
    w_in = w_in.astype(BF16)
    widths = (GROUP_W, GROUP_W, GROUP_W, MLA_Q_RANK, MLA_KV_RANK, MLA_ROPE,
              GROUP_W, GROUP_W, GROUP_W, GROUP_W, N_HEADS, N_HEADS,
              GROUP_W, GROUP_W, GROUP_W, GROUP_W)
    cols, start = [], 0
    for wd in widths:
        cols.append(w_in[..., start:start + wd])
        start += wd
    (sc_x, sc_c, sc_b, m_cq, m_ckv, m_kr, g_q, g_k, g_v, g_z, g_a, g_b, r_q, r_f, r_i, r_z) = cols
    lead = w_in.shape[:-1]

    def zeros(wd):
        return jnp.zeros(lead + (wd,), w_in.dtype)

    half = MLA_ROPE // 2
    kr_swapped = jnp.concatenate([m_kr[..., half:], m_kr[..., :half]], axis=-1)
    seg_q = jnp.concatenate([m_cq, g_a, g_b, zeros(SEG_W - MLA_Q_RANK - 2 * N_HEADS)], axis=-1)
    seg_kv = jnp.concatenate([m_ckv, m_kr, zeros(LANES - MLA_ROPE), kr_swapped, zeros(LANES - MLA_ROPE)], axis=-1)
    segs = [sc_x, sc_c, sc_b, g_q, g_k, g_v, g_z, r_q, r_f, r_i, r_z, seg_q, seg_kv]
    return jnp.concatenate(segs, axis=-1)


def _pack_mla_weights(w_uq, w_ukv):
    l = w_uq.shape[0]
    half = MLA_ROPE // 2
    wq = w_uq.reshape(l, MLA_Q_RANK, N_HEADS, MLA_NOPE + MLA_ROPE)
    nope, rp = wq[..., :MLA_NOPE], wq[..., MLA_NOPE:]
    z64 = jnp.zeros(rp.shape, rp.dtype)
    wqa = jnp.concatenate([nope, rp, z64], axis=-1).reshape(l, MLA_Q_RANK, N_HEADS * QK_PAD)
    rp_sw = jnp.concatenate([rp[..., half:], rp[..., :half]], axis=-1)
    wqb = jnp.concatenate([rp_sw, z64], axis=-1).reshape(l, MLA_Q_RANK, N_HEADS * LANES)
    wkv = w_ukv.reshape(l, MLA_KV_RANK, N_HEADS, MLA_NOPE + HEAD_DIM)
    wk = wkv[..., :MLA_NOPE].reshape(l, MLA_KV_RANK, N_HEADS * MLA_NOPE)
    wv = wkv[..., MLA_NOPE:].reshape(l, MLA_KV_RANK, N_HEADS * HEAD_DIM)
    return wqa.astype(BF16), wqb.astype(BF16), wk.astype(BF16), jnp.swapaxes(wv, 1, 2).astype(BF16)


def _pad_lanes(x):
    l, k = x.shape
    return jnp.concatenate([x, jnp.zeros((l, LANES - k), x.dtype)], axis=-1).reshape(l, 1, LANES)


def kernel(x, positions, norm1_g, w_in, sconv_w, sconv_out_g, mla_q_g, mla_kv_g, mla_w_uq, mla_w_ukv,
           mla_out_g, gdn_conv_w, gdn_a_log, gdn_dt_bias, gdn_norm_g, hgrn_lb_logits, hgrn_norm_g,
           w_o, norm2_g, w_ff1, w_ff2, final_g):
    b, s, d = x.shape
    n = b * s
    depth = w_in.shape[0]

    w_in_p = _pack_w_in(w_in)
    wqa, wqb, wk, wvt = _pack_mla_weights(mla_w_uq, mla_w_ukv)
    w_o_b = w_o.astype(BF16)
    w1_b = w_ff1.astype(BF16)
    w2_b = w_ff2.astype(BF16)
    alog_p = _pad_lanes(gdn_a_log)
    dtb_p = _pad_lanes(gdn_dt_bias)

    half = MLA_ROPE // 2
    inv_freq = ROPE_THETA ** (-jnp.arange(half, dtype=F32) / half)
    invf = jnp.concatenate([inv_freq, inv_freq, jnp.zeros((LANES - MLA_ROPE,), F32)]).reshape(1, LANES)
    cos, sin = _rope_tables(positions.astype(F32).reshape(n, 1), invf)

    h = x.reshape(n, d)
    for l in range(depth):
        proj = _in_proj(h, norm1_g[l].reshape(1, d), w_in_p, l)
        y_sc = _sconv(proj, sconv_w[l], sconv_out_g[l].reshape(1, GROUP_W), b, s)
        q, k, vt = _mla_up(proj, cos, sin, mla_q_g[l].reshape(1, -1), mla_kv_g[l].reshape(1, -1),
                           wqa[l], wqb[l], wk[l], wvt[l], b, s)
        y_mla = _attention(q, k, vt, mla_out_g[l].reshape(1, GROUP_W), b, s)
        gdn_in = _gdn_prep(proj, gdn_conv_w[l], alog_p[l], dtb_p[l], b, s)
        y_gdn, y_hg = _recurrent(gdn_in, proj, gdn_norm_g[l].reshape(1, HEAD_DIM), hgrn_lb_logits,
                                 hgrn_norm_g[l].reshape(1, HEAD_DIM), l, b, s)
        h = _mix_mlp((y_sc, y_mla, y_gdn, y_hg), w_o_b, h, norm2_g[l].reshape(1, d), w1_b, w2_b,
                     final_g.reshape(1, d), l, final_norm=(l == depth - 1))
    return h.reshape(b, s, d)
```

```python
import functools

import jax
import jax.numpy as jnp
from jax import lax
from jax.experimental import pallas as pl
from jax.experimental.pallas import tpu as pltpu

F32 = jnp.float32
BF16 = jnp.bfloat16
HIGHEST = lax.Precision.HIGHEST

D_MODEL = 2048
GROUP_W = 512
HEAD_DIM = 128
N_HEADS = 4
MLA_Q_RANK = 384
MLA_KV_RANK = 256
MLA_NOPE = 128
MLA_ROPE = 64
ROPE_THETA = 10000.0
CHUNK = 64
D_FF = 4 * D_MODEL
EPS = 1e-6
LOG2_E = 1.4426950408889634

SEG_W = 512
(SEG_SC_X, SEG_SC_C, SEG_SC_B, SEG_G_Q, SEG_G_K, SEG_G_V, SEG_G_Z,
 SEG_R_Q, SEG_R_F, SEG_R_I, SEG_R_Z, SEG_MLA_Q, SEG_MLA_KV) = range(13)
N_SEG = 13
QK_PAD = 256
LANES = 128
SUBLANES = 8
VMEM_LIMIT = 56 * 1024 * 1024


def _params(sem, vmem=VMEM_LIMIT):
    return pltpu.CompilerParams(dimension_semantics=sem, vmem_limit_bytes=vmem)


def _dot(a, b, precision=None):
    return jnp.dot(a, b, preferred_element_type=F32, precision=precision)


def _dot_nt(a, b, precision=None):
    return lax.dot_general(a, b, (((1,), (1,)), ((), ())), preferred_element_type=F32, precision=precision)


def _dot_tn(a, b, precision=None):
    return lax.dot_general(a, b, (((0,), (0,)), ((), ())), preferred_element_type=F32, precision=precision)


def _rms(x, g):
    return x * lax.rsqrt(jnp.mean(x * x, axis=-1, keepdims=True) + EPS) * g


def _sigmoid(x):
    return jax.nn.sigmoid(x)


def _inproj_kernel(x_ref, g_ref, w_ref, o_ref, xn_ref):
    @pl.when(pl.program_id(1) == 0)
    def _():
        xn_ref[...] = _rms(x_ref[...], g_ref[...]).astype(BF16)

    o_ref[...] = _dot_nt(xn_ref[...], w_ref[...]).astype(o_ref.dtype)


def _in_proj(h, g, w, layer):
    n, d = h.shape
    tm = min(1024, n)
    return pl.pallas_call(
        _inproj_kernel,
        grid=(n // tm, N_SEG),
        in_specs=[pl.BlockSpec((tm, d), lambda i, j: (i, 0)),
                  pl.BlockSpec((1, d), lambda i, j: (0, 0)),
                  pl.BlockSpec((None, SEG_W, d), lambda i, j: (layer, j, 0))],
        out_specs=pl.BlockSpec((None, tm, SEG_W), lambda i, j: (j, i, 0)),
        out_shape=jax.ShapeDtypeStruct((N_SEG, n, SEG_W), BF16),
        scratch_shapes=[pltpu.VMEM((tm, d), BF16)],
        compiler_params=_params(("parallel", "arbitrary")),
        name="in_proj",
    )(h, g, w)


def _sconv_kernel(x_ref, c_ref, b_ref, w_ref, g_ref, o_ref, zbuf):
    s = pl.program_id(1)
    ts = x_ref.shape[0]

    @pl.when(s == 0)
    def _():
        zbuf[0:SUBLANES, :] = jnp.zeros((SUBLANES, GROUP_W), F32)

    @pl.when(s > 0)
    def _():
        zbuf[0:SUBLANES, :] = zbuf[ts:ts + SUBLANES, :]

    zbuf[SUBLANES:ts + SUBLANES, :] = c_ref[...].astype(F32) * x_ref[...].astype(F32)
    w = w_ref[...]
    y = (w[2:3] * zbuf[SUBLANES:ts + SUBLANES, :]
         + w[1:2] * zbuf[SUBLANES - 1:ts + SUBLANES - 1, :]
         + w[0:1] * zbuf[SUBLANES - 2:ts + SUBLANES - 2, :])
    o_ref[...] = _rms(b_ref[...].astype(F32) * y, g_ref[...]).astype(BF16)


def _seq_tile(s):
    return min(512, s)


def _sconv(proj, w, g, b, s):
    ts = _seq_tile(s)
    ns = s // ts

    def seg(k):
        return pl.BlockSpec((None, ts, SEG_W), lambda bi, si: (k, bi * ns + si, 0))

    return pl.pallas_call(
        _sconv_kernel,
        grid=(b, ns),
        in_specs=[seg(SEG_SC_X), seg(SEG_SC_C), seg(SEG_SC_B),
                  pl.BlockSpec((3, GROUP_W), lambda bi, si: (0, 0)),
                  pl.BlockSpec((1, GROUP_W), lambda bi, si: (0, 0))],
        out_specs=pl.BlockSpec((ts, GROUP_W), lambda bi, si: (bi * ns + si, 0)),
        out_shape=jax.ShapeDtypeStruct((b * s, GROUP_W), BF16),
        scratch_shapes=[pltpu.VMEM((ts + SUBLANES, GROUP_W), F32)],
        compiler_params=_params(("parallel", "arbitrary")),
        name="sconv",
    )(proj, proj, proj, w, g)


def _rope_kernel(pos_ref, invf_ref, cos_ref, sin_ref):
    ang = pos_ref[...] * invf_ref[...]
    lane = lax.broadcasted_iota(jnp.int32, ang.shape, 1)
    half = MLA_ROPE // 2
    cos_ref[...] = jnp.where(lane < MLA_ROPE, jnp.cos(ang), 0.0)
    sn = jnp.sin(ang)
    sin_ref[...] = jnp.where(lane < half, -sn, jnp.where(lane < MLA_ROPE, sn, 0.0))


def _rope_tables(pos_col, invf):
    n = pos_col.shape[0]
    tm = min(2048, n)
    return pl.pallas_call(
        _rope_kernel,
        grid=(n // tm,),
        in_specs=[pl.BlockSpec((tm, 1), lambda i: (i, 0)),
                  pl.BlockSpec((1, LANES), lambda i: (0, 0))],
        out_specs=[pl.BlockSpec((tm, LANES), lambda i: (i, 0))] * 2,
        out_shape=[jax.ShapeDtypeStruct((n, LANES), F32)] * 2,
        compiler_params=_params(("parallel",)),
        name="rope_tables",
    )(pos_col, invf)


def _mla_up_kernel(a_ref, b_ref, cos_ref, sin_ref, gq_ref, gkv_ref,
                   wqa_ref, wqb_ref, wk_ref, wvt_ref, q_ref, k_ref, vt_ref):
    scale = (MLA_NOPE + MLA_ROPE) ** -0.5 * LOG2_E
    cos = cos_ref[...]
    sin = sin_ref[...]
    a = a_ref[...].astype(F32)
    cqn = _rms(a[:, :MLA_Q_RANK], gq_ref[...]).astype(BF16)
    qa = _dot(cqn, wqa_ref[...])
    qb = _dot(cqn, wqb_ref[...])
    for h in range(N_HEADS):
        o = h * QK_PAD
        q_ref[:, o:o + LANES] = (qa[:, o:o + LANES] * scale).astype(BF16)
        roped = qa[:, o + LANES:o + QK_PAD] * cos + qb[:, h * LANES:(h + 1) * LANES] * sin
        q_ref[:, o + LANES:o + QK_PAD] = (roped * scale).astype(BF16)
    bb = b_ref[...].astype(F32)
    ckvn = _rms(bb[:, :MLA_KV_RANK], gkv_ref[...]).astype(BF16)
    kk = _dot(ckvn, wk_ref[...])
    vt_ref[...] = _dot_nt(wvt_ref[...], ckvn).astype(BF16)
    kr = (bb[:, MLA_KV_RANK:MLA_KV_RANK + LANES] * cos
          + bb[:, MLA_KV_RANK + LANES:MLA_KV_RANK + 2 * LANES] * sin).astype(BF16)
    for h in range(N_HEADS):
        o = h * QK_PAD
        k_ref[:, o:o + LANES] = kk[:, h * LANES:(h + 1) * LANES].astype(BF16)
        k_ref[:, o + LANES:o + QK_PAD] = kr


def _mla_up(proj, cos, sin, gq, gkv, wqa, wqb, wk, wvt, b, s):
    n = proj.shape[1]
    tm = min(512, s)
    ns = s // tm

    def seg(k):
        return pl.BlockSpec((None, tm, SEG_W), lambda i: (k, i, 0))

    def full(x):
        return pl.BlockSpec(x.shape, lambda i: (0,) * x.ndim)

    def rows(w):
        return pl.BlockSpec((tm, w), lambda i: (i, 0))

    return pl.pallas_call(
        _mla_up_kernel,
        grid=(n // tm,),
        in_specs=[seg(SEG_MLA_Q), seg(SEG_MLA_KV), rows(LANES), rows(LANES),
                  full(gq), full(gkv), full(wqa), full(wqb), full(wk), full(wvt)],
        out_specs=[rows(N_HEADS * QK_PAD), rows(N_HEADS * QK_PAD),
                   pl.BlockSpec((None, GROUP_W, tm), lambda i: (i // ns, 0, i % ns))],
        out_shape=[jax.ShapeDtypeStruct((n, N_HEADS * QK_PAD), BF16),
                   jax.ShapeDtypeStruct((n, N_HEADS * QK_PAD), BF16),
                   jax.ShapeDtypeStruct((b, GROUP_W, s), BF16)],
        compiler_params=_params(("parallel",)),
        name="mla_up",
    )(proj, proj, cos, sin, gq, gkv, wqa, wqb, wk, wvt)


def _attn_kernel(q_ref, k_ref, vt_ref, g_ref, o_ref, acc_ref):
    i = pl.program_id(1)
    tq = q_ref.shape[0]
    kv_i = lax.broadcasted_iota(jnp.int32, (tq, tq), 0)
    q_i = lax.broadcasted_iota(jnp.int32, (tq, tq), 1)
    causal = kv_i <= q_i
    heads = range(N_HEADS)

    def block(j, carry, masked):
        start = pl.multiple_of(j * tq, tq)
        sc = [_dot_nt(k_ref[pl.ds(start, tq), h * QK_PAD:(h + 1) * QK_PAD],
                      q_ref[:, h * QK_PAD:(h + 1) * QK_PAD]) for h in heads]
        if masked:
            sc = [jnp.where(causal, x, -jnp.inf) for x in sc]
        m_old = carry[:N_HEADS]
        l_old = carry[N_HEADS:]
        m_new = [jnp.maximum(m, jnp.max(x, axis=0, keepdims=True)) for m, x in zip(m_old, sc)]
        alpha = [jnp.exp2(mo - mn) for mo, mn in zip(m_old, m_new)]
        p = [jnp.exp2(x - mn) for x, mn in zip(sc, m_new)]
        l_new = [a * l + jnp.sum(x, axis=0, keepdims=True) for a, l, x in zip(alpha, l_old, p)]
        pv = [_dot(vt_ref[h * HEAD_DIM:(h + 1) * HEAD_DIM, pl.ds(start, tq)], x.astype(BF16))
              for h, x in zip(heads, p)]
        for h in heads:
            acc_ref[h] = alpha[h] * acc_ref[h] + pv[h]
        return tuple(m_new) + tuple(l_new)

    acc_ref[...] = jnp.zeros(acc_ref.shape, F32)
    init = (jnp.full((1, tq), -jnp.inf, F32),) * N_HEADS + (jnp.zeros((1, tq), F32),) * N_HEADS
    carry = lax.fori_loop(0, i, functools.partial(block, masked=False), init)
    carry = block(i, carry, True)
    ot = jnp.concatenate([acc_ref[h] / carry[N_HEADS + h] for h in heads], axis=0)
    ot = ot * lax.rsqrt(jnp.mean(ot * ot, axis=0, keepdims=True) + EPS)
    o_ref[...] = (ot.T * g_ref[...]).astype(BF16)


def _attention(q, k, vt, g, b, s):
    tq = min(512, s)
    nq = s // tq
    q3 = q.reshape(b, s, N_HEADS * QK_PAD)
    k3 = k.reshape(b, s, N_HEADS * QK_PAD)
    out = pl.pallas_call(
        _attn_kernel,
        grid=(b, nq),
        in_specs=[pl.BlockSpec((None, tq, N_HEADS * QK_PAD), lambda bi, i: (bi, i, 0)),
                  pl.BlockSpec((None, s, N_HEADS * QK_PAD), lambda bi, i: (bi, 0, 0)),
                  pl.BlockSpec((None, GROUP_W, s), lambda bi, i: (bi, 0, 0)),
                  pl.BlockSpec((1, GROUP_W), lambda bi, i: (0, 0))],
        out_specs=pl.BlockSpec((None, tq, GROUP_W), lambda bi, i: (bi, i, 0)),
        out_shape=jax.ShapeDtypeStruct((b, s, GROUP_W), BF16),
        scratch_shapes=[pltpu.VMEM((N_HEADS, HEAD_DIM, tq), F32)],
        compiler_params=_params(("parallel", "arbitrary")),
        name="mla_attention",
    )(q3, k3, vt, g)
    return out.reshape(b * s, GROUP_W)


GDN_CONV_K = 4


def _gdn_prep_kernel(q_ref, k_ref, v_ref, a_ref, w_ref, alog_ref, dtb_ref,
                     qo_ref, ko_ref, kbo_ref, vbo_ref, la128_ref, la64_ref, buf):
    s = pl.program_id(1)
    ts = q_ref.shape[0]
    w = w_ref[...]
    gx = a_ref[:, MLA_Q_RANK:MLA_Q_RANK + LANES].astype(F32)
    xs = gx + dtb_ref[...]
    softplus = jnp.maximum(xs, 0.0) + jnp.log1p(jnp.exp(-jnp.abs(xs)))
    log_a = -jnp.exp(alog_ref[...]) * softplus
    beta = _sigmoid(gx)
    lane = lax.broadcasted_iota(jnp.int32, (ts, LANES), 1)
    la_b = [jnp.broadcast_to(log_a[:, h:h + 1], (ts, LANES)) for h in range(N_HEADS)]
    beta_b = [jnp.broadcast_to(beta[:, N_HEADS + h:N_HEADS + h + 1], (ts, HEAD_DIM)) for h in range(N_HEADS)]
    for h in range(N_HEADS):
        la128_ref[:, h * HEAD_DIM:(h + 1) * HEAD_DIM] = la_b[h]
    for pr in range(N_HEADS // 2):
        la64_ref[:, pr * LANES:(pr + 1) * LANES] = jnp.where(lane < CHUNK, la_b[2 * pr], la_b[2 * pr + 1])
    for idx, src in enumerate((q_ref, k_ref, v_ref)):
        @pl.when(s == 0)
        def _(idx=idx):
            buf[idx, 0:SUBLANES, :] = jnp.zeros((SUBLANES, GROUP_W), F32)

        @pl.when(s > 0)
        def _(idx=idx):
            buf[idx, 0:SUBLANES, :] = buf[idx, ts:ts + SUBLANES, :]

        buf[idx, SUBLANES:ts + SUBLANES, :] = src[...].astype(F32)
        wi = w[:, idx * GROUP_W:(idx + 1) * GROUP_W]
        y = wi[3:4] * buf[idx, SUBLANES:ts + SUBLANES, :]
        for j in range(1, GDN_CONV_K):
            y = y + wi[3 - j:4 - j] * buf[idx, SUBLANES - j:ts + SUBLANES - j, :]
        y = y * _sigmoid(y)
        for h in range(N_HEADS):
            hs = slice(h * HEAD_DIM, (h + 1) * HEAD_DIM)
            yh = y[:, hs]
            if idx < 2:
                yh = yh * lax.rsqrt(jnp.sum(yh * yh, axis=-1, keepdims=True) + EPS)
            if idx == 0:
                qo_ref[:, hs] = (yh * (HEAD_DIM ** -0.5)).astype(BF16)
            elif idx == 1:
                ko_ref[:, hs] = yh.astype(BF16)
                kbo_ref[:, hs] = (yh * beta_b[h]).astype(BF16)
            else:
                vbo_ref[:, hs] = (yh * beta_b[h]).astype(BF16)


def _gdn_prep(proj, w, alog, dtb, b, s):
    ts = _seq_tile(s)
    ns = s // ts

    def seg(k):
        return pl.BlockSpec((None, ts, SEG_W), lambda bi, si: (k, bi * ns + si, 0))

    def rows(wd):
        return pl.BlockSpec((ts, wd), lambda bi, si: (bi * ns + si, 0))

    def full(x):
        return pl.BlockSpec(x.shape, lambda bi, si: (0,) * x.ndim)

    n = b * s
    return pl.pallas_call(
        _gdn_prep_kernel,
        grid=(b, ns),
        in_specs=[seg(SEG_G_Q), seg(SEG_G_K), seg(SEG_G_V), seg(SEG_MLA_Q),
                  full(w), full(alog), full(dtb)],
        out_specs=[rows(GROUP_W)] * 5 + [rows(N_HEADS * CHUNK)],
        out_shape=[jax.ShapeDtypeStruct((n, GROUP_W), BF16)] * 4 + [jax.ShapeDtypeStruct((n, GROUP_W), F32)]
        + [jax.ShapeDtypeStruct((n, N_HEADS * CHUNK), F32)],
        scratch_shapes=[pltpu.VMEM((3, ts + SUBLANES, GROUP_W), F32)],
        compiler_params=_params(("parallel", "arbitrary")),
        name="gdn_prep",
    )(proj, proj, proj, proj, w, alog, dtb)


def _split3(x):
    hi = x.astype(BF16)
    r1 = x - hi.astype(F32)
    mid = r1.astype(BF16)
    lo = (r1 - mid.astype(F32)).astype(BF16)
    return jnp.concatenate([hi, mid, lo], axis=0)


def _gdn_stages(q_ref, k_ref, kb_ref, vb_ref, la128_ref, la64_ref, z_ref, ng_ref, o_ref,
                st_ref, l_s, at_s, rhs_s, wq_s, u_s, kd_s, egl_s):
    c = CHUNK
    pw = N_HEADS * c
    row = lax.broadcasted_iota(jnp.int32, (c, pw), 0)
    col = lax.broadcasted_iota(jnp.int32, (c, pw), 1) % c
    tri = col <= row
    strict = col < row
    eye_f = (col == row).astype(F32)
    diag8 = (col // SUBLANES) == (row // SUBLANES)
    merge_levels = (8, 16, 32)
    merge_masks = [((row & m) != 0) & ((col & m) == 0) & ((row // (2 * m)) == (col // (2 * m)))
                   for m in merge_levels]
    lane_head_p = lax.broadcasted_iota(jnp.int32, (c, pw), 1) // c
    head_sel_p = [(lane_head_p == h).astype(BF16) for h in range(N_HEADS)]
    lane_head_w = lax.broadcasted_iota(jnp.int32, (c, GROUP_W), 1) // HEAD_DIM
    head_sel_w = [(lane_head_w == h).astype(BF16) for h in range(N_HEADS)]
    row3 = lax.broadcasted_iota(jnp.int32, (c, 3 * c), 0)
    col3 = lax.broadcasted_iota(jnp.int32, (c, 3 * c), 1)
    tri3 = ((col3 % c) <= row3).astype(BF16)
    zeros_half = jnp.zeros((c, 2 * HEAD_DIM), BF16)
    ng = ng_ref[...]

    def block_diag(x_b):
        return jnp.concatenate([x_b * head_sel_p[h] for h in range(N_HEADS)], axis=0)

    def pad_rows(y_b, second):
        z = zeros_half[:, :y_b.shape[1]]
        return jnp.concatenate([z, y_b] if second else [y_b, z], axis=0)

    def prepare(ci):
        r0 = ci * c
        rows_c = pl.ds(r0, c)
        dlt = _dot(tri3, _split3(jnp.where(strict, la64_ref[rows_c, :], 0.0)))
        gc = _dot(tri3, _split3(la128_ref[rows_c, :]))
        yield
        decay = jnp.exp(jnp.where(tri, dlt, -jnp.inf))
        eg = jnp.exp(gc)
        kb = kb_ref[rows_c, :]
        k_b = k_ref[rows_c, :]
        q_b = q_ref[rows_c, :]
        k_rows = jnp.concatenate([k_b * head_sel_w[h] for h in range(N_HEADS)], axis=0)
        qk = _dot_nt(jnp.concatenate([kb, q_b], axis=0), k_rows)
        yield
        l_s[ci] = jnp.where(strict, qk[:c] * decay, 0.0).astype(BF16)
        at_s[ci] = jnp.where(tri, qk[c:] * decay, 0.0).astype(BF16)
        kbe = kb.astype(F32) * eg
        qg = (q_b.astype(F32) * eg).astype(BF16)
        gl = gc[c - 1:c, :]
        kd = (k_b.astype(F32) * jnp.exp(gl - gc)).astype(BF16)
        vb = vb_ref[rows_c, :]
        for h in range(N_HEADS):
            hs = slice(h * HEAD_DIM, (h + 1) * HEAD_DIM)
            rhs_s[ci, h, :, 0:HEAD_DIM] = vb[:, hs]
            rhs_s[ci, h, :, HEAD_DIM:2 * HEAD_DIM] = kbe[:, hs].astype(BF16)
            wq_s[ci, h, c:2 * c, :] = qg[:, hs]
            kd_s[ci, h] = kd[:, hs]
        egl_s[ci] = jnp.broadcast_to(jnp.exp(gl), (SUBLANES, GROUP_W))

    def invert(ci):
        lf = l_s[ci].astype(F32)
        n8 = jnp.where(diag8, -lf, 0.0)
        t = eye_f + n8
        p = n8.astype(BF16)
        p = _dot(p, block_diag(p)).astype(BF16)
        yield
        pt = _dot(jnp.concatenate([p, t.astype(BF16)], axis=0), block_diag(p))
        t = t + pt[c:]
        p = pt[:c].astype(BF16)
        yield
        t = t + _dot(t.astype(BF16), block_diag(p))
        yield
        for mask in merge_masks:
            off = jnp.where(mask, lf, 0.0).astype(BF16)
            a = _dot(t.astype(BF16), block_diag(off)).astype(BF16)
            yield
            t = t - _dot(a, block_diag(t.astype(BF16)))
            yield
        tb = t.astype(BF16)
        for h in range(N_HEADS):
            pr = slice((h // 2) * LANES, (h // 2 + 1) * LANES)
            x = _dot(tb[:, pr], pad_rows(rhs_s[ci, h], h % 2))
            u_s[ci, h] = x[:, :HEAD_DIM]
            wq_s[ci, h, 0:c, :] = x[:, HEAD_DIM:].astype(BF16)
        yield

    def recur(cis):
        for ci in cis:
            r0 = ci * c
            at = at_s[ci]
            egl = egl_s[ci]
            sts = [st_ref[h] for h in range(N_HEADS)]
            wss = [_dot(wq_s[ci, h], sts[h].astype(BF16)) for h in range(N_HEADS)]
            yield
            vnbs = [(u_s[ci, h] - wss[h][:c]).astype(BF16) for h in range(N_HEADS)]
            for h in range(N_HEADS):
                hs = slice(h * HEAD_DIM, (h + 1) * HEAD_DIM)
                pr = slice((h // 2) * LANES, (h // 2 + 1) * LANES)
                out = wss[h][c:] + _dot(at[:, pr], pad_rows(vnbs[h], h % 2))
                st_ref[h] = sts[h] * egl[0:1, hs] + _dot_tn(kd_s[ci, h], vnbs[h])
                zh = z_ref[pl.ds(r0, c), hs].astype(F32)
                o_ref[pl.ds(r0, c), hs] = (_rms(out, ng) * (zh * _sigmoid(zh))).astype(BF16)
            yield

    return prepare, invert, recur


def _hgrn_stages(q_ref, f_ref, i_ref, z_ref, lbl_ref, ng_ref, o_ref, st_ref, at_s, qe_s, kd_s, dec_s, layer):
    c = CHUNK
    logits = lbl_ref[...]
    e = jnp.exp(logits - jnp.max(logits, axis=0, keepdims=True))
    p = e / jnp.sum(e, axis=0, keepdims=True)
    lb = jnp.sum(p[0:layer + 1], axis=0, keepdims=True) - p[0:1]
    ng = ng_ref[...]

    tt = lax.broadcasted_iota(jnp.int32, (c, c), 0)
    ss = lax.broadcasted_iota(jnp.int32, (c, c), 1)
    trow = lax.broadcasted_iota(jnp.int32, (c, HEAD_DIM), 0)
    levels = (1, 2, 4, 8, 16, 32)
    pair_masks = [((tt & m) != 0) & ((ss & m) == 0) & ((tt // (2 * m)) == (ss // (2 * m))) for m in levels]
    odd_rows = [(trow & m) != 0 for m in levels]
    eye = tt == ss

    def local(ci, h):
        r0 = ci * c
        slot = ci % 2
        hs = slice(h * HEAD_DIM, (h + 1) * HEAD_DIM)
        fr = f_ref[pl.ds(r0, c), hs].astype(F32)
        rq = q_ref[pl.ds(r0, c), hs].astype(F32)
        lbh = lb[:, hs]
        lf = jnp.log2(lbh + (1.0 - lbh) * _sigmoid(fr))
        kk = (1.0 - lbh) * _sigmoid(-fr)
        qq = rq * _sigmoid(rq)
        attn = jnp.where(eye, _dot_nt(qq.astype(BF16), kk.astype(BF16)), 0.0)
        yield
        cs = lf
        tot = lf
        for m, pmask, odd in zip(levels, pair_masks, odd_rows):
            qm = (qq * jnp.exp2(cs)).astype(BF16)
            km = (kk * jnp.exp2(tot - cs)).astype(BF16)
            attn = jnp.where(pmask, _dot_nt(qm, km), attn)
            if m < SUBLANES:
                prev = pltpu.roll(tot, m, axis=0)
                nxt = pltpu.roll(tot, c - m, axis=0)
                cs = cs + jnp.where(odd, prev, 0.0)
                tot = tot + jnp.where(odd, prev, nxt)
            else:
                cs_parts, tot_parts = [], []
                for b0 in range(0, c, 2 * m):
                    t_even = tot[b0:b0 + m]
                    t_both = t_even + tot[b0 + m:b0 + 2 * m]
                    cs_parts += [cs[b0:b0 + m], cs[b0 + m:b0 + 2 * m] + t_even]
                    tot_parts += [t_both, t_both]
                cs = jnp.concatenate(cs_parts, axis=0)
                tot = jnp.concatenate(tot_parts, axis=0)
            yield
        at_s[slot, h] = attn.astype(BF16)
        qe_s[slot, h] = (qq * jnp.exp2(cs)).astype(BF16)
        kd_s[slot, h] = (kk * jnp.exp2(tot - cs)).astype(BF16)
        dec_s[slot, h] = jnp.exp2(tot[0:SUBLANES, :])

    def tail(ci, h):
        r0 = ci * c
        slot = ci % 2
        hs = slice(h * HEAD_DIM, (h + 1) * HEAD_DIM)
        ii = i_ref[pl.ds(r0, c), hs]
        st = st_ref[h]
        out = _dot_nt(qe_s[slot, h], st.astype(BF16)) + _dot(at_s[slot, h], ii)
        st_ref[h] = st * dec_s[slot, h, 0:1, :] + _dot_tn(ii, kd_s[slot, h])
        yield
        zh = z_ref[pl.ds(r0, c), hs].astype(F32)
        o_ref[pl.ds(r0, c), hs] = (_rms(out, ng) * _sigmoid(zh)).astype(BF16)

    return local, tail


def _round_robin(gens):
    gens = list(gens)
    while gens:
        alive = []
        for g in gens:
            try:
                next(g)
                alive.append(g)
            except StopIteration:
                pass
        gens = alive
        yield


def _chain(gens):
    for g in gens:
        yield from g


def _recurrent_kernel(gq_ref, gk_ref, gkb_ref, gvb_ref, la128_ref, la64_ref, gz_ref, gng_ref,
                      rq_ref, rf_ref, ri_ref, rz_ref, lbl_ref, hng_ref, og_ref, oh_ref,
                      gst_ref, l_s, at_s, rhs_s, wq_s, u_s, kd_s, egl_s,
                      hst_ref, hat_s, hqe_s, hkd_s, hdec_s, *, layer):
    @pl.when(pl.program_id(1) == 0)
    def _():
        gst_ref[...] = jnp.zeros(gst_ref.shape, F32)
        hst_ref[...] = jnp.zeros(hst_ref.shape, F32)

    prepare, invert, recur = _gdn_stages(gq_ref, gk_ref, gkb_ref, gvb_ref, la128_ref, la64_ref, gz_ref, gng_ref,
                                         og_ref, gst_ref, l_s, at_s, rhs_s, wq_s, u_s, kd_s, egl_s)
    local, tail = _hgrn_stages(rq_ref, rf_ref, ri_ref, rz_ref, lbl_ref, hng_ref, oh_ref,
                               hst_ref, hat_s, hqe_s, hkd_s, hdec_s, layer)
    n = gq_ref.shape[0] // CHUNK
    heads = range(N_HEADS)

    def hgrn_step(ci):
        gens = [tail(ci - 1, h) for h in heads] if ci > 0 else []
        gens += [local(ci, h) for h in heads] if ci < n else []
        return _round_robin(gens)

    per_phase = [1] + [2] * (n // 2 - 1) + [1, 1]
    first = [sum(per_phase[:k]) for k in range(len(per_phase))]

    for k in range(n // 2 + 2):
        gens = []
        if k < n // 2:
            gens += [prepare(2 * k), prepare(2 * k + 1)]
        if 1 <= k <= n // 2:
            gens += [invert(2 * k - 2), invert(2 * k - 1)]
        if 2 <= k:
            gens += [recur([2 * k - 4, 2 * k - 3])]
        gens += [_chain([hgrn_step(first[k] + i) for i in range(per_phase[k])])]
        for _ in _round_robin(gens):
            pass


def _recurrent(gdn_in, proj, gng, lbl, hng, layer, b, s):
    ts = _seq_tile(s)
    ns = s // ts
    nc = ts // CHUNK
    assert nc % 2 == 0

    def rows(wd):
        return pl.BlockSpec((ts, wd), lambda bi, si: (bi * ns + si, 0))

    def seg(k):
        return pl.BlockSpec((None, ts, SEG_W), lambda bi, si: (k, bi * ns + si, 0))

    head_vec = pl.BlockSpec((1, HEAD_DIM), lambda bi, si: (0, 0))
    n = b * s
    return pl.pallas_call(
        functools.partial(_recurrent_kernel, layer=layer),
        grid=(b, ns),
        in_specs=[rows(GROUP_W)] * 5 + [rows(N_HEADS * CHUNK), seg(SEG_G_Z), head_vec,
                  seg(SEG_R_Q), seg(SEG_R_F), seg(SEG_R_I), seg(SEG_R_Z),
                  pl.BlockSpec(lbl.shape, lambda bi, si: (0, 0)), head_vec],
        out_specs=[rows(GROUP_W), rows(GROUP_W)],
        out_shape=[jax.ShapeDtypeStruct((n, GROUP_W), BF16)] * 2,
        scratch_shapes=[pltpu.VMEM((N_HEADS, HEAD_DIM, HEAD_DIM), F32),
                        pltpu.VMEM((nc, CHUNK, N_HEADS * CHUNK), BF16),
                        pltpu.VMEM((nc, CHUNK, N_HEADS * CHUNK), BF16),
                        pltpu.VMEM((nc, N_HEADS, CHUNK, 2 * HEAD_DIM), BF16),
                        pltpu.VMEM((nc, N_HEADS, 2 * CHUNK, HEAD_DIM), BF16),
                        pltpu.VMEM((nc, N_HEADS, CHUNK, HEAD_DIM), F32),
                        pltpu.VMEM((nc, N_HEADS, CHUNK, HEAD_DIM), BF16),
                        pltpu.VMEM((nc, SUBLANES, GROUP_W), F32),
                        pltpu.VMEM((N_HEADS, HEAD_DIM, HEAD_DIM), F32),
                        pltpu.VMEM((2, N_HEADS, CHUNK, CHUNK), BF16),
                        pltpu.VMEM((2, N_HEADS, CHUNK, HEAD_DIM), BF16),
                        pltpu.VMEM((2, N_HEADS, CHUNK, HEAD_DIM), BF16),
                        pltpu.VMEM((2, N_HEADS, SUBLANES, HEAD_DIM), F32)],
        compiler_params=_params(("parallel", "arbitrary")),
        name="recurrent",
    )(*gdn_in, proj, gng, proj, proj, proj, proj, lbl, hng)


def _mix_mlp_kernel(ya_ref, yb_ref, yc_ref, yd_ref, wo_ref, h_ref, g_ref, w1_ref, w2_ref, fg_ref,
                    o_ref, xn_ref, *, final_norm):
    f = pl.program_id(1)

    @pl.when(f == 0)
    def _():
        y = jnp.concatenate([ya_ref[...], yb_ref[...], yc_ref[...], yd_ref[...]], axis=1)
        h1 = h_ref[...] + _dot(y, wo_ref[...])
        xn_ref[...] = _rms(h1, g_ref[...]).astype(BF16)
        o_ref[...] = h1

    a = jnp.maximum(_dot(xn_ref[...], w1_ref[...]), 0.0)
    o_ref[...] += _dot((a * a).astype(BF16), w2_ref[...])

    if final_norm:
        @pl.when(f == pl.num_programs(1) - 1)
        def _():
            o_ref[...] = _rms(o_ref[...], fg_ref[...])


def _mix_mlp(ys, wo, h, g, w1, w2, fg, layer, final_norm):
    n, d = h.shape
    dff = w1.shape[2]
    tm = min(512, n)
    tf = 1024
    ysp = pl.BlockSpec((tm, GROUP_W), lambda i, f: (i, 0))
    rows = pl.BlockSpec((tm, d), lambda i, f: (i, 0))
    vec = pl.BlockSpec((1, d), lambda i, f: (0, 0))
    return pl.pallas_call(
        functools.partial(_mix_mlp_kernel, final_norm=final_norm),
        grid=(n // tm, dff // tf),
        in_specs=[ysp, ysp, ysp, ysp,
                  pl.BlockSpec((None, d, d), lambda i, f: (layer, 0, 0), pipeline_mode=pl.Buffered(1)),
                  rows, vec,
                  pl.BlockSpec((None, d, tf), lambda i, f: (layer, 0, f)),
                  pl.BlockSpec((None, tf, d), lambda i, f: (layer, f, 0)),
                  vec],
        out_specs=rows,
        out_shape=jax.ShapeDtypeStruct((n, d), F32),
        scratch_shapes=[pltpu.VMEM((tm, d), BF16)],
        compiler_params=_params(("parallel", "arbitrary")),
        name="mix_mlp",
    )(*ys, wo, h, g, w1, w2, fg)


def _pack_w_in(w_in):
    wt = jnp.swapaxes(w_in, 1, 2)
    widths = (GROUP_W, GROUP_W, GROUP_W, MLA_Q_RANK, MLA_KV_RANK, MLA_ROPE,
              GROUP_W, GROUP_W, GROUP_W, GROUP_W, N_HEADS, N_HEADS,
              GROUP_W, GROUP_W, GROUP_W, GROUP_W)
    rows, start = [], 0
    for wd in widths:
        rows.append(wt[:, start:start + wd])
        start += wd
    (sc_x, sc_c, sc_b, m_cq, m_ckv, m_kr, g_q, g_k, g_v, g_z, g_a, g_b, r_q, r_f, r_i, r_z) = rows
    l, _, d = wt.shape

    def zeros(n):
        return jnp.zeros((l, n, d), wt.dtype)

    half = MLA_ROPE // 2
    kr_swapped = jnp.concatenate([m_kr[:, half:], m_kr[:, :half]], axis=1)
    seg_q = [m_cq, g_a, g_b, zeros(SEG_W - MLA_Q_RANK - 2 * N_HEADS)]
    seg_kv = [m_ckv, m_kr, zeros(LANES - MLA_ROPE), kr_swapped, zeros(LANES - MLA_ROPE)]
    segs = [sc_x, sc_c, sc_b, g_q, g_k, g_v, g_z, r_q, r_f, r_i, r_z] + seg_q + seg_kv
    return jnp.concatenate(segs, axis=1).astype(BF16)


def _pack_mla_weights(w_uq, w_ukv):
    l = w_uq.shape[0]
    half = MLA_ROPE // 2
    wq = w_uq.reshape(l, MLA_Q_RANK, N_HEADS, MLA_NOPE + MLA_ROPE)
    nope, rp = wq[..., :MLA_NOPE], wq[..., MLA_NOPE:]
    z64 = jnp.zeros(rp.shape, rp.dtype)
    wqa = jnp.concatenate([nope, rp, z64], axis=-1).reshape(l, MLA_Q_RANK, N_HEADS * QK_PAD)
    rp_sw = jnp.concatenate([rp[..., half:], rp[..., :half]], axis=-1)
    wqb = jnp.concatenate([rp_sw, z64], axis=-1).reshape(l, MLA_Q_RANK, N_HEADS * LANES)
    wkv = w_ukv.reshape(l, MLA_KV_RANK, N_HEADS, MLA_NOPE + HEAD_DIM)
    wk = wkv[..., :MLA_NOPE].reshape(l, MLA_KV_RANK, N_HEADS * MLA_NOPE)
    wv = wkv[..., MLA_NOPE:].reshape(l, MLA_KV_RANK, N_HEADS * HEAD_DIM)
    return wqa.astype(BF16), wqb.astype(BF16), wk.astype(BF16), jnp.swapaxes(wv, 1, 2).astype(BF16)


def _pad_lanes(x):
    l, k = x.shape
    return jnp.concatenate([x, jnp.zeros((l, LANES - k), x.dtype)], axis=-1).reshape(l, 1, LANES)


def kernel(x, positions, norm1_g, w_in, sconv_w, sconv_out_g, mla_q_g, mla_kv_g, mla_w_uq, mla_w_ukv,
           mla_out_g, gdn_conv_w, gdn_a_log, gdn_dt_bias, gdn_norm_g, hgrn_lb_logits, hgrn_norm_g,
           w_o, norm2_g, w_ff1, w_ff2, final_g):
    b, s, d = x.shape
    n = b * s
    depth = w_in.shape[0]

    w_in_p = _pack_w_in(w_in)
    wqa, wqb, wk, wvt = _pack_mla_weights(mla_w_uq, mla_w_ukv)
    w_o_b = w_o.astype(BF16)
    w1_b = w_ff1.astype(BF16)
    w2_b = w_ff2.astype(BF16)
    alog_p = _pad_lanes(gdn_a_log)
    dtb_p = _pad_lanes(gdn_dt_bias)

    half = MLA_ROPE // 2
    inv_freq = ROPE_THETA ** (-jnp.arange(half, dtype=F32) / half)
    invf = jnp.concatenate([inv_freq, inv_freq, jnp.zeros((LANES - MLA_ROPE,), F32)]).reshape(1, LANES)
    cos, sin = _rope_tables(positions.astype(F32).reshape(n, 1), invf)

    h = x.reshape(n, d)
    for l in range(depth):
        proj = _in_proj(h, norm1_g[l].reshape(1, d), w_in_p, l)
        y_sc = _sconv(proj, sconv_w[l], sconv_out_g[l].reshape(1, GROUP_W), b, s)
        q, k, vt = _mla_up(proj, cos, sin, mla_q_g[l].reshape(1, -1), mla_kv_g[l].reshape(1, -1),
                           wqa[l], wqb[l], wk[l], wvt[l], b, s)
        y_mla = _attention(q, k, vt, mla_out_g[l].reshape(1, GROUP_W), b, s)
        gdn_in = _gdn_prep(proj, gdn_conv_w[l], alog_p[l], dtb_p[l], b, s)
        y_gdn, y_hg = _recurrent(gdn_in, proj, gdn_norm_g[l].reshape(1, HEAD_DIM), hgrn_lb_logits,
                                 hgrn_norm_g[l].reshape(1, HEAD_DIM), l, b, s)
        h = _mix_mlp((y_sc, y_mla, y_gdn, y_hg), w_o_b, h, norm2_g[l].reshape(1, d), w1_b, w2_b,
                     final_g.reshape(1, d), l, final_norm=(l == depth - 1))
    return h.reshape(b, s, d)
```

```python
import functools

import jax
import jax.numpy as jnp
from jax import lax
from jax.experimental import pallas as pl
from jax.experimental.pallas import tpu as pltpu

F32 = jnp.float32
BF16 = jnp.bfloat16
HIGHEST = lax.Precision.HIGHEST

D_MODEL = 2048
GROUP_W = 512
HEAD_DIM = 128
N_HEADS = 4
MLA_Q_RANK = 384
MLA_KV_RANK = 256
MLA_NOPE = 128
MLA_ROPE = 64
ROPE_THETA = 10000.0
CHUNK = 64
D_FF = 4 * D_MODEL
EPS = 1e-6
LOG2_E = 1.4426950408889634

SEG_W = 512
(SEG_SC_X, SEG_SC_C, SEG_SC_B, SEG_G_Q, SEG_G_K, SEG_G_V, SEG_G_Z,
 SEG_R_Q, SEG_R_F, SEG_R_I, SEG_R_Z, SEG_MLA_Q, SEG_MLA_KV) = range(13)
N_SEG = 13
QK_PAD = 256
LANES = 128
SUBLANES = 8
VMEM_LIMIT = 56 * 1024 * 1024


def _params(sem, vmem=VMEM_LIMIT):
    return pltpu.CompilerParams(dimension_semantics=sem, vmem_limit_bytes=vmem)


def _dot(a, b, precision=None):
    return jnp.dot(a, b, preferred_element_type=F32, precision=precision)


def _dot_nt(a, b, precision=None):
    return lax.dot_general(a, b, (((1,), (1,)), ((), ())), preferred_element_type=F32, precision=precision)


def _dot_tn(a, b, precision=None):
    return lax.dot_general(a, b, (((0,), (0,)), ((), ())), preferred_element_type=F32, precision=precision)


def _rms(x, g):
    return x * lax.rsqrt(jnp.mean(x * x, axis=-1, keepdims=True) + EPS) * g


def _sigmoid(x):
    return jax.nn.sigmoid(x)


def _inproj_kernel(x_ref, g_ref, w_ref, o_ref, xn_ref):
    @pl.when(pl.program_id(1) == 0)
    def _():
        xn_ref[...] = _rms(x_ref[...], g_ref[...]).astype(BF16)

    o_ref[...] = _dot_nt(xn_ref[...], w_ref[...]).astype(o_ref.dtype)


def _in_proj(h, g, w, layer):
    n, d = h.shape
    tm = min(1024, n)
    return pl.pallas_call(
        _inproj_kernel,
        grid=(n // tm, N_SEG),
        in_specs=[pl.BlockSpec((tm, d), lambda i, j: (i, 0)),
                  pl.BlockSpec((1, d), lambda i, j: (0, 0)),
                  pl.BlockSpec((None, SEG_W, d), lambda i, j: (layer, j, 0))],
        out_specs=pl.BlockSpec((None, tm, SEG_W), lambda i, j: (j, i, 0)),
        out_shape=jax.ShapeDtypeStruct((N_SEG, n, SEG_W), BF16),
        scratch_shapes=[pltpu.VMEM((tm, d), BF16)],
        compiler_params=_params(("parallel", "arbitrary")),
        name="in_proj",
    )(h, g, w)


def _sconv_kernel(x_ref, c_ref, b_ref, w_ref, g_ref, o_ref, zbuf):
    s = pl.program_id(1)
    ts = x_ref.shape[0]

    @pl.when(s == 0)
    def _():
        zbuf[0:SUBLANES, :] = jnp.zeros((SUBLANES, GROUP_W), F32)

    @pl.when(s > 0)
    def _():
        zbuf[0:SUBLANES, :] = zbuf[ts:ts + SUBLANES, :]

    zbuf[SUBLANES:ts + SUBLANES, :] = c_ref[...].astype(F32) * x_ref[...].astype(F32)
    w = w_ref[...]
    y = (w[2:3] * zbuf[SUBLANES:ts + SUBLANES, :]
         + w[1:2] * zbuf[SUBLANES - 1:ts + SUBLANES - 1, :]
         + w[0:1] * zbuf[SUBLANES - 2:ts + SUBLANES - 2, :])
    o_ref[...] = _rms(b_ref[...].astype(F32) * y, g_ref[...]).astype(BF16)


def _seq_tile(s):
    return min(512, s)


def _sconv(proj, w, g, b, s):
    ts = _seq_tile(s)
    ns = s // ts

    def seg(k):
        return pl.BlockSpec((None, ts, SEG_W), lambda bi, si: (k, bi * ns + si, 0))

    return pl.pallas_call(
        _sconv_kernel,
        grid=(b, ns),
        in_specs=[seg(SEG_SC_X), seg(SEG_SC_C), seg(SEG_SC_B),
                  pl.BlockSpec((3, GROUP_W), lambda bi, si: (0, 0)),
                  pl.BlockSpec((1, GROUP_W), lambda bi, si: (0, 0))],
        out_specs=pl.BlockSpec((ts, GROUP_W), lambda bi, si: (bi * ns + si, 0)),
        out_shape=jax.ShapeDtypeStruct((b * s, GROUP_W), BF16),
        scratch_shapes=[pltpu.VMEM((ts + SUBLANES, GROUP_W), F32)],
        compiler_params=_params(("parallel", "arbitrary")),
        name="sconv",
    )(proj, proj, proj, w, g)


def _rope_kernel(pos_ref, invf_ref, cos_ref, sin_ref):
    ang = pos_ref[...] * invf_ref[...]
    lane = lax.broadcasted_iota(jnp.int32, ang.shape, 1)
    half = MLA_ROPE // 2
    cos_ref[...] = jnp.where(lane < MLA_ROPE, jnp.cos(ang), 0.0)
    sn = jnp.sin(ang)
    sin_ref[...] = jnp.where(lane < half, -sn, jnp.where(lane < MLA_ROPE, sn, 0.0))


def _rope_tables(pos_col, invf):
    n = pos_col.shape[0]
    tm = min(2048, n)
    return pl.pallas_call(
        _rope_kernel,
        grid=(n // tm,),
        in_specs=[pl.BlockSpec((tm, 1), lambda i: (i, 0)),
                  pl.BlockSpec((1, LANES), lambda i: (0, 0))],
        out_specs=[pl.BlockSpec((tm, LANES), lambda i: (i, 0))] * 2,
        out_shape=[jax.ShapeDtypeStruct((n, LANES), F32)] * 2,
        compiler_params=_params(("parallel",)),
        name="rope_tables",
    )(pos_col, invf)


def _mla_up_kernel(a_ref, b_ref, cos_ref, sin_ref, gq_ref, gkv_ref,
                   wqa_ref, wqb_ref, wk_ref, wvt_ref, q_ref, k_ref, vt_ref):
    scale = (MLA_NOPE + MLA_ROPE) ** -0.5 * LOG2_E
    cos = cos_ref[...]
    sin = sin_ref[...]
    a = a_ref[...].astype(F32)
    cqn = _rms(a[:, :MLA_Q_RANK], gq_ref[...]).astype(BF16)
    qa = _dot(cqn, wqa_ref[...])
    qb = _dot(cqn, wqb_ref[...])
    for h in range(N_HEADS):
        o = h * QK_PAD
        q_ref[:, o:o + LANES] = (qa[:, o:o + LANES] * scale).astype(BF16)
        roped = qa[:, o + LANES:o + QK_PAD] * cos + qb[:, h * LANES:(h + 1) * LANES] * sin
        q_ref[:, o + LANES:o + QK_PAD] = (roped * scale).astype(BF16)
    bb = b_ref[...].astype(F32)
    ckvn = _rms(bb[:, :MLA_KV_RANK], gkv_ref[...]).astype(BF16)
    kk = _dot(ckvn, wk_ref[...])
    vt_ref[...] = _dot_nt(wvt_ref[...], ckvn).astype(BF16)
    kr = (bb[:, MLA_KV_RANK:MLA_KV_RANK + LANES] * cos
          + bb[:, MLA_KV_RANK + LANES:MLA_KV_RANK + 2 * LANES] * sin).astype(BF16)
    for h in range(N_HEADS):
        o = h * QK_PAD
        k_ref[:, o:o + LANES] = kk[:, h * LANES:(h + 1) * LANES].astype(BF16)
        k_ref[:, o + LANES:o + QK_PAD] = kr


def _mla_up(proj, cos, sin, gq, gkv, wqa, wqb, wk, wvt, b, s):
    n = proj.shape[1]
    tm = min(512, s)
    ns = s // tm

    def seg(k):
        return pl.BlockSpec((None, tm, SEG_W), lambda i: (k, i, 0))

    def full(x):
        return pl.BlockSpec(x.shape, lambda i: (0,) * x.ndim)

    def rows(w):
        return pl.BlockSpec((tm, w), lambda i: (i, 0))

    return pl.pallas_call(
        _mla_up_kernel,
        grid=(n // tm,),
        in_specs=[seg(SEG_MLA_Q), seg(SEG_MLA_KV), rows(LANES), rows(LANES),
                  full(gq), full(gkv), full(wqa), full(wqb), full(wk), full(wvt)],
        out_specs=[rows(N_HEADS * QK_PAD), rows(N_HEADS * QK_PAD),
                   pl.BlockSpec((None, GROUP_W, tm), lambda i: (i // ns, 0, i % ns))],
        out_shape=[jax.ShapeDtypeStruct((n, N_HEADS * QK_PAD), BF16),
                   jax.ShapeDtypeStruct((n, N_HEADS * QK_PAD), BF16),
                   jax.ShapeDtypeStruct((b, GROUP_W, s), BF16)],
        compiler_params=_params(("parallel",)),
        name="mla_up",
    )(proj, proj, cos, sin, gq, gkv, wqa, wqb, wk, wvt)


def _attn_kernel(q_ref, k_ref, vt_ref, g_ref, o_ref, acc_ref):
    i = pl.program_id(1)
    tq = q_ref.shape[0]
    kv_i = lax.broadcasted_iota(jnp.int32, (tq, tq), 0)
    q_i = lax.broadcasted_iota(jnp.int32, (tq, tq), 1)
    causal = kv_i <= q_i
    heads = range(N_HEADS)

    def block(j, carry, masked):
        start = pl.multiple_of(j * tq, tq)
        sc = [_dot_nt(k_ref[pl.ds(start, tq), h * QK_PAD:(h + 1) * QK_PAD],
                      q_ref[:, h * QK_PAD:(h + 1) * QK_PAD]) for h in heads]
        if masked:
            sc = [jnp.where(causal, x, -jnp.inf) for x in sc]
        m_old = carry[:N_HEADS]
        l_old = carry[N_HEADS:]
        m_new = [jnp.maximum(m, jnp.max(x, axis=0, keepdims=True)) for m, x in zip(m_old, sc)]
        alpha = [jnp.exp2(mo - mn) for mo, mn in zip(m_old, m_new)]
        p = [jnp.exp2(x - mn) for x, mn in zip(sc, m_new)]
        l_new = [a * l + jnp.sum(x, axis=0, keepdims=True) for a, l, x in zip(alpha, l_old, p)]
        pv = [_dot(vt_ref[h * HEAD_DIM:(h + 1) * HEAD_DIM, pl.ds(start, tq)], x.astype(BF16))
              for h, x in zip(heads, p)]
        for h in heads:
            acc_ref[h] = alpha[h] * acc_ref[h] + pv[h]
        return tuple(m_new) + tuple(l_new)

    acc_ref[...] = jnp.zeros(acc_ref.shape, F32)
    init = (jnp.full((1, tq), -jnp.inf, F32),) * N_HEADS + (jnp.zeros((1, tq), F32),) * N_HEADS
    carry = lax.fori_loop(0, i, functools.partial(block, masked=False), init)
    carry = block(i, carry, True)
    ot = jnp.concatenate([acc_ref[h] / carry[N_HEADS + h] for h in heads], axis=0)
    ot = ot * lax.rsqrt(jnp.mean(ot * ot, axis=0, keepdims=True) + EPS)
    o_ref[...] = (ot.T * g_ref[...]).astype(BF16)


def _attention(q, k, vt, g, b, s):
    tq = min(512, s)
    nq = s // tq
    q3 = q.reshape(b, s, N_HEADS * QK_PAD)
    k3 = k.reshape(b, s, N_HEADS * QK_PAD)
    out = pl.pallas_call(
        _attn_kernel,
        grid=(b, nq),
        in_specs=[pl.BlockSpec((None, tq, N_HEADS * QK_PAD), lambda bi, i: (bi, i, 0)),
                  pl.BlockSpec((None, s, N_HEADS * QK_PAD), lambda bi, i: (bi, 0, 0)),
                  pl.BlockSpec((None, GROUP_W, s), lambda bi, i: (bi, 0, 0)),
                  pl.BlockSpec((1, GROUP_W), lambda bi, i: (0, 0))],
        out_specs=pl.BlockSpec((None, tq, GROUP_W), lambda bi, i: (bi, i, 0)),
        out_shape=jax.ShapeDtypeStruct((b, s, GROUP_W), BF16),
        scratch_shapes=[pltpu.VMEM((N_HEADS, HEAD_DIM, tq), F32)],
        compiler_params=_params(("parallel", "arbitrary")),
        name="mla_attention",
    )(q3, k3, vt, g)
    return out.reshape(b * s, GROUP_W)


GDN_CONV_K = 4


def _gdn_prep_kernel(q_ref, k_ref, v_ref, a_ref, w_ref, alog_ref, dtb_ref,
                     qo_ref, ko_ref, kbo_ref, vbo_ref, la128_ref, la64_ref, buf):
    s = pl.program_id(1)
    ts = q_ref.shape[0]
    blk = LANES
    w = w_ref[...]
    lane = lax.broadcasted_iota(jnp.int32, (blk, LANES), 1)
    for idx, src in enumerate((q_ref, k_ref, v_ref)):
        @pl.when(s == 0)
        def _(idx=idx):
            buf[idx, 0:SUBLANES, :] = jnp.zeros((SUBLANES, GROUP_W), F32)

        @pl.when(s > 0)
        def _(idx=idx):
            buf[idx, 0:SUBLANES, :] = buf[idx, ts:ts + SUBLANES, :]

        buf[idx, SUBLANES:ts + SUBLANES, :] = src[...].astype(F32)

    for b0 in range(0, ts, blk):
        rows_b = slice(b0, b0 + blk)
        gx = a_ref[rows_b, MLA_Q_RANK:MLA_Q_RANK + LANES].astype(F32)
        xs = gx + dtb_ref[...]
        softplus = jnp.maximum(xs, 0.0) + jnp.log1p(jnp.exp(-jnp.abs(xs)))
        log_a = -jnp.exp(alog_ref[...]) * softplus
        beta = _sigmoid(gx)
        la_b = [jnp.broadcast_to(log_a[:, h:h + 1], (blk, LANES)) for h in range(N_HEADS)]
        for h in range(N_HEADS):
            la128_ref[rows_b, h * HEAD_DIM:(h + 1) * HEAD_DIM] = la_b[h]
        for pr in range(N_HEADS // 2):
            la64_ref[rows_b, pr * LANES:(pr + 1) * LANES] = jnp.where(lane < CHUNK, la_b[2 * pr], la_b[2 * pr + 1])
        beta_c = [beta[:, N_HEADS + h:N_HEADS + h + 1] for h in range(N_HEADS)]
        for idx in range(3):
            wi = w[:, idx * GROUP_W:(idx + 1) * GROUP_W]
            y = wi[3:4] * buf[idx, SUBLANES + b0:SUBLANES + b0 + blk, :]
            for j in range(1, GDN_CONV_K):
                y = y + wi[3 - j:4 - j] * buf[idx, SUBLANES + b0 - j:SUBLANES + b0 + blk - j, :]
            y = y * _sigmoid(y)
            for h in range(N_HEADS):
                hs = slice(h * HEAD_DIM, (h + 1) * HEAD_DIM)
                yh = y[:, hs]
                if idx < 2:
                    yh = yh * lax.rsqrt(jnp.sum(yh * yh, axis=-1, keepdims=True) + EPS)
                if idx == 0:
                    qo_ref[rows_b, hs] = (yh * (HEAD_DIM ** -0.5)).astype(BF16)
                elif idx == 1:
                    ko_ref[rows_b, hs] = yh.astype(BF16)
                    kbo_ref[rows_b, hs] = (yh * beta_c[h]).astype(BF16)
                else:
                    vbo_ref[rows_b, hs] = (yh * beta_c[h]).astype(BF16)


def _gdn_prep(proj, w, alog, dtb, b, s):
    ts = _seq_tile(s)
    ns = s // ts

    def seg(k):
        return pl.BlockSpec((None, ts, SEG_W), lambda bi, si: (k, bi * ns + si, 0))

    def rows(wd):
        return pl.BlockSpec((ts, wd), lambda bi, si: (bi * ns + si, 0))

    def full(x):
        return pl.BlockSpec(x.shape, lambda bi, si: (0,) * x.ndim)

    n = b * s
    return pl.pallas_call(
        _gdn_prep_kernel,
        grid=(b, ns),
        in_specs=[seg(SEG_G_Q), seg(SEG_G_K), seg(SEG_G_V), seg(SEG_MLA_Q),
                  full(w), full(alog), full(dtb)],
        out_specs=[rows(GROUP_W)] * 5 + [rows(N_HEADS * CHUNK)],
        out_shape=[jax.ShapeDtypeStruct((n, GROUP_W), BF16)] * 4 + [jax.ShapeDtypeStruct((n, GROUP_W), F32)]
        + [jax.ShapeDtypeStruct((n, N_HEADS * CHUNK), F32)],
        scratch_shapes=[pltpu.VMEM((3, ts + SUBLANES, GROUP_W), F32)],
        compiler_params=_params(("parallel", "arbitrary")),
        name="gdn_prep",
    )(proj, proj, proj, proj, w, alog, dtb)


def _split3(x):
    hi = x.astype(BF16)
    r1 = x - hi.astype(F32)
    mid = r1.astype(BF16)
    lo = (r1 - mid.astype(F32)).astype(BF16)
    return jnp.concatenate([hi, mid, lo], axis=0)


def _gdn_stages(q_ref, k_ref, kb_ref, vb_ref, la128_ref, la64_ref, z_ref, ng_ref, o_ref,
                st_ref, l_s, at_s, rhs_s, wq_s, u_s, kd_s, egl_s):
    c = CHUNK
    pw = N_HEADS * c
    row = lax.broadcasted_iota(jnp.int32, (c, pw), 0)
    col = lax.broadcasted_iota(jnp.int32, (c, pw), 1) % c
    tri = col <= row
    strict = col < row
    eye_f = (col == row).astype(F32)
    diag8 = (col // SUBLANES) == (row // SUBLANES)
    merge_levels = (8, 16, 32)
    merge_masks = [((row & m) != 0) & ((col & m) == 0) & ((row // (2 * m)) == (col // (2 * m)))
                   for m in merge_levels]
    lane_head_p = lax.broadcasted_iota(jnp.int32, (c, pw), 1) // c
    head_sel_p = [(lane_head_p == h).astype(BF16) for h in range(N_HEADS)]
    lane_head_w = lax.broadcasted_iota(jnp.int32, (c, GROUP_W), 1) // HEAD_DIM
    head_sel_w = [(lane_head_w == h).astype(BF16) for h in range(N_HEADS)]
    row3 = lax.broadcasted_iota(jnp.int32, (c, 3 * c), 0)
    col3 = lax.broadcasted_iota(jnp.int32, (c, 3 * c), 1)
    tri3 = ((col3 % c) <= row3).astype(BF16)
    zeros_half = jnp.zeros((c, 2 * HEAD_DIM), BF16)
    ng = ng_ref[...]

    def block_diag(x_b):
        return jnp.concatenate([x_b * head_sel_p[h] for h in range(N_HEADS)], axis=0)

    def pad_rows(y_b, second):
        z = zeros_half[:, :y_b.shape[1]]
        return jnp.concatenate([z, y_b] if second else [y_b, z], axis=0)

    def prepare(ci):
        r0 = ci * c
        rows_c = pl.ds(r0, c)
        dlt = _dot(tri3, _split3(jnp.where(strict, la64_ref[rows_c, :], 0.0)))
        gc = _dot(tri3, _split3(la128_ref[rows_c, :]))
        yield
        decay = jnp.exp(jnp.where(tri, dlt, -jnp.inf))
        eg = jnp.exp(gc)
        kb = kb_ref[rows_c, :]
        k_b = k_ref[rows_c, :]
        q_b = q_ref[rows_c, :]
        k_rows = jnp.concatenate([k_b * head_sel_w[h] for h in range(N_HEADS)], axis=0)
        qk = _dot_nt(jnp.concatenate([kb, q_b], axis=0), k_rows)
        yield
        l_s[ci] = jnp.where(strict, qk[:c] * decay, 0.0).astype(BF16)
        at_s[ci] = jnp.where(tri, qk[c:] * decay, 0.0).astype(BF16)
        kbe = kb.astype(F32) * eg
        qg = (q_b.astype(F32) * eg).astype(BF16)
        gl = gc[c - 1:c, :]
        kd = (k_b.astype(F32) * jnp.exp(gl - gc)).astype(BF16)
        vb = vb_ref[rows_c, :]
        for h in range(N_HEADS):
            hs = slice(h * HEAD_DIM, (h + 1) * HEAD_DIM)
            rhs_s[ci, h, :, 0:HEAD_DIM] = vb[:, hs]
            rhs_s[ci, h, :, HEAD_DIM:2 * HEAD_DIM] = kbe[:, hs].astype(BF16)
            wq_s[ci, h, c:2 * c, :] = qg[:, hs]
            kd_s[ci, h] = kd[:, hs]
        egl_s[ci] = jnp.broadcast_to(jnp.exp(gl), (SUBLANES, GROUP_W))

    def invert(ci):
        lf = l_s[ci].astype(F32)
        n8 = jnp.where(diag8, -lf, 0.0)
        t = eye_f + n8
        p = n8.astype(BF16)
        p = _dot(p, block_diag(p)).astype(BF16)
        yield
        pt = _dot(jnp.concatenate([p, t.astype(BF16)], axis=0), block_diag(p))
        t = t + pt[c:]
        p = pt[:c].astype(BF16)
        yield
        t = t + _dot(t.astype(BF16), block_diag(p))
        yield
        for mask in merge_masks:
            off = jnp.where(mask, lf, 0.0).astype(BF16)
            a = _dot(t.astype(BF16), block_diag(off)).astype(BF16)
            yield
            t = t - _dot(a, block_diag(t.astype(BF16)))
            yield
        tb = t.astype(BF16)
        for h in range(N_HEADS):
            pr = slice((h // 2) * LANES, (h // 2 + 1) * LANES)
            x = _dot(tb[:, pr], pad_rows(rhs_s[ci, h], h % 2))
            u_s[ci, h] = x[:, :HEAD_DIM]
            wq_s[ci, h, 0:c, :] = x[:, HEAD_DIM:].astype(BF16)
        yield

    def recur(cis):
        for ci in cis:
            r0 = ci * c
            at = at_s[ci]
            egl = egl_s[ci]
            sts = [st_ref[h] for h in range(N_HEADS)]
            wss = [_dot(wq_s[ci, h], sts[h].astype(BF16)) for h in range(N_HEADS)]
            yield
            vnbs = [(u_s[ci, h] - wss[h][:c]).astype(BF16) for h in range(N_HEADS)]
            for h in range(N_HEADS):
                hs = slice(h * HEAD_DIM, (h + 1) * HEAD_DIM)
                pr = slice((h // 2) * LANES, (h // 2 + 1) * LANES)
                out = wss[h][c:] + _dot(at[:, pr], pad_rows(vnbs[h], h % 2))
                st_ref[h] = sts[h] * egl[0:1, hs] + _dot_tn(kd_s[ci, h], vnbs[h])
                zh = z_ref[pl.ds(r0, c), hs].astype(F32)
                o_ref[pl.ds(r0, c), hs] = (_rms(out, ng) * (zh * _sigmoid(zh))).astype(BF16)
            yield

    return prepare, invert, recur


def _hgrn_stages(q_ref, f_ref, i_ref, z_ref, lbl_ref, ng_ref, o_ref, st_ref, at_s, qe_s, kd_s, dec_s, layer):
    c = CHUNK
    logits = lbl_ref[...]
    e = jnp.exp(logits - jnp.max(logits, axis=0, keepdims=True))
    p = e / jnp.sum(e, axis=0, keepdims=True)
    lb = jnp.sum(p[0:layer + 1], axis=0, keepdims=True) - p[0:1]
    ng = ng_ref[...]

    tt = lax.broadcasted_iota(jnp.int32, (c, c), 0)
    ss = lax.broadcasted_iota(jnp.int32, (c, c), 1)
    trow = lax.broadcasted_iota(jnp.int32, (c, HEAD_DIM), 0)
    levels = (1, 2, 4, 8, 16, 32)
    pair_masks = [((tt & m) != 0) & ((ss & m) == 0) & ((tt // (2 * m)) == (ss // (2 * m))) for m in levels]
    odd_rows = [(trow & m) != 0 for m in levels]
    eye = tt == ss

    def local(ci, h):
        r0 = ci * c
        slot = ci % 2
        hs = slice(h * HEAD_DIM, (h + 1) * HEAD_DIM)
        fr = f_ref[pl.ds(r0, c), hs].astype(F32)
        rq = q_ref[pl.ds(r0, c), hs].astype(F32)
        lbh = lb[:, hs]
        lf = jnp.log2(lbh + (1.0 - lbh) * _sigmoid(fr))
        kk = (1.0 - lbh) * _sigmoid(-fr)
        qq = rq * _sigmoid(rq)
        attn = jnp.where(eye, _dot_nt(qq.astype(BF16), kk.astype(BF16)), 0.0)
        yield
        cs = lf
        tot = lf
        for m, pmask, odd in zip(levels, pair_masks, odd_rows):
            qm = (qq * jnp.exp2(cs)).astype(BF16)
            km = (kk * jnp.exp2(tot - cs)).astype(BF16)
            attn = jnp.where(pmask, _dot_nt(qm, km), attn)
            if m < SUBLANES:
                prev = pltpu.roll(tot, m, axis=0)
                nxt = pltpu.roll(tot, c - m, axis=0)
                cs = cs + jnp.where(odd, prev, 0.0)
                tot = tot + jnp.where(odd, prev, nxt)
            else:
                cs_parts, tot_parts = [], []
                for b0 in range(0, c, 2 * m):
                    t_even = tot[b0:b0 + m]
                    t_both = t_even + tot[b0 + m:b0 + 2 * m]
                    cs_parts += [cs[b0:b0 + m], cs[b0 + m:b0 + 2 * m] + t_even]
                    tot_parts += [t_both, t_both]
                cs = jnp.concatenate(cs_parts, axis=0)
                tot = jnp.concatenate(tot_parts, axis=0)
            yield
        at_s[slot, h] = attn.astype(BF16)
        qe_s[slot, h] = (qq * jnp.exp2(cs)).astype(BF16)
        kd_s[slot, h] = (kk * jnp.exp2(tot - cs)).astype(BF16)
        dec_s[slot, h] = jnp.exp2(tot[0:SUBLANES, :])

    def tail(ci, h):
        r0 = ci * c
        slot = ci % 2
        hs = slice(h * HEAD_DIM, (h + 1) * HEAD_DIM)
        ii = i_ref[pl.ds(r0, c), hs]
        st = st_ref[h]
        out = _dot_nt(qe_s[slot, h], st.astype(BF16)) + _dot(at_s[slot, h], ii)
        st_ref[h] = st * dec_s[slot, h, 0:1, :] + _dot_tn(ii, kd_s[slot, h])
        yield
        zh = z_ref[pl.ds(r0, c), hs].astype(F32)
        o_ref[pl.ds(r0, c), hs] = (_rms(out, ng) * _sigmoid(zh)).astype(BF16)

    return local, tail


def _round_robin(gens):
    gens = list(gens)
    while gens:
        alive = []
        for g in gens:
            try:
                next(g)
                alive.append(g)
            except StopIteration:
                pass
        gens = alive
        yield


def _chain(gens):
    for g in gens:
        yield from g


def _recurrent_kernel(gq_ref, gk_ref, gkb_ref, gvb_ref, la128_ref, la64_ref, gz_ref, gng_ref,
                      rq_ref, rf_ref, ri_ref, rz_ref, lbl_ref, hng_ref, og_ref, oh_ref,
                      gst_ref, l_s, at_s, rhs_s, wq_s, u_s, kd_s, egl_s,
                      hst_ref, hat_s, hqe_s, hkd_s, hdec_s, *, layer):
    @pl.when(pl.program_id(1) == 0)
    def _():
        gst_ref[...] = jnp.zeros(gst_ref.shape, F32)
        hst_ref[...] = jnp.zeros(hst_ref.shape, F32)

    prepare, invert, recur = _gdn_stages(gq_ref, gk_ref, gkb_ref, gvb_ref, la128_ref, la64_ref, gz_ref, gng_ref,
                                         og_ref, gst_ref, l_s, at_s, rhs_s, wq_s, u_s, kd_s, egl_s)
    local, tail = _hgrn_stages(rq_ref, rf_ref, ri_ref, rz_ref, lbl_ref, hng_ref, oh_ref,
                               hst_ref, hat_s, hqe_s, hkd_s, hdec_s, layer)
    n = gq_ref.shape[0] // CHUNK
    heads = range(N_HEADS)

    def hgrn_step(ci):
        gens = [tail(ci - 1, h) for h in heads] if ci > 0 else []
        gens += [local(ci, h) for h in heads] if ci < n else []
        return _round_robin(gens)

    per_phase = [1] + [2] * (n // 2 - 1) + [1, 1]
    first = [sum(per_phase[:k]) for k in range(len(per_phase))]

    for k in range(n // 2 + 2):
        gens = []
        if k < n // 2:
            gens += [prepare(2 * k), prepare(2 * k + 1)]
        if 1 <= k <= n // 2:
            gens += [invert(2 * k - 2), invert(2 * k - 1)]
        if 2 <= k:
            gens += [recur([2 * k - 4, 2 * k - 3])]
        gens += [_chain([hgrn_step(first[k] + i) for i in range(per_phase[k])])]
        for _ in _round_robin(gens):
            pass


def _recurrent(gdn_in, proj, gng, lbl, hng, layer, b, s):
    ts = _seq_tile(s)
    ns = s // ts
    nc = ts // CHUNK
    assert nc % 2 == 0

    def rows(wd):
        return pl.BlockSpec((ts, wd), lambda bi, si: (bi * ns + si, 0))

    def seg(k):
        return pl.BlockSpec((None, ts, SEG_W), lambda bi, si: (k, bi * ns + si, 0))

    head_vec = pl.BlockSpec((1, HEAD_DIM), lambda bi, si: (0, 0))
    n = b * s
    return pl.pallas_call(
        functools.partial(_recurrent_kernel, layer=layer),
        grid=(b, ns),
        in_specs=[rows(GROUP_W)] * 5 + [rows(N_HEADS * CHUNK), seg(SEG_G_Z), head_vec,
                  seg(SEG_R_Q), seg(SEG_R_F), seg(SEG_R_I), seg(SEG_R_Z),
                  pl.BlockSpec(lbl.shape, lambda bi, si: (0, 0)), head_vec],
        out_specs=[rows(GROUP_W), rows(GROUP_W)],
        out_shape=[jax.ShapeDtypeStruct((n, GROUP_W), BF16)] * 2,
        scratch_shapes=[pltpu.VMEM((N_HEADS, HEAD_DIM, HEAD_DIM), F32),
                        pltpu.VMEM((nc, CHUNK, N_HEADS * CHUNK), BF16),
                        pltpu.VMEM((nc, CHUNK, N_HEADS * CHUNK), BF16),
                        pltpu.VMEM((nc, N_HEADS, CHUNK, 2 * HEAD_DIM), BF16),
                        pltpu.VMEM((nc, N_HEADS, 2 * CHUNK, HEAD_DIM), BF16),
                        pltpu.VMEM((nc, N_HEADS, CHUNK, HEAD_DIM), F32),
                        pltpu.VMEM((nc, N_HEADS, CHUNK, HEAD_DIM), BF16),
                        pltpu.VMEM((nc, SUBLANES, GROUP_W), F32),
                        pltpu.VMEM((N_HEADS, HEAD_DIM, HEAD_DIM), F32),
                        pltpu.VMEM((2, N_HEADS, CHUNK, CHUNK), BF16),
                        pltpu.VMEM((2, N_HEADS, CHUNK, HEAD_DIM), BF16),
                        pltpu.VMEM((2, N_HEADS, CHUNK, HEAD_DIM), BF16),
                        pltpu.VMEM((2, N_HEADS, SUBLANES, HEAD_DIM), F32)],
        compiler_params=_params(("parallel", "arbitrary")),
        name="recurrent",
    )(*gdn_in, proj, gng, proj, proj, proj, proj, lbl, hng)


def _mix_mlp_kernel(ya_ref, yb_ref, yc_ref, yd_ref, wo_ref, h_ref, g_ref, w1_ref, w2_ref, fg_ref,
                    o_ref, xn_ref, *, final_norm):
    f = pl.program_id(1)

    @pl.when(f == 0)
    def _():
        y = jnp.concatenate([ya_ref[...], yb_ref[...], yc_ref[...], yd_ref[...]], axis=1)
        h1 = h_ref[...] + _dot(y, wo_ref[...])
        xn_ref[...] = _rms(h1, g_ref[...]).astype(BF16)
        o_ref[...] = h1

    a = jnp.maximum(_dot(xn_ref[...], w1_ref[...]), 0.0)
    o_ref[...] += _dot((a * a).astype(BF16), w2_ref[...])

    if final_norm:
        @pl.when(f == pl.num_programs(1) - 1)
        def _():
            o_ref[...] = _rms(o_ref[...], fg_ref[...])


def _mix_mlp(ys, wo, h, g, w1, w2, fg, layer, final_norm):
    n, d = h.shape
    dff = w1.shape[2]
    tm = min(512, n)
    tf = 1024
    ysp = pl.BlockSpec((tm, GROUP_W), lambda i, f: (i, 0))
    rows = pl.BlockSpec((tm, d), lambda i, f: (i, 0))
    vec = pl.BlockSpec((1, d), lambda i, f: (0, 0))
    return pl.pallas_call(
        functools.partial(_mix_mlp_kernel, final_norm=final_norm),
        grid=(n // tm, dff // tf),
        in_specs=[ysp, ysp, ysp, ysp,
                  pl.BlockSpec((None, d, d), lambda i, f: (layer, 0, 0), pipeline_mode=pl.Buffered(1)),
                  rows, vec,
                  pl.BlockSpec((None, d, tf), lambda i, f: (layer, 0, f)),
                  pl.BlockSpec((None, tf, d), lambda i, f: (layer, f, 0)),
                  vec],
        out_specs=rows,
        out_shape=jax.ShapeDtypeStruct((n, d), F32),
        scratch_shapes=[pltpu.VMEM((tm, d), BF16)],
        compiler_params=_params(("parallel", "arbitrary")),
        name="mix_mlp",
    )(*ys, wo, h, g, w1, w2, fg)


def _pack_w_in(w_in):
    wt = jnp.swapaxes(w_in, 1, 2)
    widths = (GROUP_W, GROUP_W, GROUP_W, MLA_Q_RANK, MLA_KV_RANK, MLA_ROPE,
              GROUP_W, GROUP_W, GROUP_W, GROUP_W, N_HEADS, N_HEADS,
              GROUP_W, GROUP_W, GROUP_W, GROUP_W)
    rows, start = [], 0
    for wd in widths:
        rows.append(wt[:, start:start + wd])
        start += wd
    (sc_x, sc_c, sc_b, m_cq, m_ckv, m_kr, g_q, g_k, g_v, g_z, g_a, g_b, r_q, r_f, r_i, r_z) = rows
    l, _, d = wt.shape

    def zeros(n):
        return jnp.zeros((l, n, d), wt.dtype)

    half = MLA_ROPE // 2
    kr_swapped = jnp.concatenate([m_kr[:, half:], m_kr[:, :half]], axis=1)
    seg_q = [m_cq, g_a, g_b, zeros(SEG_W - MLA_Q_RANK - 2 * N_HEADS)]
    seg_kv = [m_ckv, m_kr, zeros(LANES - MLA_ROPE), kr_swapped, zeros(LANES - MLA_ROPE)]
    segs = [sc_x, sc_c, sc_b, g_q, g_k, g_v, g_z, r_q, r_f, r_i, r_z] + seg_q + seg_kv
    return jnp.concatenate(segs, axis=1).astype(BF16)


def _pack_mla_weights(w_uq, w_ukv):
    l = w_uq.shape[0]
    half = MLA_ROPE // 2
    wq = w_uq.reshape(l, MLA_Q_RANK, N_HEADS, MLA_NOPE + MLA_ROPE)
    nope, rp = wq[..., :MLA_NOPE], wq[..., MLA_NOPE:]
    z64 = jnp.zeros(rp.shape, rp.dtype)
    wqa = jnp.concatenate([nope, rp, z64], axis=-1).reshape(l, MLA_Q_RANK, N_HEADS * QK_PAD)
    rp_sw = jnp.concatenate([rp[..., half:], rp[..., :half]], axis=-1)
    wqb = jnp.concatenate([rp_sw, z64], axis=-1).reshape(l, MLA_Q_RANK, N_HEADS * LANES)
    wkv = w_ukv.reshape(l, MLA_KV_RANK, N_HEADS, MLA_NOPE + HEAD_DIM)
    wk = wkv[..., :MLA_NOPE].reshape(l, MLA_KV_RANK, N_HEADS * MLA_NOPE)
    wv = wkv[..., MLA_NOPE:].reshape(l, MLA_KV_RANK, N_HEADS * HEAD_DIM)
    return wqa.astype(BF16), wqb.astype(BF16), wk.astype(BF16), jnp.swapaxes(wv, 1, 2).astype(BF16)


def _pad_lanes(x):
    l, k = x.shape
    return jnp.concatenate([x, jnp.zeros((l, LANES - k), x.dtype)], axis=-1).reshape(l, 1, LANES)


def kernel(x, positions, norm1_g, w_in, sconv_w, sconv_out_g, mla_q_g, mla_kv_g, mla_w_uq, mla_w_ukv,
           mla_out_g, gdn_conv_w, gdn_a_log, gdn_dt_bias, gdn_norm_g, hgrn_lb_logits, hgrn_norm_g,
           w_o, norm2_g, w_ff1, w_ff2, final_g):
    b, s, d = x.shape
    n = b * s
    depth = w_in.shape[0]

    w_in_p = _pack_w_in(w_in)
    wqa, wqb, wk, wvt = _pack_mla_weights(mla_w_uq, mla_w_ukv)
    w_o_b = w_o.astype(BF16)
    w1_b = w_ff1.astype(BF16)
    w2_b = w_ff2.astype(BF16)
    alog_p = _pad_lanes(gdn_a_log)
    dtb_p = _pad_lanes(gdn_dt_bias)

    half = MLA_ROPE // 2
    inv_freq = ROPE_THETA ** (-jnp.arange(half, dtype=F32) / half)
    invf = jnp.concatenate([inv_freq, inv_freq, jnp.zeros((LANES - MLA_ROPE,), F32)]).reshape(1, LANES)
    cos, sin = _rope_tables(positions.astype(F32).reshape(n, 1), invf)

    h = x.reshape(n, d)
    for l in range(depth):
        proj = _in_proj(h, norm1_g[l].reshape(1, d), w_in_p, l)
        y_sc = _sconv(proj, sconv_w[l], sconv_out_g[l].reshape(1, GROUP_W), b, s)
        q, k, vt = _mla_up(proj, cos, sin, mla_q_g[l].reshape(1, -1), mla_kv_g[l].reshape(1, -1),
                           wqa[l], wqb[l], wk[l], wvt[l], b, s)
        y_mla = _attention(q, k, vt, mla_out_g[l].reshape(1, GROUP_W), b, s)
        gdn_in = _gdn_prep(proj, gdn_conv_w[l], alog_p[l], dtb_p[l], b, s)
        y_gdn, y_hg = _recurrent(gdn_in, proj, gdn_norm_g[l].reshape(1, HEAD_DIM), hgrn_lb_logits,
                                 hgrn_norm_g[l].reshape(1, HEAD_DIM), l, b, s)
        h = _mix_mlp((y_sc, y_mla, y_gdn, y_hg), w_o_b, h, norm2_g[l].reshape(1, d), w1_b, w2_b,
                     final_g.reshape(1, d), l, final_norm=(l == depth - 1))
    return h.reshape(b, s, d)
```

```python
import functools

import jax
import jax.numpy as jnp
from jax import lax
from jax.experimental import pallas as pl
from jax.experimental.pallas import tpu as pltpu

F32 = jnp.float32
BF16 = jnp.bfloat16
HIGHEST = lax.Precision.HIGHEST

D_MODEL = 2048
GROUP_W = 512
HEAD_DIM = 128
N_HEADS = 4
MLA_Q_RANK = 384
MLA_KV_RANK = 256
MLA_NOPE = 128
MLA_ROPE = 64
ROPE_THETA = 10000.0
CHUNK = 64
D_FF = 4 * D_MODEL
EPS = 1e-6
LOG2_E = 1.4426950408889634

SEG_W = 512
(SEG_SC_X, SEG_SC_C, SEG_SC_B, SEG_G_Q, SEG_G_K, SEG_G_V, SEG_G_Z,
 SEG_R_Q, SEG_R_F, SEG_R_I, SEG_R_Z, SEG_MLA_Q, SEG_MLA_KV) = range(13)
N_SEG = 13
QK_PAD = 256
LANES = 128
SUBLANES = 8
VMEM_LIMIT = 56 * 1024 * 1024


def _params(sem, vmem=VMEM_LIMIT):
    return pltpu.CompilerParams(dimension_semantics=sem, vmem_limit_bytes=vmem)


def _dot(a, b, precision=None):
    return jnp.dot(a, b, preferred_element_type=F32, precision=precision)


def _dot_nt(a, b, precision=None):
    return lax.dot_general(a, b, (((1,), (1,)), ((), ())), preferred_element_type=F32, precision=precision)


def _dot_tn(a, b, precision=None):
    return lax.dot_general(a, b, (((0,), (0,)), ((), ())), preferred_element_type=F32, precision=precision)


def _rms(x, g):
    return x * lax.rsqrt(jnp.mean(x * x, axis=-1, keepdims=True) + EPS) * g


def _sigmoid(x):
    return jax.nn.sigmoid(x)


def _inproj_kernel(x_ref, g_ref, w_ref, o_ref, xn_ref):
    @pl.when(pl.program_id(1) == 0)
    def _():
        xn_ref[...] = _rms(x_ref[...], g_ref[...]).astype(BF16)

    o_ref[...] = _dot_nt(xn_ref[...], w_ref[...]).astype(o_ref.dtype)


def _in_proj(h, g, w, layer):
    n, d = h.shape
    tm = min(1024, n)
    return pl.pallas_call(
        _inproj_kernel,
        grid=(n // tm, N_SEG),
        in_specs=[pl.BlockSpec((tm, d), lambda i, j: (i, 0)),
                  pl.BlockSpec((1, d), lambda i, j: (0, 0)),
                  pl.BlockSpec((None, SEG_W, d), lambda i, j: (layer, j, 0))],
        out_specs=pl.BlockSpec((None, tm, SEG_W), lambda i, j: (j, i, 0)),
        out_shape=jax.ShapeDtypeStruct((N_SEG, n, SEG_W), BF16),
        scratch_shapes=[pltpu.VMEM((tm, d), BF16)],
        compiler_params=_params(("parallel", "arbitrary")),
        name="in_proj",
    )(h, g, w)


def _sconv_kernel(x_ref, c_ref, b_ref, w_ref, g_ref, o_ref, zbuf):
    s = pl.program_id(1)
    ts = x_ref.shape[0]

    @pl.when(s == 0)
    def _():
        zbuf[0:SUBLANES, :] = jnp.zeros((SUBLANES, GROUP_W), F32)

    @pl.when(s > 0)
    def _():
        zbuf[0:SUBLANES, :] = zbuf[ts:ts + SUBLANES, :]

    zbuf[SUBLANES:ts + SUBLANES, :] = c_ref[...].astype(F32) * x_ref[...].astype(F32)
    w = w_ref[...]
    y = (w[2:3] * zbuf[SUBLANES:ts + SUBLANES, :]
         + w[1:2] * zbuf[SUBLANES - 1:ts + SUBLANES - 1, :]
         + w[0:1] * zbuf[SUBLANES - 2:ts + SUBLANES - 2, :])
    o_ref[...] = _rms(b_ref[...].astype(F32) * y, g_ref[...]).astype(BF16)


def _seq_tile(s):
    return min(512, s)


def _sconv(proj, w, g, b, s):
    ts = _seq_tile(s)
    ns = s // ts

    def seg(k):
        return pl.BlockSpec((None, ts, SEG_W), lambda bi, si: (k, bi * ns + si, 0))

    return pl.pallas_call(
        _sconv_kernel,
        grid=(b, ns),
        in_specs=[seg(SEG_SC_X), seg(SEG_SC_C), seg(SEG_SC_B),
                  pl.BlockSpec((3, GROUP_W), lambda bi, si: (0, 0)),
                  pl.BlockSpec((1, GROUP_W), lambda bi, si: (0, 0))],
        out_specs=pl.BlockSpec((ts, GROUP_W), lambda bi, si: (bi * ns + si, 0)),
        out_shape=jax.ShapeDtypeStruct((b * s, GROUP_W), BF16),
        scratch_shapes=[pltpu.VMEM((ts + SUBLANES, GROUP_W), F32)],
        compiler_params=_params(("parallel", "arbitrary")),
        name="sconv",
    )(proj, proj, proj, w, g)


def _rope_kernel(pos_ref, invf_ref, cos_ref, sin_ref):
    ang = pos_ref[...] * invf_ref[...]
    lane = lax.broadcasted_iota(jnp.int32, ang.shape, 1)
    half = MLA_ROPE // 2
    cos_ref[...] = jnp.where(lane < MLA_ROPE, jnp.cos(ang), 0.0)
    sn = jnp.sin(ang)
    sin_ref[...] = jnp.where(lane < half, -sn, jnp.where(lane < MLA_ROPE, sn, 0.0))


def _rope_tables(pos_col, invf):
    n = pos_col.shape[0]
    tm = min(2048, n)
    return pl.pallas_call(
        _rope_kernel,
        grid=(n // tm,),
        in_specs=[pl.BlockSpec((tm, 1), lambda i: (i, 0)),
                  pl.BlockSpec((1, LANES), lambda i: (0, 0))],
        out_specs=[pl.BlockSpec((tm, LANES), lambda i: (i, 0))] * 2,
        out_shape=[jax.ShapeDtypeStruct((n, LANES), F32)] * 2,
        compiler_params=_params(("parallel",)),
        name="rope_tables",
    )(pos_col, invf)


def _mla_up_kernel(a_ref, b_ref, cos_ref, sin_ref, gq_ref, gkv_ref,
                   wqa_ref, wqb_ref, wk_ref, wvt_ref, q_ref, k_ref, vt_ref):
    scale = (MLA_NOPE + MLA_ROPE) ** -0.5 * LOG2_E
    cos = cos_ref[...]
    sin = sin_ref[...]
    a = a_ref[...].astype(F32)
    cqn = _rms(a[:, :MLA_Q_RANK], gq_ref[...]).astype(BF16)
    qa = _dot(cqn, wqa_ref[...])
    qb = _dot(cqn, wqb_ref[...])
    for h in range(N_HEADS):
        o = h * QK_PAD
        q_ref[:, o:o + LANES] = (qa[:, o:o + LANES] * scale).astype(BF16)
        roped = qa[:, o + LANES:o + QK_PAD] * cos + qb[:, h * LANES:(h + 1) * LANES] * sin
        q_ref[:, o + LANES:o + QK_PAD] = (roped * scale).astype(BF16)
    bb = b_ref[...].astype(F32)
    ckvn = _rms(bb[:, :MLA_KV_RANK], gkv_ref[...]).astype(BF16)
    kk = _dot(ckvn, wk_ref[...])
    vt_ref[...] = _dot_nt(wvt_ref[...], ckvn).astype(BF16)
    kr = (bb[:, MLA_KV_RANK:MLA_KV_RANK + LANES] * cos
          + bb[:, MLA_KV_RANK + LANES:MLA_KV_RANK + 2 * LANES] * sin).astype(BF16)
    for h in range(N_HEADS):
        o = h * QK_PAD
        k_ref[:, o:o + LANES] = kk[:, h * LANES:(h + 1) * LANES].astype(BF16)
        k_ref[:, o + LANES:o + QK_PAD] = kr


def _mla_up(proj, cos, sin, gq, gkv, wqa, wqb, wk, wvt, b, s):
    n = proj.shape[1]
    tm = min(512, s)
    ns = s // tm

    def seg(k):
        return pl.BlockSpec((None, tm, SEG_W), lambda i: (k, i, 0))

    def full(x):
        return pl.BlockSpec(x.shape, lambda i: (0,) * x.ndim)

    def rows(w):
        return pl.BlockSpec((tm, w), lambda i: (i, 0))

    return pl.pallas_call(
        _mla_up_kernel,
        grid=(n // tm,),
        in_specs=[seg(SEG_MLA_Q), seg(SEG_MLA_KV), rows(LANES), rows(LANES),
                  full(gq), full(gkv), full(wqa), full(wqb), full(wk), full(wvt)],
        out_specs=[rows(N_HEADS * QK_PAD), rows(N_HEADS * QK_PAD),
                   pl.BlockSpec((None, GROUP_W, tm), lambda i: (i // ns, 0, i % ns))],
        out_shape=[jax.ShapeDtypeStruct((n, N_HEADS * QK_PAD), BF16),
                   jax.ShapeDtypeStruct((n, N_HEADS * QK_PAD), BF16),
                   jax.ShapeDtypeStruct((b, GROUP_W, s), BF16)],
        compiler_params=_params(("parallel",)),
        name="mla_up",
    )(proj, proj, cos, sin, gq, gkv, wqa, wqb, wk, wvt)


def _attn_kernel(q_ref, k_ref, vt_ref, g_ref, o_ref, acc_ref):
    i = pl.program_id(1)
    tq = q_ref.shape[0]
    kv_i = lax.broadcasted_iota(jnp.int32, (tq, tq), 0)
    q_i = lax.broadcasted_iota(jnp.int32, (tq, tq), 1)
    causal = kv_i <= q_i
    heads = range(N_HEADS)

    def block(j, carry, masked):
        start = pl.multiple_of(j * tq, tq)
        sc = [_dot_nt(k_ref[pl.ds(start, tq), h * QK_PAD:(h + 1) * QK_PAD],
                      q_ref[:, h * QK_PAD:(h + 1) * QK_PAD]) for h in heads]
        if masked:
            sc = [jnp.where(causal, x, -jnp.inf) for x in sc]
        m_old = carry[:N_HEADS]
        l_old = carry[N_HEADS:]
        m_new = [jnp.maximum(m, jnp.max(x, axis=0, keepdims=True)) for m, x in zip(m_old, sc)]
        alpha = [jnp.exp2(mo - mn) for mo, mn in zip(m_old, m_new)]
        p = [jnp.exp2(x - mn) for x, mn in zip(sc, m_new)]
        l_new = [a * l + jnp.sum(x, axis=0, keepdims=True) for a, l, x in zip(alpha, l_old, p)]
        pv = [_dot(vt_ref[h * HEAD_DIM:(h + 1) * HEAD_DIM, pl.ds(start, tq)], x.astype(BF16))
              for h, x in zip(heads, p)]
        for h in heads:
            acc_ref[h] = alpha[h] * acc_ref[h] + pv[h]
        return tuple(m_new) + tuple(l_new)

    acc_ref[...] = jnp.zeros(acc_ref.shape, F32)
    init = (jnp.full((1, tq), -jnp.inf, F32),) * N_HEADS + (jnp.zeros((1, tq), F32),) * N_HEADS
    carry = lax.fori_loop(0, i, functools.partial(block, masked=False), init)
    carry = block(i, carry, True)
    ot = jnp.concatenate([acc_ref[h] / carry[N_HEADS + h] for h in heads], axis=0)
    ot = ot * lax.rsqrt(jnp.mean(ot * ot, axis=0, keepdims=True) + EPS)
    o_ref[...] = (ot.T * g_ref[...]).astype(BF16)


def _attention(q, k, vt, g, b, s):
    tq = min(512, s)
    nq = s // tq
    q3 = q.reshape(b, s, N_HEADS * QK_PAD)
    k3 = k.reshape(b, s, N_HEADS * QK_PAD)
    out = pl.pallas_call(
        _attn_kernel,
        grid=(b, nq),
        in_specs=[pl.BlockSpec((None, tq, N_HEADS * QK_PAD), lambda bi, i: (bi, i, 0)),
                  pl.BlockSpec((None, s, N_HEADS * QK_PAD), lambda bi, i: (bi, 0, 0)),
                  pl.BlockSpec((None, GROUP_W, s), lambda bi, i: (bi, 0, 0)),
                  pl.BlockSpec((1, GROUP_W), lambda bi, i: (0, 0))],
        out_specs=pl.BlockSpec((None, tq, GROUP_W), lambda bi, i: (bi, i, 0)),
        out_shape=jax.ShapeDtypeStruct((b, s, GROUP_W), BF16),
        scratch_shapes=[pltpu.VMEM((N_HEADS, HEAD_DIM, tq), F32)],
        compiler_params=_params(("parallel", "arbitrary")),
        name="mla_attention",
    )(q3, k3, vt, g)
    return out.reshape(b * s, GROUP_W)


GDN_CONV_K = 4


def _gdn_prep_kernel(q_ref, k_ref, v_ref, a_ref, w_ref, alog_ref, dtb_ref,
                     qo_ref, ko_ref, kbo_ref, vbo_ref, la128_ref, la64_ref, buf):
    s = pl.program_id(1)
    ts = q_ref.shape[0]
    blk = LANES
    w = w_ref[...]
    lane = lax.broadcasted_iota(jnp.int32, (blk, LANES), 1)
    for idx, src in enumerate((q_ref, k_ref, v_ref)):
        @pl.when(s == 0)
        def _(idx=idx):
            buf[idx, 0:SUBLANES, :] = jnp.zeros((SUBLANES, GROUP_W), F32)

        @pl.when(s > 0)
        def _(idx=idx):
            buf[idx, 0:SUBLANES, :] = buf[idx, ts:ts + SUBLANES, :]

        buf[idx, SUBLANES:ts + SUBLANES, :] = src[...].astype(F32)

    for b0 in range(0, ts, blk):
        rows_b = slice(b0, b0 + blk)
        gx = a_ref[rows_b, MLA_Q_RANK:MLA_Q_RANK + LANES].astype(F32)
        xs = gx + dtb_ref[...]
        softplus = jnp.maximum(xs, 0.0) + jnp.log1p(jnp.exp(-jnp.abs(xs)))
        log_a = -jnp.exp(alog_ref[...]) * softplus
        beta = _sigmoid(gx)
        la_b = [jnp.broadcast_to(log_a[:, h:h + 1], (blk, LANES)) for h in range(N_HEADS)]
        for h in range(N_HEADS):
            la128_ref[rows_b, h * HEAD_DIM:(h + 1) * HEAD_DIM] = la_b[h]
        for pr in range(N_HEADS // 2):
            la64_ref[rows_b, pr * LANES:(pr + 1) * LANES] = jnp.where(lane < CHUNK, la_b[2 * pr], la_b[2 * pr + 1])
        beta_c = [beta[:, N_HEADS + h:N_HEADS + h + 1] for h in range(N_HEADS)]
        for idx in range(3):
            wi = w[:, idx * GROUP_W:(idx + 1) * GROUP_W]
            y = wi[3:4] * buf[idx, SUBLANES + b0:SUBLANES + b0 + blk, :]
            for j in range(1, GDN_CONV_K):
                y = y + wi[3 - j:4 - j] * buf[idx, SUBLANES + b0 - j:SUBLANES + b0 + blk - j, :]
            y = y * _sigmoid(y)
            for h in range(N_HEADS):
                hs = slice(h * HEAD_DIM, (h + 1) * HEAD_DIM)
                yh = y[:, hs]
                if idx < 2:
                    yh = yh * lax.rsqrt(jnp.sum(yh * yh, axis=-1, keepdims=True) + EPS)
                if idx == 0:
                    qo_ref[rows_b, hs] = (yh * (HEAD_DIM ** -0.5)).astype(BF16)
                elif idx == 1:
                    ko_ref[rows_b, hs] = yh.astype(BF16)
                    kbo_ref[rows_b, hs] = (yh * beta_c[h]).astype(BF16)
                else:
                    vbo_ref[rows_b, hs] = (yh * beta_c[h]).astype(BF16)


def _gdn_prep(proj, w, alog, dtb, b, s):
    ts = _seq_tile(s)
    ns = s // ts

    def seg(k):
        return pl.BlockSpec((None, ts, SEG_W), lambda bi, si: (k, bi * ns + si, 0))

    def rows(wd):
        return pl.BlockSpec((ts, wd), lambda bi, si: (bi * ns + si, 0))

    def full(x):
        return pl.BlockSpec(x.shape, lambda bi, si: (0,) * x.ndim)

    n = b * s
    return pl.pallas_call(
        _gdn_prep_kernel,
        grid=(b, ns),
        in_specs=[seg(SEG_G_Q), seg(SEG_G_K), seg(SEG_G_V), seg(SEG_MLA_Q),
                  full(w), full(alog), full(dtb)],
        out_specs=[rows(GROUP_W)] * 5 + [rows(N_HEADS * CHUNK)],
        out_shape=[jax.ShapeDtypeStruct((n, GROUP_W), BF16)] * 4 + [jax.ShapeDtypeStruct((n, GROUP_W), F32)]
        + [jax.ShapeDtypeStruct((n, N_HEADS * CHUNK), F32)],
        scratch_shapes=[pltpu.VMEM((3, ts + SUBLANES, GROUP_W), F32)],
        compiler_params=_params(("parallel", "arbitrary")),
        name="gdn_prep",
    )(proj, proj, proj, proj, w, alog, dtb)


def _split3(x):
    hi = x.astype(BF16)
    r1 = x - hi.astype(F32)
    mid = r1.astype(BF16)
    lo = (r1 - mid.astype(F32)).astype(BF16)
    return jnp.concatenate([hi, mid, lo], axis=0)


def _gdn_stages(q_ref, k_ref, kb_ref, vb_ref, la128_ref, la64_ref, z_ref, ng_ref, o_ref,
                st_ref, l_s, at_s, rhs_s, wq_s, u_s, kd_s, egl_s):
    c = CHUNK
    pw = N_HEADS * c
    row = lax.broadcasted_iota(jnp.int32, (c, pw), 0)
    col = lax.broadcasted_iota(jnp.int32, (c, pw), 1) % c
    tri = col <= row
    strict = col < row
    eye_f = (col == row).astype(F32)
    diag8 = (col // SUBLANES) == (row // SUBLANES)
    merge_levels = (8, 16, 32)
    merge_masks = [((row & m) != 0) & ((col & m) == 0) & ((row // (2 * m)) == (col // (2 * m)))
                   for m in merge_levels]
    lane_head_p = lax.broadcasted_iota(jnp.int32, (c, pw), 1) // c
    head_sel_p = [(lane_head_p == h).astype(BF16) for h in range(N_HEADS)]
    lane_head_w = lax.broadcasted_iota(jnp.int32, (c, GROUP_W), 1) // HEAD_DIM
    head_sel_w = [(lane_head_w == h).astype(BF16) for h in range(N_HEADS)]
    row3 = lax.broadcasted_iota(jnp.int32, (c, 3 * c), 0)
    col3 = lax.broadcasted_iota(jnp.int32, (c, 3 * c), 1)
    tri3 = ((col3 % c) <= row3).astype(BF16)
    zeros_half = jnp.zeros((c, 2 * HEAD_DIM), BF16)
    ng = ng_ref[...]

    def block_diag(x_b):
        return jnp.concatenate([x_b * head_sel_p[h] for h in range(N_HEADS)], axis=0)

    def pad_rows(y_b, second):
        z = zeros_half[:, :y_b.shape[1]]
        return jnp.concatenate([z, y_b] if second else [y_b, z], axis=0)

    def prepare(ci):
        r0 = ci * c
        rows_c = pl.ds(r0, c)
        dlt = _dot(tri3, _split3(jnp.where(strict, la64_ref[rows_c, :], 0.0)))
        gc = _dot(tri3, _split3(la128_ref[rows_c, :]))
        yield
        decay = jnp.exp(jnp.where(tri, dlt, -jnp.inf))
        eg = jnp.exp(gc)
        kb = kb_ref[rows_c, :]
        k_b = k_ref[rows_c, :]
        q_b = q_ref[rows_c, :]
        k_rows = jnp.concatenate([k_b * head_sel_w[h] for h in range(N_HEADS)], axis=0)
        qk = _dot_nt(jnp.concatenate([kb, q_b], axis=0), k_rows)
        yield
        l_s[ci] = jnp.where(strict, qk[:c] * decay, 0.0).astype(BF16)
        at_s[ci] = jnp.where(tri, qk[c:] * decay, 0.0).astype(BF16)
        kbe = kb.astype(F32) * eg
        qg = (q_b.astype(F32) * eg).astype(BF16)
        gl = gc[c - 1:c, :]
        kd = (k_b.astype(F32) * jnp.exp(gl - gc)).astype(BF16)
        vb = vb_ref[rows_c, :]
        for h in range(N_HEADS):
            hs = slice(h * HEAD_DIM, (h + 1) * HEAD_DIM)
            rhs_s[ci, h, :, 0:HEAD_DIM] = vb[:, hs]
            rhs_s[ci, h, :, HEAD_DIM:2 * HEAD_DIM] = kbe[:, hs].astype(BF16)
            wq_s[ci, h, c:2 * c, :] = qg[:, hs]
            kd_s[ci, h] = kd[:, hs]
        egl_s[ci] = jnp.broadcast_to(jnp.exp(gl), (SUBLANES, GROUP_W))

    def invert(ci):
        lf = l_s[ci].astype(F32)
        n8 = jnp.where(diag8, -lf, 0.0)
        t = eye_f + n8
        p = n8.astype(BF16)
        p = _dot(p, block_diag(p)).astype(BF16)
        yield
        pt = _dot(jnp.concatenate([p, t.astype(BF16)], axis=0), block_diag(p))
        t = t + pt[c:]
        p = pt[:c].astype(BF16)
        yield
        t = t + _dot(t.astype(BF16), block_diag(p))
        yield
        for mask in merge_masks:
            off = jnp.where(mask, lf, 0.0).astype(BF16)
            a = _dot(t.astype(BF16), block_diag(off)).astype(BF16)
            yield
            t = t - _dot(a, block_diag(t.astype(BF16)))
            yield
        tb = t.astype(BF16)
        for h in range(N_HEADS):
            pr = slice((h // 2) * LANES, (h // 2 + 1) * LANES)
            x = _dot(tb[:, pr], pad_rows(rhs_s[ci, h], h % 2))
            u_s[ci, h] = x[:, :HEAD_DIM]
            wq_s[ci, h, 0:c, :] = x[:, HEAD_DIM:].astype(BF16)
        yield

    def recur(cis):
        for ci in cis:
            r0 = ci * c
            at = at_s[ci]
            egl = egl_s[ci]
            sts = [st_ref[h] for h in range(N_HEADS)]
            wss = [_dot(wq_s[ci, h], sts[h].astype(BF16)) for h in range(N_HEADS)]
            yield
            vnbs = [(u_s[ci, h] - wss[h][:c]).astype(BF16) for h in range(N_HEADS)]
            for h in range(N_HEADS):
                hs = slice(h * HEAD_DIM, (h + 1) * HEAD_DIM)
                pr = slice((h // 2) * LANES, (h // 2 + 1) * LANES)
                out = wss[h][c:] + _dot(at[:, pr], pad_rows(vnbs[h], h % 2))
                st_ref[h] = sts[h] * egl[0:1, hs] + _dot_tn(kd_s[ci, h], vnbs[h])
                zh = z_ref[pl.ds(r0, c), hs].astype(F32)
                o_ref[pl.ds(r0, c), hs] = (_rms(out, ng) * (zh * _sigmoid(zh))).astype(BF16)
            yield

    return prepare, invert, recur


def _hgrn_stages(q_ref, f_ref, i_ref, z_ref, lbl_ref, ng_ref, o_ref, st_ref, at_s, qe_s, kd_s, dec_s, layer):
    c = CHUNK
    logits = lbl_ref[...]
    e = jnp.exp(logits - jnp.max(logits, axis=0, keepdims=True))
    p = e / jnp.sum(e, axis=0, keepdims=True)
    lb = jnp.sum(p[0:layer + 1], axis=0, keepdims=True) - p[0:1]
    ng = ng_ref[...]

    tt = lax.broadcasted_iota(jnp.int32, (c, c), 0)
    ss = lax.broadcasted_iota(jnp.int32, (c, c), 1)
    trow = lax.broadcasted_iota(jnp.int32, (c, HEAD_DIM), 0)
    levels = (1, 2, 4, 8, 16, 32)
    pair_masks = [((tt & m) != 0) & ((ss & m) == 0) & ((tt // (2 * m)) == (ss // (2 * m))) for m in levels]
    odd_rows = [(trow & m) != 0 for m in levels]
    eye = tt == ss

    def local(ci, h):
        r0 = ci * c
        slot = ci % 2
        hs = slice(h * HEAD_DIM, (h + 1) * HEAD_DIM)
        fr = f_ref[pl.ds(r0, c), hs].astype(F32)
        rq = q_ref[pl.ds(r0, c), hs].astype(F32)
        lbh = lb[:, hs]
        lf = jnp.log2(lbh + (1.0 - lbh) * _sigmoid(fr))
        kk = (1.0 - lbh) * _sigmoid(-fr)
        qq = rq * _sigmoid(rq)
        attn = jnp.where(eye, _dot_nt(qq.astype(BF16), kk.astype(BF16)), 0.0)
        yield
        cs = lf
        tot = lf
        for m, pmask, odd in zip(levels, pair_masks, odd_rows):
            qm = (qq * jnp.exp2(cs)).astype(BF16)
            km = (kk * jnp.exp2(tot - cs)).astype(BF16)
            attn = jnp.where(pmask, _dot_nt(qm, km), attn)
            if m < SUBLANES:
                prev = pltpu.roll(tot, m, axis=0)
                nxt = pltpu.roll(tot, c - m, axis=0)
                cs = cs + jnp.where(odd, prev, 0.0)
                tot = tot + jnp.where(odd, prev, nxt)
            else:
                cs_parts, tot_parts = [], []
                for b0 in range(0, c, 2 * m):
                    t_even = tot[b0:b0 + m]
                    t_both = t_even + tot[b0 + m:b0 + 2 * m]
                    cs_parts += [cs[b0:b0 + m], cs[b0 + m:b0 + 2 * m] + t_even]
                    tot_parts += [t_both, t_both]
                cs = jnp.concatenate(cs_parts, axis=0)
                tot = jnp.concatenate(tot_parts, axis=0)
            yield
        at_s[slot, h] = attn.astype(BF16)
        qe_s[slot, h] = (qq * jnp.exp2(cs)).astype(BF16)
        kd_s[slot, h] = (kk * jnp.exp2(tot - cs)).astype(BF16)
        dec_s[slot, h] = jnp.exp2(tot[0:SUBLANES, :])

    def tail(ci, h):
        r0 = ci * c
        slot = ci % 2
        hs = slice(h * HEAD_DIM, (h + 1) * HEAD_DIM)
        ii = i_ref[pl.ds(r0, c), hs]
        st = st_ref[h]
        out = _dot_nt(qe_s[slot, h], st.astype(BF16)) + _dot(at_s[slot, h], ii)
        st_ref[h] = st * dec_s[slot, h, 0:1, :] + _dot_tn(ii, kd_s[slot, h])
        yield
        zh = z_ref[pl.ds(r0, c), hs].astype(F32)
        o_ref[pl.ds(r0, c), hs] = (_rms(out, ng) * _sigmoid(zh)).astype(BF16)

    return local, tail


def _round_robin(gens):
    gens = list(gens)
    while gens:
        alive = []
        for g in gens:
            try:
                next(g)
                alive.append(g)
            except StopIteration:
                pass
        gens = alive
        yield


def _chain(gens):
    for g in gens:
        yield from g


def _recurrent_kernel(gq_ref, gk_ref, gkb_ref, gvb_ref, la128_ref, la64_ref, gz_ref, gng_ref,
                      rq_ref, rf_ref, ri_ref, rz_ref, lbl_ref, hng_ref, og_ref, oh_ref,
                      gst_ref, l_s, at_s, rhs_s, wq_s, u_s, kd_s, egl_s,
                      hst_ref, hat_s, hqe_s, hkd_s, hdec_s, *, layer):
    @pl.when(pl.program_id(1) == 0)
    def _():
        gst_ref[...] = jnp.zeros(gst_ref.shape, F32)
        hst_ref[...] = jnp.zeros(hst_ref.shape, F32)

    prepare, invert, recur = _gdn_stages(gq_ref, gk_ref, gkb_ref, gvb_ref, la128_ref, la64_ref, gz_ref, gng_ref,
                                         og_ref, gst_ref, l_s, at_s, rhs_s, wq_s, u_s, kd_s, egl_s)
    local, tail = _hgrn_stages(rq_ref, rf_ref, ri_ref, rz_ref, lbl_ref, hng_ref, oh_ref,
                               hst_ref, hat_s, hqe_s, hkd_s, hdec_s, layer)
    n = gq_ref.shape[0] // CHUNK
    heads = range(N_HEADS)

    def hgrn_step(ci):
        gens = [tail(ci - 1, h) for h in heads] if ci > 0 else []
        gens += [local(ci, h) for h in heads] if ci < n else []
        return _round_robin(gens)

    per_phase = [0] + [2] * (n // 2) + [1]
    first = [sum(per_phase[:k]) for k in range(len(per_phase))]

    for k in range(n // 2 + 2):
        gens = []
        if k < n // 2:
            gens += [prepare(2 * k), prepare(2 * k + 1)]
        if 1 <= k <= n // 2:
            gens += [invert(2 * k - 2), invert(2 * k - 1)]
        if 2 <= k:
            gens += [recur([2 * k - 4, 2 * k - 3])]
        gens += [_chain([hgrn_step(first[k] + i) for i in range(per_phase[k])])]
        for _ in _round_robin(gens):
            pass


def _recurrent(gdn_in, proj, gng, lbl, hng, layer, b, s):
    ts = _seq_tile(s)
    ns = s // ts
    nc = ts // CHUNK
    assert nc % 2 == 0

    def rows(wd):
        return pl.BlockSpec((ts, wd), lambda bi, si: (bi * ns + si, 0))

    def seg(k):
        return pl.BlockSpec((None, ts, SEG_W), lambda bi, si: (k, bi * ns + si, 0))

    head_vec = pl.BlockSpec((1, HEAD_DIM), lambda bi, si: (0, 0))
    n = b * s
    return pl.pallas_call(
        functools.partial(_recurrent_kernel, layer=layer),
        grid=(b, ns),
        in_specs=[rows(GROUP_W)] * 5 + [rows(N_HEADS * CHUNK), seg(SEG_G_Z), head_vec,
                  seg(SEG_R_Q), seg(SEG_R_F), seg(SEG_R_I), seg(SEG_R_Z),
                  pl.BlockSpec(lbl.shape, lambda bi, si: (0, 0)), head_vec],
        out_specs=[rows(GROUP_W), rows(GROUP_W)],
        out_shape=[jax.ShapeDtypeStruct((n, GROUP_W), BF16)] * 2,
        scratch_shapes=[pltpu.VMEM((N_HEADS, HEAD_DIM, HEAD_DIM), F32),
                        pltpu.VMEM((nc, CHUNK, N_HEADS * CHUNK), BF16),
                        pltpu.VMEM((nc, CHUNK, N_HEADS * CHUNK), BF16),
                        pltpu.VMEM((nc, N_HEADS, CHUNK, 2 * HEAD_DIM), BF16),
                        pltpu.VMEM((nc, N_HEADS, 2 * CHUNK, HEAD_DIM), BF16),
                        pltpu.VMEM((nc, N_HEADS, CHUNK, HEAD_DIM), F32),
                        pltpu.VMEM((nc, N_HEADS, CHUNK, HEAD_DIM), BF16),
                        pltpu.VMEM((nc, SUBLANES, GROUP_W), F32),
                        pltpu.VMEM((N_HEADS, HEAD_DIM, HEAD_DIM), F32),
                        pltpu.VMEM((2, N_HEADS, CHUNK, CHUNK), BF16),
                        pltpu.VMEM((2, N_HEADS, CHUNK, HEAD_DIM), BF16),
                        pltpu.VMEM((2, N_HEADS, CHUNK, HEAD_DIM), BF16),
                        pltpu.VMEM((2, N_HEADS, SUBLANES, HEAD_DIM), F32)],
        compiler_params=_params(("parallel", "arbitrary")),
        name="recurrent",
    )(*gdn_in, proj, gng, proj, proj, proj, proj, lbl, hng)


def _mix_mlp_kernel(ya_ref, yb_ref, yc_ref, yd_ref, wo_ref, h_ref, g_ref, w1_ref, w2_ref, fg_ref,
                    o_ref, xn_ref, *, final_norm):
    f = pl.program_id(1)

    @pl.when(f == 0)
    def _():
        y = jnp.concatenate([ya_ref[...], yb_ref[...], yc_ref[...], yd_ref[...]], axis=1)
        h1 = h_ref[...] + _dot(y, wo_ref[...])
        xn_ref[...] = _rms(h1, g_ref[...]).astype(BF16)
        o_ref[...] = h1

    a = jnp.maximum(_dot(xn_ref[...], w1_ref[...]), 0.0)
    o_ref[...] += _dot((a * a).astype(BF16), w2_ref[...])

    if final_norm:
        @pl.when(f == pl.num_programs(1) - 1)
        def _():
            o_ref[...] = _rms(o_ref[...], fg_ref[...])


def _mix_mlp(ys, wo, h, g, w1, w2, fg, layer, final_norm):
    n, d = h.shape
    dff = w1.shape[2]
    tm = min(512, n)
    tf = 1024
    ysp = pl.BlockSpec((tm, GROUP_W), lambda i, f: (i, 0))
    rows = pl.BlockSpec((tm, d), lambda i, f: (i, 0))
    vec = pl.BlockSpec((1, d), lambda i, f: (0, 0))
    return pl.pallas_call(
        functools.partial(_mix_mlp_kernel, final_norm=final_norm),
        grid=(n // tm, dff // tf),
        in_specs=[ysp, ysp, ysp, ysp,
                  pl.BlockSpec((None, d, d), lambda i, f: (layer, 0, 0), pipeline_mode=pl.Buffered(1)),
                  rows, vec,
                  pl.BlockSpec((None, d, tf), lambda i, f: (layer, 0, f)),
                  pl.BlockSpec((None, tf, d), lambda i, f: (layer, f, 0)),
                  vec],
        out_specs=rows,
        out_shape=jax.ShapeDtypeStruct((n, d), F32),
        scratch_shapes=[pltpu.VMEM((tm, d), BF16)],
        compiler_params=_params(("parallel", "arbitrary")),
        name="mix_mlp",
    )(*ys, wo, h, g, w1, w2, fg)


def _pack_w_in(w_in):
    wt = jnp.swapaxes(w_in, 1, 2)
    widths = (GROUP_W, GROUP_W, GROUP_W, MLA_Q_RANK, MLA_KV_RANK, MLA_ROPE,
              GROUP_W, GROUP_W, GROUP_W, GROUP_W, N_HEADS, N_HEADS,
              GROUP_W, GROUP_W, GROUP_W, GROUP_W)
    rows, start = [], 0
    for wd in widths:
        rows.append(wt[:, start:start + wd])
        start += wd
    (sc_x, sc_c, sc_b, m_cq, m_ckv, m_kr, g_q, g_k, g_v, g_z, g_a, g_b, r_q, r_f, r_i, r_z) = rows
    l, _, d = wt.shape

    def zeros(n):
        return jnp.zeros((l, n, d), wt.dtype)

    half = MLA_ROPE // 2
    kr_swapped = jnp.concatenate([m_kr[:, half:], m_kr[:, :half]], axis=1)
    seg_q = [m_cq, g_a, g_b, zeros(SEG_W - MLA_Q_RANK - 2 * N_HEADS)]
    seg_kv = [m_ckv, m_kr, zeros(LANES - MLA_ROPE), kr_swapped, zeros(LANES - MLA_ROPE)]
    segs = [sc_x, sc_c, sc_b, g_q, g_k, g_v, g_z, r_q, r_f, r_i, r_z] + seg_q + seg_kv
    return jnp.concatenate(segs, axis=1).astype(BF16)


def _pack_mla_weights(w_uq, w_ukv):
    l = w_uq.shape[0]
    half = MLA_ROPE // 2
    wq = w_uq.reshape(l, MLA_Q_RANK, N_HEADS, MLA_NOPE + MLA_ROPE)
    nope, rp = wq[..., :MLA_NOPE], wq[..., MLA_NOPE:]
    z64 = jnp.zeros(rp.shape, rp.dtype)
    wqa = jnp.concatenate([nope, rp, z64], axis=-1).reshape(l, MLA_Q_RANK, N_HEADS * QK_PAD)
    rp_sw = jnp.concatenate([rp[..., half:], rp[..., :half]], axis=-1)
    wqb = jnp.concatenate([rp_sw, z64], axis=-1).reshape(l, MLA_Q_RANK, N_HEADS * LANES)
    wkv = w_ukv.reshape(l, MLA_KV_RANK, N_HEADS, MLA_NOPE + HEAD_DIM)
    wk = wkv[..., :MLA_NOPE].reshape(l, MLA_KV_RANK, N_HEADS * MLA_NOPE)
    wv = wkv[..., MLA_NOPE:].reshape(l, MLA_KV_RANK, N_HEADS * HEAD_DIM)
    return wqa.astype(BF16), wqb.astype(BF16), wk.astype(BF16), jnp.swapaxes(wv, 1, 2).astype(BF16)


def _pad_lanes(x):
    l, k = x.shape
    return jnp.concatenate([x, jnp.zeros((l, LANES - k), x.dtype)], axis=-1).reshape(l, 1, LANES)


def kernel(x, positions, norm1_g, w_in, sconv_w, sconv_out_g, mla_q_g, mla_kv_g, mla_w_uq, mla_w_ukv,
           mla_out_g, gdn_conv_w, gdn_a_log, gdn_dt_bias, gdn_norm_g, hgrn_lb_logits, hgrn_norm_g,
           w_o, norm2_g, w_ff1, w_ff2, final_g):
    b, s, d = x.shape
    n = b * s
    depth = w_in.shape[0]

    w_in_p = _pack_w_in(w_in)
    wqa, wqb, wk, wvt = _pack_mla_weights(mla_w_uq, mla_w_ukv)
    w_o_b = w_o.astype(BF16)
    w1_b = w_ff1.astype(BF16)
    w2_b = w_ff2.astype(BF16)
    alog_p = _pad_lanes(gdn_a_log)
    dtb_p = _pad_lanes(gdn_dt_bias)

    half = MLA_ROPE // 2
    inv_freq = ROPE_THETA ** (-jnp.arange(half, dtype=F32) / half)
    invf = jnp.concatenate([inv_freq, inv_freq, jnp.zeros((LANES - MLA_ROPE,), F32)]).reshape(1, LANES)
    cos, sin = _rope_tables(positions.astype(F32).reshape(n, 1), invf)

    h = x.reshape(n, d)
    for l in range(depth):
        proj = _in_proj(h, norm1_g[l].reshape(1, d), w_in_p, l)
        y_sc = _sconv(proj, sconv_w[l], sconv_out_g[l].reshape(1, GROUP_W), b, s)
        q, k, vt = _mla_up(proj, cos, sin, mla_q_g[l].reshape(1, -1), mla_kv_g[l].reshape(1, -1),
                           wqa[l], wqb[l], wk[l], wvt[l], b, s)
        y_mla = _attention(q, k, vt, mla_out_g[l].reshape(1, GROUP_W), b, s)
        gdn_in = _gdn_prep(proj, gdn_conv_w[l], alog_p[l], dtb_p[l], b, s)
        y_gdn, y_hg = _recurrent(gdn_in, proj, gdn_norm_g[l].reshape(1, HEAD_DIM), hgrn_lb_logits,
                                 hgrn_norm_g[l].reshape(1, HEAD_DIM), l, b, s)
        h = _mix_mlp((y_sc, y_mla, y_gdn, y_hg), w_o_b, h, norm2_g[l].reshape(1, d), w1_b, w2_b,
                     final_g.reshape(1, d), l, final_norm=(l == depth - 1))
    return h.reshape(b, s, d)
```

```python
import functools

import jax
import jax.numpy as jnp
from jax import lax
from jax.experimental import pallas as pl
from jax.experimental.pallas import tpu as pltpu

F32 = jnp.float32
BF16 = jnp.bfloat16

GROUP_W = 512
HEAD_DIM = 128
N_HEADS = 4
MLA_Q_RANK = 384
MLA_KV_RANK = 256
MLA_NOPE = 128
MLA_ROPE = 64
ROPE_THETA = 10000.0
CHUNK = 64
EPS = 1e-6
LOG2_E = 1.4426950408889634

SEG_W = 512
(SEG_SC_X, SEG_SC_C, SEG_SC_B, SEG_G_Q, SEG_G_K, SEG_G_V, SEG_G_Z,
 SEG_R_Q, SEG_R_F, SEG_R_I, SEG_R_Z, SEG_MLA_Q, SEG_MLA_KV) = range(13)
N_SEG = 13
QK_PAD = 256
LANES = 128
SUBLANES = 8
VMEM_LIMIT = 56 * 1024 * 1024


def _params(sem, vmem=VMEM_LIMIT):
    return pltpu.CompilerParams(dimension_semantics=sem, vmem_limit_bytes=vmem)


def _dot(a, b):
    return jnp.dot(a, b, preferred_element_type=F32)


def _dot_nt(a, b):
    return lax.dot_general(a, b, (((1,), (1,)), ((), ())), preferred_element_type=F32)


def _dot_tn(a, b):
    return lax.dot_general(a, b, (((0,), (0,)), ((), ())), preferred_element_type=F32)


def _rms(x, g):
    return x * lax.rsqrt(jnp.mean(x * x, axis=-1, keepdims=True) + EPS) * g


def _sigmoid(x):
    return jax.nn.sigmoid(x)


def _inproj_kernel(x_ref, g_ref, w_ref, o_ref, xn_ref):
    @pl.when(pl.program_id(1) == 0)
    def _():
        xn_ref[...] = _rms(x_ref[...], g_ref[...]).astype(BF16)

    o_ref[...] = _dot_nt(xn_ref[...], w_ref[...]).astype(o_ref.dtype)


def _in_proj(h, g, w, layer):
    n, d = h.shape
    tm = min(1024, n)
    return pl.pallas_call(
        _inproj_kernel,
        grid=(n // tm, N_SEG),
        in_specs=[pl.BlockSpec((tm, d), lambda i, j: (i, 0)),
                  pl.BlockSpec((1, d), lambda i, j: (0, 0)),
                  pl.BlockSpec((None, SEG_W, d), lambda i, j: (layer, j, 0))],
        out_specs=pl.BlockSpec((None, tm, SEG_W), lambda i, j: (j, i, 0)),
        out_shape=jax.ShapeDtypeStruct((N_SEG, n, SEG_W), BF16),
        scratch_shapes=[pltpu.VMEM((tm, d), BF16)],
        compiler_params=_params(("parallel", "arbitrary")),
        name="in_proj",
    )(h, g, w)


def _sconv_kernel(x_ref, c_ref, b_ref, w_ref, g_ref, o_ref, zbuf):
    s = pl.program_id(1)
    ts = x_ref.shape[0]

    @pl.when(s == 0)
    def _():
        zbuf[0:SUBLANES, :] = jnp.zeros((SUBLANES, GROUP_W), F32)

    @pl.when(s > 0)
    def _():
        zbuf[0:SUBLANES, :] = zbuf[ts:ts + SUBLANES, :]

    zbuf[SUBLANES:ts + SUBLANES, :] = c_ref[...].astype(F32) * x_ref[...].astype(F32)
    w = w_ref[...]
    y = (w[2:3] * zbuf[SUBLANES:ts + SUBLANES, :]
         + w[1:2] * zbuf[SUBLANES - 1:ts + SUBLANES - 1, :]
         + w[0:1] * zbuf[SUBLANES - 2:ts + SUBLANES - 2, :])
    o_ref[...] = _rms(b_ref[...].astype(F32) * y, g_ref[...]).astype(BF16)


def _seq_tile(s):
    return min(512, s)


def _sconv(proj, w, g, b, s):
    ts = _seq_tile(s)
    ns = s // ts

    def seg(k):
        return pl.BlockSpec((None, ts, SEG_W), lambda bi, si: (k, bi * ns + si, 0))

    return pl.pallas_call(
        _sconv_kernel,
        grid=(b, ns),
        in_specs=[seg(SEG_SC_X), seg(SEG_SC_C), seg(SEG_SC_B),
                  pl.BlockSpec((3, GROUP_W), lambda bi, si: (0, 0)),
                  pl.BlockSpec((1, GROUP_W), lambda bi, si: (0, 0))],
        out_specs=pl.BlockSpec((ts, GROUP_W), lambda bi, si: (bi * ns + si, 0)),
        out_shape=jax.ShapeDtypeStruct((b * s, GROUP_W), BF16),
        scratch_shapes=[pltpu.VMEM((ts + SUBLANES, GROUP_W), F32)],
        compiler_params=_params(("parallel", "arbitrary")),
        name="sconv",
    )(proj, proj, proj, w, g)


def _rope_kernel(pos_ref, invf_ref, cos_ref, sin_ref):
    ang = pos_ref[...] * invf_ref[...]
    lane = lax.broadcasted_iota(jnp.int32, ang.shape, 1)
    half = MLA_ROPE // 2
    cos_ref[...] = jnp.where(lane < MLA_ROPE, jnp.cos(ang), 0.0)
    sn = jnp.sin(ang)
    sin_ref[...] = jnp.where(lane < half, -sn, jnp.where(lane < MLA_ROPE, sn, 0.0))


def _rope_tables(pos_col, invf):
    n = pos_col.shape[0]
    tm = min(2048, n)
    return pl.pallas_call(
        _rope_kernel,
        grid=(n // tm,),
        in_specs=[pl.BlockSpec((tm, 1), lambda i: (i, 0)),
                  pl.BlockSpec((1, LANES), lambda i: (0, 0))],
        out_specs=[pl.BlockSpec((tm, LANES), lambda i: (i, 0))] * 2,
        out_shape=[jax.ShapeDtypeStruct((n, LANES), F32)] * 2,
        compiler_params=_params(("parallel",)),
        name="rope_tables",
    )(pos_col, invf)


def _mla_up_kernel(a_ref, b_ref, cos_ref, sin_ref, gq_ref, gkv_ref,
                   wqa_ref, wqb_ref, wk_ref, wvt_ref, q_ref, k_ref, vt_ref):
    scale = (MLA_NOPE + MLA_ROPE) ** -0.5 * LOG2_E
    cos = cos_ref[...]
    sin = sin_ref[...]
    a = a_ref[...].astype(F32)
    cqn = _rms(a[:, :MLA_Q_RANK], gq_ref[...]).astype(BF16)
    qa = _dot(cqn, wqa_ref[...])
    qb = _dot(cqn, wqb_ref[...])
    for h in range(N_HEADS):
        o = h * QK_PAD
        q_ref[:, o:o + LANES] = (qa[:, o:o + LANES] * scale).astype(BF16)
        roped = qa[:, o + LANES:o + QK_PAD] * cos + qb[:, h * LANES:(h + 1) * LANES] * sin
        q_ref[:, o + LANES:o + QK_PAD] = (roped * scale).astype(BF16)
    bb = b_ref[...].astype(F32)
    ckvn = _rms(bb[:, :MLA_KV_RANK], gkv_ref[...]).astype(BF16)
    kk = _dot(ckvn, wk_ref[...])
    vt_ref[...] = _dot_nt(wvt_ref[...], ckvn).astype(BF16)
    kr = (bb[:, MLA_KV_RANK:MLA_KV_RANK + LANES] * cos
          + bb[:, MLA_KV_RANK + LANES:MLA_KV_RANK + 2 * LANES] * sin).astype(BF16)
    for h in range(N_HEADS):
        o = h * QK_PAD
        k_ref[:, o:o + LANES] = kk[:, h * LANES:(h + 1) * LANES].astype(BF16)
        k_ref[:, o + LANES:o + QK_PAD] = kr


def _mla_up(proj, cos, sin, gq, gkv, wqa, wqb, wk, wvt, b, s):
    n = proj.shape[1]
    tm = min(512, s)
    ns = s // tm

    def seg(k):
        return pl.BlockSpec((None, tm, SEG_W), lambda i: (k, i, 0))

    def full(x):
        return pl.BlockSpec(x.shape, lambda i: (0,) * x.ndim)

    def rows(w):
        return pl.BlockSpec((tm, w), lambda i: (i, 0))

    return pl.pallas_call(
        _mla_up_kernel,
        grid=(n // tm,),
        in_specs=[seg(SEG_MLA_Q), seg(SEG_MLA_KV), rows(LANES), rows(LANES),
                  full(gq), full(gkv), full(wqa), full(wqb), full(wk), full(wvt)],
        out_specs=[rows(N_HEADS * QK_PAD), rows(N_HEADS * QK_PAD),
                   pl.BlockSpec((None, GROUP_W, tm), lambda i: (i // ns, 0, i % ns))],
        out_shape=[jax.ShapeDtypeStruct((n, N_HEADS * QK_PAD), BF16),
                   jax.ShapeDtypeStruct((n, N_HEADS * QK_PAD), BF16),
                   jax.ShapeDtypeStruct((b, GROUP_W, s), BF16)],
        compiler_params=_params(("parallel",)),
        name="mla_up",
    )(proj, proj, cos, sin, gq, gkv, wqa, wqb, wk, wvt)


def _attn_kernel(q_ref, k_ref, vt_ref, g_ref, o_ref, acc_ref):
    i = pl.program_id(1)
    tq = q_ref.shape[0]
    kv_i = lax.broadcasted_iota(jnp.int32, (tq, tq), 0)
    q_i = lax.broadcasted_iota(jnp.int32, (tq, tq), 1)
    causal = kv_i <= q_i
    heads = range(N_HEADS)

    def block(j, carry, masked):
        start = pl.multiple_of(j * tq, tq)
        sc = [_dot_nt(k_ref[pl.ds(start, tq), h * QK_PAD:(h + 1) * QK_PAD],
                      q_ref[:, h * QK_PAD:(h + 1) * QK_PAD]) for h in heads]
        if masked:
            sc = [jnp.where(causal, x, -jnp.inf) for x in sc]
        m_old = carry[:N_HEADS]
        l_old = carry[N_HEADS:]
        m_new = [jnp.maximum(m, jnp.max(x, axis=0, keepdims=True)) for m, x in zip(m_old, sc)]
        alpha = [jnp.exp2(mo - mn) for mo, mn in zip(m_old, m_new)]
        p = [jnp.exp2(x - mn) for x, mn in zip(sc, m_new)]
        l_new = [a * l + jnp.sum(x, axis=0, keepdims=True) for a, l, x in zip(alpha, l_old, p)]
        pv = [_dot(vt_ref[h * HEAD_DIM:(h + 1) * HEAD_DIM, pl.ds(start, tq)], x.astype(BF16))
              for h, x in zip(heads, p)]
        for h in heads:
            acc_ref[h] = alpha[h] * acc_ref[h] + pv[h]
        return tuple(m_new) + tuple(l_new)

    acc_ref[...] = jnp.zeros(acc_ref.shape, F32)
    init = (jnp.full((1, tq), -jnp.inf, F32),) * N_HEADS + (jnp.zeros((1, tq), F32),) * N_HEADS
    carry = lax.fori_loop(0, i, functools.partial(block, masked=False), init)
    carry = block(i, carry, True)
    ot = jnp.concatenate([acc_ref[h] / carry[N_HEADS + h] for h in heads], axis=0)
    ot = ot * lax.rsqrt(jnp.mean(ot * ot, axis=0, keepdims=True) + EPS)
    o_ref[...] = (ot.T * g_ref[...]).astype(BF16)


def _attention(q, k, vt, g, b, s):
    tq = min(512, s)
    nq = s // tq
    q3 = q.reshape(b, s, N_HEADS * QK_PAD)
    k3 = k.reshape(b, s, N_HEADS * QK_PAD)
    out = pl.pallas_call(
        _attn_kernel,
        grid=(b, nq),
        in_specs=[pl.BlockSpec((None, tq, N_HEADS * QK_PAD), lambda bi, i: (bi, i, 0)),
                  pl.BlockSpec((None, s, N_HEADS * QK_PAD), lambda bi, i: (bi, 0, 0)),
                  pl.BlockSpec((None, GROUP_W, s), lambda bi, i: (bi, 0, 0)),
                  pl.BlockSpec((1, GROUP_W), lambda bi, i: (0, 0))],
        out_specs=pl.BlockSpec((None, tq, GROUP_W), lambda bi, i: (bi, i, 0)),
        out_shape=jax.ShapeDtypeStruct((b, s, GROUP_W), BF16),
        scratch_shapes=[pltpu.VMEM((N_HEADS, HEAD_DIM, tq), F32)],
        compiler_params=_params(("parallel", "arbitrary")),
        name="mla_attention",
    )(q3, k3, vt, g)
    return out.reshape(b * s, GROUP_W)


GDN_CONV_K = 4


def _gdn_prep_kernel(q_ref, k_ref, v_ref, a_ref, w_ref, alog_ref, dtb_ref,
                     qo_ref, ko_ref, kbo_ref, vbo_ref, la128_ref, la64_ref, buf):
    s = pl.program_id(1)
    ts = q_ref.shape[0]
    blk = LANES
    w = w_ref[...]
    lane = lax.broadcasted_iota(jnp.int32, (blk, LANES), 1)
    for idx, src in enumerate((q_ref, k_ref, v_ref)):
        @pl.when(s == 0)
        def _(idx=idx):
            buf[idx, 0:SUBLANES, :] = jnp.zeros((SUBLANES, GROUP_W), F32)

        @pl.when(s > 0)
        def _(idx=idx):
            buf[idx, 0:SUBLANES, :] = buf[idx, ts:ts + SUBLANES, :]

        buf[idx, SUBLANES:ts + SUBLANES, :] = src[...].astype(F32)

    for b0 in range(0, ts, blk):
        rows_b = slice(b0, b0 + blk)
        gx = a_ref[rows_b, MLA_Q_RANK:MLA_Q_RANK + LANES].astype(F32)
        xs = gx + dtb_ref[...]
        softplus = jnp.maximum(xs, 0.0) + jnp.log1p(jnp.exp(-jnp.abs(xs)))
        log_a = -jnp.exp(alog_ref[...]) * softplus
        beta = _sigmoid(gx)
        la_b = [jnp.broadcast_to(log_a[:, h:h + 1], (blk, LANES)) for h in range(N_HEADS)]
        for h in range(N_HEADS):
            la128_ref[rows_b, h * HEAD_DIM:(h + 1) * HEAD_DIM] = la_b[h]
        for pr in range(N_HEADS // 2):
            la64_ref[rows_b, pr * LANES:(pr + 1) * LANES] = jnp.where(lane < CHUNK, la_b[2 * pr], la_b[2 * pr + 1])
        beta_c = [beta[:, N_HEADS + h:N_HEADS + h + 1] for h in range(N_HEADS)]
        for idx in range(3):
            wi = w[:, idx * GROUP_W:(idx + 1) * GROUP_W]
            y = wi[3:4] * buf[idx, SUBLANES + b0:SUBLANES + b0 + blk, :]
            for j in range(1, GDN_CONV_K):
                y = y + wi[3 - j:4 - j] * buf[idx, SUBLANES + b0 - j:SUBLANES + b0 + blk - j, :]
            y = y * _sigmoid(y)
            for h in range(N_HEADS):
                hs = slice(h * HEAD_DIM, (h + 1) * HEAD_DIM)
                yh = y[:, hs]
                if idx < 2:
                    yh = yh * lax.rsqrt(jnp.sum(yh * yh, axis=-1, keepdims=True) + EPS)
                if idx == 0:
                    qo_ref[rows_b, hs] = (yh * (HEAD_DIM ** -0.5)).astype(BF16)
                elif idx == 1:
                    ko_ref[rows_b, hs] = yh.astype(BF16)
                    kbo_ref[rows_b, hs] = (yh * beta_c[h]).astype(BF16)
                else:
                    vbo_ref[rows_b, hs] = (yh * beta_c[h]).astype(BF16)


def _gdn_prep(proj, w, alog, dtb, b, s):
    ts = _seq_tile(s)
    ns = s // ts

    def seg(k):
        return pl.BlockSpec((None, ts, SEG_W), lambda bi, si: (k, bi * ns + si, 0))

    def rows(wd):
        return pl.BlockSpec((ts, wd), lambda bi, si: (bi * ns + si, 0))

    def full(x):
        return pl.BlockSpec(x.shape, lambda bi, si: (0,) * x.ndim)

    n = b * s
    return pl.pallas_call(
        _gdn_prep_kernel,
        grid=(b, ns),
        in_specs=[seg(SEG_G_Q), seg(SEG_G_K), seg(SEG_G_V), seg(SEG_MLA_Q),
                  full(w), full(alog), full(dtb)],
        out_specs=[rows(GROUP_W)] * 5 + [rows(N_HEADS * CHUNK)],
        out_shape=[jax.ShapeDtypeStruct((n, GROUP_W), BF16)] * 4 + [jax.ShapeDtypeStruct((n, GROUP_W), F32)]
        + [jax.ShapeDtypeStruct((n, N_HEADS * CHUNK), F32)],
        scratch_shapes=[pltpu.VMEM((3, ts + SUBLANES, GROUP_W), F32)],
        compiler_params=_params(("parallel", "arbitrary")),
        name="gdn_prep",
    )(proj, proj, proj, proj, w, alog, dtb)


def _split3(x):
    hi = x.astype(BF16)
    r1 = x - hi.astype(F32)
    mid = r1.astype(BF16)
    lo = (r1 - mid.astype(F32)).astype(BF16)
    return jnp.concatenate([hi, mid, lo], axis=0)


def _gdn_stages(q_ref, k_ref, kb_ref, vb_ref, la128_ref, la64_ref, z_ref, ng_ref, o_ref,
                st_ref, l_s, at_s, rhs_s, wq_s, u_s, kd_s, egl_s):
    c = CHUNK
    pw = N_HEADS * c
    row = lax.broadcasted_iota(jnp.int32, (c, pw), 0)
    col = lax.broadcasted_iota(jnp.int32, (c, pw), 1) % c
    tri = col <= row
    strict = col < row
    eye_f = (col == row).astype(F32)
    diag8 = (col // SUBLANES) == (row // SUBLANES)
    merge_levels = (8, 16, 32)
    merge_masks = [((row & m) != 0) & ((col & m) == 0) & ((row // (2 * m)) == (col // (2 * m)))
                   for m in merge_levels]
    lane_head_p = lax.broadcasted_iota(jnp.int32, (c, pw), 1) // c
    head_sel_p = [(lane_head_p == h).astype(BF16) for h in range(N_HEADS)]
    lane_head_w = lax.broadcasted_iota(jnp.int32, (c, GROUP_W), 1) // HEAD_DIM
    head_sel_w = [(lane_head_w == h).astype(BF16) for h in range(N_HEADS)]
    row3 = lax.broadcasted_iota(jnp.int32, (c, 3 * c), 0)
    col3 = lax.broadcasted_iota(jnp.int32, (c, 3 * c), 1)
    tri3 = ((col3 % c) <= row3).astype(BF16)
    zeros_half = jnp.zeros((c, 2 * HEAD_DIM), BF16)
    ng = ng_ref[...]

    def block_diag(x_b):
        return jnp.concatenate([x_b * head_sel_p[h] for h in range(N_HEADS)], axis=0)

    def pad_rows(y_b, second):
        z = zeros_half[:, :y_b.shape[1]]
        return jnp.concatenate([z, y_b] if second else [y_b, z], axis=0)

    def prepare(ci):
        r0 = ci * c
        rows_c = pl.ds(r0, c)
        dlt = _dot(tri3, _split3(jnp.where(strict, la64_ref[rows_c, :], 0.0)))
        gc = _dot(tri3, _split3(la128_ref[rows_c, :]))
        yield
        decay = jnp.exp(jnp.where(tri, dlt, -jnp.inf))
        eg = jnp.exp(gc)
        kb = kb_ref[rows_c, :]
        k_b = k_ref[rows_c, :]
        q_b = q_ref[rows_c, :]
        k_rows = jnp.concatenate([k_b * head_sel_w[h] for h in range(N_HEADS)], axis=0)
        qk = _dot_nt(jnp.concatenate([kb, q_b], axis=0), k_rows)
        yield
        l_s[ci] = jnp.where(strict, qk[:c] * decay, 0.0).astype(BF16)
        at_s[ci] = jnp.where(tri, qk[c:] * decay, 0.0).astype(BF16)
        kbe = kb.astype(F32) * eg
        qg = (q_b.astype(F32) * eg).astype(BF16)
        gl = gc[c - 1:c, :]
        kd = (k_b.astype(F32) * jnp.exp(gl - gc)).astype(BF16)
        vb = vb_ref[rows_c, :]
        for h in range(N_HEADS):
            hs = slice(h * HEAD_DIM, (h + 1) * HEAD_DIM)
            rhs_s[ci, h, :, 0:HEAD_DIM] = vb[:, hs]
            rhs_s[ci, h, :, HEAD_DIM:2 * HEAD_DIM] = kbe[:, hs].astype(BF16)
            wq_s[ci, h, c:2 * c, :] = qg[:, hs]
            kd_s[ci, h] = kd[:, hs]
        egl_s[ci] = jnp.broadcast_to(jnp.exp(gl), (SUBLANES, GROUP_W))

    def invert(ci):
        lf = l_s[ci].astype(F32)
        n8 = jnp.where(diag8, -lf, 0.0)
        t = eye_f + n8
        p = n8.astype(BF16)
        p = _dot(p, block_diag(p)).astype(BF16)
        yield
        pt = _dot(jnp.concatenate([p, t.astype(BF16)], axis=0), block_diag(p))
        t = t + pt[c:]
        p = pt[:c].astype(BF16)
        yield
        t = t + _dot(t.astype(BF16), block_diag(p))
        yield
        for mask in merge_masks:
            off = jnp.where(mask, lf, 0.0).astype(BF16)
            a = _dot(t.astype(BF16), block_diag(off)).astype(BF16)
            yield
            t = t - _dot(a, block_diag(t.astype(BF16)))
            yield
        tb = t.astype(BF16)
        for h in range(N_HEADS):
            pr = slice((h // 2) * LANES, (h // 2 + 1) * LANES)
            x = _dot(tb[:, pr], pad_rows(rhs_s[ci, h], h % 2))
            u_s[ci, h] = x[:, :HEAD_DIM]
            wq_s[ci, h, 0:c, :] = x[:, HEAD_DIM:].astype(BF16)
        yield

    def recur(cis):
        for ci in cis:
            r0 = ci * c
            at = at_s[ci]
            egl = egl_s[ci]
            sts = [st_ref[h] for h in range(N_HEADS)]
            wss = [_dot(wq_s[ci, h], sts[h].astype(BF16)) for h in range(N_HEADS)]
            yield
            vnbs = [(u_s[ci, h] - wss[h][:c]).astype(BF16) for h in range(N_HEADS)]
            for h in range(N_HEADS):
                hs = slice(h * HEAD_DIM, (h + 1) * HEAD_DIM)
                pr = slice((h // 2) * LANES, (h // 2 + 1) * LANES)
                out = wss[h][c:] + _dot(at[:, pr], pad_rows(vnbs[h], h % 2))
                st_ref[h] = sts[h] * egl[0:1, hs] + _dot_tn(kd_s[ci, h], vnbs[h])
                zh = z_ref[pl.ds(r0, c), hs].astype(F32)
                o_ref[pl.ds(r0, c), hs] = (_rms(out, ng) * (zh * _sigmoid(zh))).astype(BF16)
            yield

    return prepare, invert, recur


def _hgrn_stages(q_ref, f_ref, i_ref, z_ref, lbl_ref, ng_ref, o_ref, st_ref, at_s, qe_s, kd_s, dec_s, layer):
    c = CHUNK
    logits = lbl_ref[...]
    e = jnp.exp(logits - jnp.max(logits, axis=0, keepdims=True))
    p = e / jnp.sum(e, axis=0, keepdims=True)
    lb = jnp.sum(p[0:layer + 1], axis=0, keepdims=True) - p[0:1]
    ng = ng_ref[...]

    tt = lax.broadcasted_iota(jnp.int32, (c, c), 0)
    ss = lax.broadcasted_iota(jnp.int32, (c, c), 1)
    trow = lax.broadcasted_iota(jnp.int32, (c, HEAD_DIM), 0)
    levels = (1, 2, 4, 8, 16, 32)
    pair_masks = [((tt & m) != 0) & ((ss & m) == 0) & ((tt // (2 * m)) == (ss // (2 * m))) for m in levels]
    odd_rows = [(trow & m) != 0 for m in levels]
    eye = tt == ss

    def local(ci, h):
        r0 = ci * c
        slot = ci % 2
        hs = slice(h * HEAD_DIM, (h + 1) * HEAD_DIM)
        fr = f_ref[pl.ds(r0, c), hs].astype(F32)
        rq = q_ref[pl.ds(r0, c), hs].astype(F32)
        lbh = lb[:, hs]
        lf = jnp.log2(lbh + (1.0 - lbh) * _sigmoid(fr))
        kk = (1.0 - lbh) * _sigmoid(-fr)
        qq = rq * _sigmoid(rq)
        attn = jnp.where(eye, _dot_nt(qq.astype(BF16), kk.astype(BF16)), 0.0)
        yield
        cs = lf
        tot = lf
        for m, pmask, odd in zip(levels, pair_masks, odd_rows):
            qm = (qq * jnp.exp2(cs)).astype(BF16)
            km = (kk * jnp.exp2(tot - cs)).astype(BF16)
            attn = jnp.where(pmask, _dot_nt(qm, km), attn)
            if m < SUBLANES:
                prev = pltpu.roll(tot, m, axis=0)
                nxt = pltpu.roll(tot, c - m, axis=0)
                cs = cs + jnp.where(odd, prev, 0.0)
                tot = tot + jnp.where(odd, prev, nxt)
            else:
                cs_parts, tot_parts = [], []
                for b0 in range(0, c, 2 * m):
                    t_even = tot[b0:b0 + m]
                    t_both = t_even + tot[b0 + m:b0 + 2 * m]
                    cs_parts += [cs[b0:b0 + m], cs[b0 + m:b0 + 2 * m] + t_even]
                    tot_parts += [t_both, t_both]
                cs = jnp.concatenate(cs_parts, axis=0)
                tot = jnp.concatenate(tot_parts, axis=0)
            yield
        at_s[slot, h] = attn.astype(BF16)
        qe_s[slot, h] = (qq * jnp.exp2(cs)).astype(BF16)
        kd_s[slot, h] = (kk * jnp.exp2(tot - cs)).astype(BF16)
        dec_s[slot, h] = jnp.exp2(tot[0:SUBLANES, :])

    def tail(ci, h):
        r0 = ci * c
        slot = ci % 2
        hs = slice(h * HEAD_DIM, (h + 1) * HEAD_DIM)
        ii = i_ref[pl.ds(r0, c), hs]
        st = st_ref[h]
        out = _dot_nt(qe_s[slot, h], st.astype(BF16)) + _dot(at_s[slot, h], ii)
        st_ref[h] = st * dec_s[slot, h, 0:1, :] + _dot_tn(ii, kd_s[slot, h])
        yield
        zh = z_ref[pl.ds(r0, c), hs].astype(F32)
        o_ref[pl.ds(r0, c), hs] = (_rms(out, ng) * _sigmoid(zh)).astype(BF16)

    return local, tail


def _round_robin(gens):
    gens = list(gens)
    while gens:
        alive = []
        for g in gens:
            try:
                next(g)
                alive.append(g)
            except StopIteration:
                pass
        gens = alive
        yield


def _chain(gens):
    for g in gens:
        yield from g


def _recurrent_kernel(gq_ref, gk_ref, gkb_ref, gvb_ref, la128_ref, la64_ref, gz_ref, gng_ref,
                      rq_ref, rf_ref, ri_ref, rz_ref, lbl_ref, hng_ref, og_ref, oh_ref,
                      gst_ref, l_s, at_s, rhs_s, wq_s, u_s, kd_s, egl_s,
                      hst_ref, hat_s, hqe_s, hkd_s, hdec_s, *, layer):
    @pl.when(pl.program_id(1) == 0)
    def _():
        gst_ref[...] = jnp.zeros(gst_ref.shape, F32)
        hst_ref[...] = jnp.zeros(hst_ref.shape, F32)

    prepare, invert, recur = _gdn_stages(gq_ref, gk_ref, gkb_ref, gvb_ref, la128_ref, la64_ref, gz_ref, gng_ref,
                                         og_ref, gst_ref, l_s, at_s, rhs_s, wq_s, u_s, kd_s, egl_s)
    local, tail = _hgrn_stages(rq_ref, rf_ref, ri_ref, rz_ref, lbl_ref, hng_ref, oh_ref,
                               hst_ref, hat_s, hqe_s, hkd_s, hdec_s, layer)
    n = gq_ref.shape[0] // CHUNK
    heads = range(N_HEADS)

    def hgrn_step(ci):
        gens = [tail(ci - 1, h) for h in heads] if ci > 0 else []
        gens += [local(ci, h) for h in heads] if ci < n else []
        return _round_robin(gens)

    per_phase = [0] + [2] * (n // 2) + [1]
    first = [sum(per_phase[:k]) for k in range(len(per_phase))]

    for k in range(n // 2 + 2):
        gens = []
        if k < n // 2:
            gens += [prepare(2 * k), prepare(2 * k + 1)]
        if 1 <= k <= n // 2:
            gens += [invert(2 * k - 2), invert(2 * k - 1)]
        if 2 <= k:
            gens += [recur([2 * k - 4, 2 * k - 3])]
        gens += [_chain([hgrn_step(first[k] + i) for i in range(per_phase[k])])]
        for _ in _round_robin(gens):
            pass


def _recurrent(gdn_in, proj, gng, lbl, hng, layer, b, s):
    ts = _seq_tile(s)
    ns = s // ts
    nc = ts // CHUNK
    assert nc % 2 == 0

    def rows(wd):
        return pl.BlockSpec((ts, wd), lambda bi, si: (bi * ns + si, 0))

    def seg(k):
        return pl.BlockSpec((None, ts, SEG_W), lambda bi, si: (k, bi * ns + si, 0))

    head_vec = pl.BlockSpec((1, HEAD_DIM), lambda bi, si: (0, 0))
    n = b * s
    return pl.pallas_call(
        functools.partial(_recurrent_kernel, layer=layer),
        grid=(b, ns),
        in_specs=[rows(GROUP_W)] * 5 + [rows(N_HEADS * CHUNK), seg(SEG_G_Z), head_vec,
                  seg(SEG_R_Q), seg(SEG_R_F), seg(SEG_R_I), seg(SEG_R_Z),
                  pl.BlockSpec(lbl.shape, lambda bi, si: (0, 0)), head_vec],
        out_specs=[rows(GROUP_W), rows(GROUP_W)],
        out_shape=[jax.ShapeDtypeStruct((n, GROUP_W), BF16)] * 2,
        scratch_shapes=[pltpu.VMEM((N_HEADS, HEAD_DIM, HEAD_DIM), F32),
                        pltpu.VMEM((nc, CHUNK, N_HEADS * CHUNK), BF16),
                        pltpu.VMEM((nc, CHUNK, N_HEADS * CHUNK), BF16),
                        pltpu.VMEM((nc, N_HEADS, CHUNK, 2 * HEAD_DIM), BF16),
                        pltpu.VMEM((nc, N_HEADS, 2 * CHUNK, HEAD_DIM), BF16),
                        pltpu.VMEM((nc, N_HEADS, CHUNK, HEAD_DIM), F32),
                        pltpu.VMEM((nc, N_HEADS, CHUNK, HEAD_DIM), BF16),
                        pltpu.VMEM((nc, SUBLANES, GROUP_W), F32),
                        pltpu.VMEM((N_HEADS, HEAD_DIM, HEAD_DIM), F32),
                        pltpu.VMEM((2, N_HEADS, CHUNK, CHUNK), BF16),
                        pltpu.VMEM((2, N_HEADS, CHUNK, HEAD_DIM), BF16),
                        pltpu.VMEM((2, N_HEADS, CHUNK, HEAD_DIM), BF16),
                        pltpu.VMEM((2, N_HEADS, SUBLANES, HEAD_DIM), F32)],
        compiler_params=_params(("parallel", "arbitrary")),
        name="recurrent",
    )(*gdn_in, proj, gng, proj, proj, proj, proj, lbl, hng)


def _mix_mlp_kernel(ya_ref, yb_ref, yc_ref, yd_ref, wo_ref, h_ref, g_ref, w1_ref, w2_ref, fg_ref,
                    o_ref, xn_ref, *, final_norm):
    f = pl.program_id(1)

    @pl.when(f == 0)
    def _():
        y = jnp.concatenate([ya_ref[...], yb_ref[...], yc_ref[...], yd_ref[...]], axis=1)
        h1 = h_ref[...] + _dot(y, wo_ref[...])
        xn_ref[...] = _rms(h1, g_ref[...]).astype(BF16)
        o_ref[...] = h1

    a = jnp.maximum(_dot(xn_ref[...], w1_ref[...]), 0.0)
    o_ref[...] += _dot((a * a).astype(BF16), w2_ref[...])

    if final_norm:
        @pl.when(f == pl.num_programs(1) - 1)
        def _():
            o_ref[...] = _rms(o_ref[...], fg_ref[...])


def _mix_mlp(ys, wo, h, g, w1, w2, fg, layer, final_norm):
    n, d = h.shape
    dff = w1.shape[2]
    tm = min(512, n)
    tf = 1024
    ysp = pl.BlockSpec((tm, GROUP_W), lambda i, f: (i, 0))
    rows = pl.BlockSpec((tm, d), lambda i, f: (i, 0))
    vec = pl.BlockSpec((1, d), lambda i, f: (0, 0))
    return pl.pallas_call(
        functools.partial(_mix_mlp_kernel, final_norm=final_norm),
        grid=(n // tm, dff // tf),
        in_specs=[ysp, ysp, ysp, ysp,
                  pl.BlockSpec((None, d, d), lambda i, f: (layer, 0, 0), pipeline_mode=pl.Buffered(1)),
                  rows, vec,
                  pl.BlockSpec((None, d, tf), lambda i, f: (layer, 0, f)),
                  pl.BlockSpec((None, tf, d), lambda i, f: (layer, f, 0)),
                  vec],
        out_specs=rows,
        out_shape=jax.ShapeDtypeStruct((n, d), F32),
        scratch_shapes=[pltpu.VMEM((tm, d), BF16)],
        compiler_params=_params(("parallel", "arbitrary")),
        name="mix_mlp",
    )(*ys, wo, h, g, w1, w2, fg)


def _pack_w_in(w_in):
    wt = jnp.swapaxes(w_in, 1, 2)
    widths = (GROUP_W, GROUP_W, GROUP_W, MLA_Q_RANK, MLA_KV_RANK, MLA_ROPE,
              GROUP_W, GROUP_W, GROUP_W, GROUP_W, N_HEADS, N_HEADS,
              GROUP_W, GROUP_W, GROUP_W, GROUP_W)
    rows, start = [], 0
    for wd in widths:
        rows.append(wt[:, start:start + wd])
        start += wd
    (sc_x, sc_c, sc_b, m_cq, m_ckv, m_kr, g_q, g_k, g_v, g_z, g_a, g_b, r_q, r_f, r_i, r_z) = rows
    l, _, d = wt.shape

    def zeros(n):
        return jnp.zeros((l, n, d), wt.dtype)

    half = MLA_ROPE // 2
    kr_swapped = jnp.concatenate([m_kr[:, half:], m_kr[:, :half]], axis=1)
    seg_q = [m_cq, g_a, g_b, zeros(SEG_W - MLA_Q_RANK - 2 * N_HEADS)]
    seg_kv = [m_ckv, m_kr, zeros(LANES - MLA_ROPE), kr_swapped, zeros(LANES - MLA_ROPE)]
    segs = [sc_x, sc_c, sc_b, g_q, g_k, g_v, g_z, r_q, r_f, r_i, r_z] + seg_q + seg_kv
    return jnp.concatenate(segs, axis=1).astype(BF16)


def _pack_mla_weights(w_uq, w_ukv):
    l = w_uq.shape[0]
    half = MLA_ROPE // 2
    wq = w_uq.reshape(l, MLA_Q_RANK, N_HEADS, MLA_NOPE + MLA_ROPE)
    nope, rp = wq[..., :MLA_NOPE], wq[..., MLA_NOPE:]
    z64 = jnp.zeros(rp.shape, rp.dtype)
    wqa = jnp.concatenate([nope, rp, z64], axis=-1).reshape(l, MLA_Q_RANK, N_HEADS * QK_PAD)
    rp_sw = jnp.concatenate([rp[..., half:], rp[..., :half]], axis=-1)
    wqb = jnp.concatenate([rp_sw, z64], axis=-1).reshape(l, MLA_Q_RANK, N_HEADS * LANES)
    wkv = w_ukv.reshape(l, MLA_KV_RANK, N_HEADS, MLA_NOPE + HEAD_DIM)
    wk = wkv[..., :MLA_NOPE].reshape(l, MLA_KV_RANK, N_HEADS * MLA_NOPE)
    wv = wkv[..., MLA_NOPE:].reshape(l, MLA_KV_RANK, N_HEADS * HEAD_DIM)
    return wqa.astype(BF16), wqb.astype(BF16), wk.astype(BF16), jnp.swapaxes(wv, 1, 2).astype(BF16)


def _pad_lanes(x):
    l, k = x.shape
    return jnp.concatenate([x, jnp.zeros((l, LANES - k), x.dtype)], axis=-1).reshape(l, 1, LANES)


def kernel(x, positions, norm1_g, w_in, sconv_w, sconv_out_g, mla_q_g, mla_kv_g, mla_w_uq, mla_w_ukv,
           mla_out_g, gdn_conv_w, gdn_a_log, gdn_dt_bias, gdn_norm_g, hgrn_lb_logits, hgrn_norm_g,
           w_o, norm2_g, w_ff1, w_ff2, final_g):
    b, s, d = x.shape
    n = b * s
    depth = w_in.shape[0]

    w_in_p = _pack_w_in(w_in)
    wqa, wqb, wk, wvt = _pack_mla_weights(mla_w_uq, mla_w_ukv)
    w_o_b = w_o.astype(BF16)
    w1_b = w_ff1.astype(BF16)
    w2_b = w_ff2.astype(BF16)
    alog_p = _pad_lanes(gdn_a_log)
    dtb_p = _pad_lanes(gdn_dt_bias)

    half = MLA_ROPE // 2
    inv_freq = ROPE_THETA ** (-jnp.arange(half, dtype=F32) / half)
    invf = jnp.concatenate([inv_freq, inv_freq, jnp.zeros((LANES - MLA_ROPE,), F32)]).reshape(1, LANES)
    cos, sin = _rope_tables(positions.astype(F32).reshape(n, 1), invf)

    h = x.reshape(n, d)
    for l in range(depth):
        proj = _in_proj(h, norm1_g[l].reshape(1, d), w_in_p, l)
        y_sc = _sconv(proj, sconv_w[l], sconv_out_g[l].reshape(1, GROUP_W), b, s)
        q, k, vt = _mla_up(proj, cos, sin, mla_q_g[l].reshape(1, -1), mla_kv_g[l].reshape(1, -1),
                           wqa[l], wqb[l], wk[l], wvt[l], b, s)
        y_mla = _attention(q, k, vt, mla_out_g[l].reshape(1, GROUP_W), b, s)
        gdn_in = _gdn_prep(proj, gdn_conv_w[l], alog_p[l], dtb_p[l], b, s)
        y_gdn, y_hg = _recurrent(gdn_in, proj, gdn_norm_g[l].reshape(1, HEAD_DIM), hgrn_lb_logits,
                                 hgrn_norm_g[l].reshape(1, HEAD_DIM), l, b, s)
        h = _mix_mlp((y_sc, y_mla, y_gdn, y_hg), w_o_b, h, norm2_g[l].reshape(1, d), w1_b, w2_b,
                     final_g.reshape(1, d), l, final_norm=(l == depth - 1))
    return h.reshape(b, s, d)
```

```python
import functools

import jax
import jax.numpy as jnp
from jax import lax
from jax.experimental import pallas as pl
from jax.experimental.pallas import tpu as pltpu

F32 = jnp.float32
BF16 = jnp.bfloat16

GROUP_W = 512
HEAD_DIM = 128
N_HEADS = 4
MLA_Q_RANK = 384
MLA_KV_RANK = 256
MLA_NOPE = 128
MLA_ROPE = 64
ROPE_THETA = 10000.0
CHUNK = 64
EPS = 1e-6
LOG2_E = 1.4426950408889634

SEG_W = 512
(SEG_SC_X, SEG_SC_C, SEG_SC_B, SEG_G_Q, SEG_G_K, SEG_G_V, SEG_G_Z,
 SEG_R_Q, SEG_R_F, SEG_R_I, SEG_R_Z, SEG_MLA_Q, SEG_MLA_KV) = range(13)
N_SEG = 13
QK_PAD = 256
LANES = 128
SUBLANES = 8
VMEM_LIMIT = 56 * 1024 * 1024


def _params(sem, vmem=VMEM_LIMIT):
    return pltpu.CompilerParams(dimension_semantics=sem, vmem_limit_bytes=vmem)


def _dot(a, b):
    return jnp.dot(a, b, preferred_element_type=F32)


def _dot_nt(a, b):
    return lax.dot_general(a, b, (((1,), (1,)), ((), ())), preferred_element_type=F32)


def _dot_tn(a, b):
    return lax.dot_general(a, b, (((0,), (0,)), ((), ())), preferred_element_type=F32)


def _rms(x, g):
    return x * lax.rsqrt(jnp.mean(x * x, axis=-1, keepdims=True) + EPS) * g


def _sigmoid(x):
    return jax.nn.sigmoid(x)


def _inproj_kernel(x_ref, g_ref, w_ref, o_ref, xn_ref):
    @pl.when(pl.program_id(1) == 0)
    def _():
        xn_ref[...] = _rms(x_ref[...], g_ref[...]).astype(BF16)

    o_ref[...] = _dot_nt(xn_ref[...], w_ref[...]).astype(o_ref.dtype)


def _in_proj(h, g, w, layer):
    n, d = h.shape
    tm = min(1024, n)
    return pl.pallas_call(
        _inproj_kernel,
        grid=(n // tm, N_SEG),
        in_specs=[pl.BlockSpec((tm, d), lambda i, j: (i, 0)),
                  pl.BlockSpec((1, d), lambda i, j: (0, 0)),
                  pl.BlockSpec((None, SEG_W, d), lambda i, j: (layer, j, 0))],
        out_specs=pl.BlockSpec((None, tm, SEG_W), lambda i, j: (j, i, 0)),
        out_shape=jax.ShapeDtypeStruct((N_SEG, n, SEG_W), BF16),
        scratch_shapes=[pltpu.VMEM((tm, d), BF16)],
        compiler_params=_params(("parallel", "arbitrary")),
        name="in_proj",
    )(h, g, w)


def _sconv_kernel(x_ref, c_ref, b_ref, w_ref, g_ref, o_ref, zbuf):
    s = pl.program_id(1)
    ts = x_ref.shape[0]

    @pl.when(s == 0)
    def _():
        zbuf[0:SUBLANES, :] = jnp.zeros((SUBLANES, GROUP_W), F32)

    @pl.when(s > 0)
    def _():
        zbuf[0:SUBLANES, :] = zbuf[ts:ts + SUBLANES, :]

    zbuf[SUBLANES:ts + SUBLANES, :] = c_ref[...].astype(F32) * x_ref[...].astype(F32)
    w = w_ref[...]
    y = (w[2:3] * zbuf[SUBLANES:ts + SUBLANES, :]
         + w[1:2] * zbuf[SUBLANES - 1:ts + SUBLANES - 1, :]
         + w[0:1] * zbuf[SUBLANES - 2:ts + SUBLANES - 2, :])
    o_ref[...] = _rms(b_ref[...].astype(F32) * y, g_ref[...]).astype(BF16)


def _seq_tile(s):
    return min(512, s)


def _wide_seq_tile(s):
    return min(1024, s)


def _sconv(proj, w, g, b, s):
    ts = _wide_seq_tile(s)
    ns = s // ts

    def seg(k):
        return pl.BlockSpec((None, ts, SEG_W), lambda bi, si: (k, bi * ns + si, 0))

    return pl.pallas_call(
        _sconv_kernel,
        grid=(b, ns),
        in_specs=[seg(SEG_SC_X), seg(SEG_SC_C), seg(SEG_SC_B),
                  pl.BlockSpec((3, GROUP_W), lambda bi, si: (0, 0)),
                  pl.BlockSpec((1, GROUP_W), lambda bi, si: (0, 0))],
        out_specs=pl.BlockSpec((ts, GROUP_W), lambda bi, si: (bi * ns + si, 0)),
        out_shape=jax.ShapeDtypeStruct((b * s, GROUP_W), BF16),
        scratch_shapes=[pltpu.VMEM((ts + SUBLANES, GROUP_W), F32)],
        compiler_params=_params(("parallel", "arbitrary")),
        name="sconv",
    )(proj, proj, proj, w, g)


def _rope_kernel(pos_ref, invf_ref, cos_ref, sin_ref):
    ang = pos_ref[...] * invf_ref[...]
    lane = lax.broadcasted_iota(jnp.int32, ang.shape, 1)
    half = MLA_ROPE // 2
    cos_ref[...] = jnp.where(lane < MLA_ROPE, jnp.cos(ang), 0.0)
    sn = jnp.sin(ang)
    sin_ref[...] = jnp.where(lane < half, -sn, jnp.where(lane < MLA_ROPE, sn, 0.0))


def _rope_tables(pos_col, invf):
    n = pos_col.shape[0]
    tm = min(2048, n)
    return pl.pallas_call(
        _rope_kernel,
        grid=(n // tm,),
        in_specs=[pl.BlockSpec((tm, 1), lambda i: (i, 0)),
                  pl.BlockSpec((1, LANES), lambda i: (0, 0))],
        out_specs=[pl.BlockSpec((tm, LANES), lambda i: (i, 0))] * 2,
        out_shape=[jax.ShapeDtypeStruct((n, LANES), F32)] * 2,
        compiler_params=_params(("parallel",)),
        name="rope_tables",
    )(pos_col, invf)


def _mla_up_kernel(a_ref, b_ref, cos_ref, sin_ref, gq_ref, gkv_ref,
                   wqa_ref, wqb_ref, wk_ref, wvt_ref, q_ref, k_ref, vt_ref):
    scale = (MLA_NOPE + MLA_ROPE) ** -0.5 * LOG2_E
    cos = cos_ref[...]
    sin = sin_ref[...]
    a = a_ref[...].astype(F32)
    cqn = _rms(a[:, :MLA_Q_RANK], gq_ref[...]).astype(BF16)
    qa = _dot(cqn, wqa_ref[...])
    qb = _dot(cqn, wqb_ref[...])
    for h in range(N_HEADS):
        o = h * QK_PAD
        q_ref[:, o:o + LANES] = (qa[:, o:o + LANES] * scale).astype(BF16)
        roped = qa[:, o + LANES:o + QK_PAD] * cos + qb[:, h * LANES:(h + 1) * LANES] * sin
        q_ref[:, o + LANES:o + QK_PAD] = (roped * scale).astype(BF16)
    bb = b_ref[...].astype(F32)
    ckvn = _rms(bb[:, :MLA_KV_RANK], gkv_ref[...]).astype(BF16)
    kk = _dot(ckvn, wk_ref[...])
    vt_ref[...] = _dot_nt(wvt_ref[...], ckvn).astype(BF16)
    kr = (bb[:, MLA_KV_RANK:MLA_KV_RANK + LANES] * cos
          + bb[:, MLA_KV_RANK + LANES:MLA_KV_RANK + 2 * LANES] * sin).astype(BF16)
    for h in range(N_HEADS):
        o = h * QK_PAD
        k_ref[:, o:o + LANES] = kk[:, h * LANES:(h + 1) * LANES].astype(BF16)
        k_ref[:, o + LANES:o + QK_PAD] = kr


def _mla_up(proj, cos, sin, gq, gkv, wqa, wqb, wk, wvt, b, s):
    n = proj.shape[1]
    tm = _wide_seq_tile(s)
    ns = s // tm

    def seg(k):
        return pl.BlockSpec((None, tm, SEG_W), lambda i: (k, i, 0))

    def full(x):
        return pl.BlockSpec(x.shape, lambda i: (0,) * x.ndim)

    def rows(w):
        return pl.BlockSpec((tm, w), lambda i: (i, 0))

    return pl.pallas_call(
        _mla_up_kernel,
        grid=(n // tm,),
        in_specs=[seg(SEG_MLA_Q), seg(SEG_MLA_KV), rows(LANES), rows(LANES),
                  full(gq), full(gkv), full(wqa), full(wqb), full(wk), full(wvt)],
        out_specs=[rows(N_HEADS * QK_PAD), rows(N_HEADS * QK_PAD),
                   pl.BlockSpec((None, GROUP_W, tm), lambda i: (i // ns, 0, i % ns))],
        out_shape=[jax.ShapeDtypeStruct((n, N_HEADS * QK_PAD), BF16),
                   jax.ShapeDtypeStruct((n, N_HEADS * QK_PAD), BF16),
                   jax.ShapeDtypeStruct((b, GROUP_W, s), BF16)],
        compiler_params=_params(("parallel",)),
        name="mla_up",
    )(proj, proj, cos, sin, gq, gkv, wqa, wqb, wk, wvt)


def _attn_kernel(q_ref, k_ref, vt_ref, g_ref, o_ref, acc_ref):
    i = pl.program_id(1)
    tq = q_ref.shape[0]
    kv_i = lax.broadcasted_iota(jnp.int32, (tq, tq), 0)
    q_i = lax.broadcasted_iota(jnp.int32, (tq, tq), 1)
    causal = kv_i <= q_i
    heads = range(N_HEADS)

    def block(j, carry, masked):
        start = pl.multiple_of(j * tq, tq)
        sc = [_dot_nt(k_ref[pl.ds(start, tq), h * QK_PAD:(h + 1) * QK_PAD],
                      q_ref[:, h * QK_PAD:(h + 1) * QK_PAD]) for h in heads]
        if masked:
            sc = [jnp.where(causal, x, -jnp.inf) for x in sc]
        m_old = carry[:N_HEADS]
        l_old = carry[N_HEADS:]
        m_new = [jnp.maximum(m, jnp.max(x, axis=0, keepdims=True)) for m, x in zip(m_old, sc)]
        alpha = [jnp.exp2(mo - mn) for mo, mn in zip(m_old, m_new)]
        p = [jnp.exp2(x - mn) for x, mn in zip(sc, m_new)]
        l_new = [a * l + jnp.sum(x, axis=0, keepdims=True) for a, l, x in zip(alpha, l_old, p)]
        pv = [_dot(vt_ref[h * HEAD_DIM:(h + 1) * HEAD_DIM, pl.ds(start, tq)], x.astype(BF16))
              for h, x in zip(heads, p)]
        for h in heads:
            acc_ref[h] = alpha[h] * acc_ref[h] + pv[h]
        return tuple(m_new) + tuple(l_new)

    acc_ref[...] = jnp.zeros(acc_ref.shape, F32)
    init = (jnp.full((1, tq), -jnp.inf, F32),) * N_HEADS + (jnp.zeros((1, tq), F32),) * N_HEADS
    carry = lax.fori_loop(0, i, functools.partial(block, masked=False), init)
    carry = block(i, carry, True)
    ot = jnp.concatenate([acc_ref[h] / carry[N_HEADS + h] for h in heads], axis=0)
    ot = ot * lax.rsqrt(jnp.mean(ot * ot, axis=0, keepdims=True) + EPS)
    o_ref[...] = (ot.T * g_ref[...]).astype(BF16)


def _attention(q, k, vt, g, b, s):
    tq = min(512, s)
    nq = s // tq
    q3 = q.reshape(b, s, N_HEADS * QK_PAD)
    k3 = k.reshape(b, s, N_HEADS * QK_PAD)
    out = pl.pallas_call(
        _attn_kernel,
        grid=(b, nq),
        in_specs=[pl.BlockSpec((None, tq, N_HEADS * QK_PAD), lambda bi, i: (bi, i, 0)),
                  pl.BlockSpec((None, s, N_HEADS * QK_PAD), lambda bi, i: (bi, 0, 0)),
                  pl.BlockSpec((None, GROUP_W, s), lambda bi, i: (bi, 0, 0)),
                  pl.BlockSpec((1, GROUP_W), lambda bi, i: (0, 0))],
        out_specs=pl.BlockSpec((None, tq, GROUP_W), lambda bi, i: (bi, i, 0)),
        out_shape=jax.ShapeDtypeStruct((b, s, GROUP_W), BF16),
        scratch_shapes=[pltpu.VMEM((N_HEADS, HEAD_DIM, tq), F32)],
        compiler_params=_params(("parallel", "arbitrary")),
        name="mla_attention",
    )(q3, k3, vt, g)
    return out.reshape(b * s, GROUP_W)


GDN_CONV_K = 4


def _gdn_prep_kernel(q_ref, k_ref, v_ref, a_ref, w_ref, alog_ref, dtb_ref,
                     qo_ref, ko_ref, kbo_ref, vbo_ref, la128_ref, la64_ref, buf):
    s = pl.program_id(1)
    ts = q_ref.shape[0]
    blk = LANES
    w = w_ref[...]
    lane = lax.broadcasted_iota(jnp.int32, (blk, LANES), 1)
    for idx, src in enumerate((q_ref, k_ref, v_ref)):
        @pl.when(s == 0)
        def _(idx=idx):
            buf[idx, 0:SUBLANES, :] = jnp.zeros((SUBLANES, GROUP_W), F32)

        @pl.when(s > 0)
        def _(idx=idx):
            buf[idx, 0:SUBLANES, :] = buf[idx, ts:ts + SUBLANES, :]

        buf[idx, SUBLANES:ts + SUBLANES, :] = src[...].astype(F32)

    for b0 in range(0, ts, blk):
        rows_b = slice(b0, b0 + blk)
        gx = a_ref[rows_b, MLA_Q_RANK:MLA_Q_RANK + LANES].astype(F32)
        xs = gx + dtb_ref[...]
        softplus = jnp.maximum(xs, 0.0) + jnp.log1p(jnp.exp(-jnp.abs(xs)))
        log_a = -jnp.exp(alog_ref[...]) * softplus
        beta = _sigmoid(gx)
        la_b = [jnp.broadcast_to(log_a[:, h:h + 1], (blk, LANES)) for h in range(N_HEADS)]
        for h in range(N_HEADS):
            la128_ref[rows_b, h * HEAD_DIM:(h + 1) * HEAD_DIM] = la_b[h]
        for pr in range(N_HEADS // 2):
            la64_ref[rows_b, pr * LANES:(pr + 1) * LANES] = jnp.where(lane < CHUNK, la_b[2 * pr], la_b[2 * pr + 1])
        beta_c = [beta[:, N_HEADS + h:N_HEADS + h + 1] for h in range(N_HEADS)]
        for idx in range(3):
            wi = w[:, idx * GROUP_W:(idx + 1) * GROUP_W]
            y = wi[3:4] * buf[idx, SUBLANES + b0:SUBLANES + b0 + blk, :]
            for j in range(1, GDN_CONV_K):
                y = y + wi[3 - j:4 - j] * buf[idx, SUBLANES + b0 - j:SUBLANES + b0 + blk - j, :]
            y = y * _sigmoid(y)
            for h in range(N_HEADS):
                hs = slice(h * HEAD_DIM, (h + 1) * HEAD_DIM)
                yh = y[:, hs]
                if idx < 2:
                    yh = yh * lax.rsqrt(jnp.sum(yh * yh, axis=-1, keepdims=True) + EPS)
                if idx == 0:
                    qo_ref[rows_b, hs] = (yh * (HEAD_DIM ** -0.5)).astype(BF16)
                elif idx == 1:
                    ko_ref[rows_b, hs] = yh.astype(BF16)
                    kbo_ref[rows_b, hs] = (yh * beta_c[h]).astype(BF16)
                else:
                    vbo_ref[rows_b, hs] = (yh * beta_c[h]).astype(BF16)


def _gdn_prep(proj, w, alog, dtb, b, s):
    ts = _wide_seq_tile(s)
    ns = s // ts

    def seg(k):
        return pl.BlockSpec((None, ts, SEG_W), lambda bi, si: (k, bi * ns + si, 0))

    def rows(wd):
        return pl.BlockSpec((ts, wd), lambda bi, si: (bi * ns + si, 0))

    def full(x):
        return pl.BlockSpec(x.shape, lambda bi, si: (0,) * x.ndim)

    n = b * s
    return pl.pallas_call(
        _gdn_prep_kernel,
        grid=(b, ns),
        in_specs=[seg(SEG_G_Q), seg(SEG_G_K), seg(SEG_G_V), seg(SEG_MLA_Q),
                  full(w), full(alog), full(dtb)],
        out_specs=[rows(GROUP_W)] * 5 + [rows(N_HEADS * CHUNK)],
        out_shape=[jax.ShapeDtypeStruct((n, GROUP_W), BF16)] * 4 + [jax.ShapeDtypeStruct((n, GROUP_W), F32)]
        + [jax.ShapeDtypeStruct((n, N_HEADS * CHUNK), F32)],
        scratch_shapes=[pltpu.VMEM((3, ts + SUBLANES, GROUP_W), F32)],
        compiler_params=_params(("parallel", "arbitrary")),
        name="gdn_prep",
    )(proj, proj, proj, proj, w, alog, dtb)


def _split3(x):
    hi = x.astype(BF16)
    r1 = x - hi.astype(F32)
    mid = r1.astype(BF16)
    lo = (r1 - mid.astype(F32)).astype(BF16)
    return jnp.concatenate([hi, mid, lo], axis=0)


def _gdn_stages(q_ref, k_ref, kb_ref, vb_ref, la128_ref, la64_ref, z_ref, ng_ref, o_ref,
                st_ref, l_s, at_s, rhs_s, wq_s, u_s, kd_s, egl_s):
    c = CHUNK
    pw = N_HEADS * c
    row = lax.broadcasted_iota(jnp.int32, (c, pw), 0)
    col = lax.broadcasted_iota(jnp.int32, (c, pw), 1) % c
    tri = col <= row
    strict = col < row
    eye_f = (col == row).astype(F32)
    diag8 = (col // SUBLANES) == (row // SUBLANES)
    merge_levels = (8, 16, 32)
    merge_masks = [((row & m) != 0) & ((col & m) == 0) & ((row // (2 * m)) == (col // (2 * m)))
                   for m in merge_levels]
    lane_head_p = lax.broadcasted_iota(jnp.int32, (c, pw), 1) // c
    head_sel_p = [(lane_head_p == h).astype(BF16) for h in range(N_HEADS)]
    lane_head_w = lax.broadcasted_iota(jnp.int32, (c, GROUP_W), 1) // HEAD_DIM
    head_sel_w = [(lane_head_w == h).astype(BF16) for h in range(N_HEADS)]
    row3 = lax.broadcasted_iota(jnp.int32, (c, 3 * c), 0)
    col3 = lax.broadcasted_iota(jnp.int32, (c, 3 * c), 1)
    tri3 = ((col3 % c) <= row3).astype(BF16)
    zeros_half = jnp.zeros((c, 2 * HEAD_DIM), BF16)
    ng = ng_ref[...]

    def block_diag(x_b):
        return jnp.concatenate([x_b * head_sel_p[h] for h in range(N_HEADS)], axis=0)

    def pad_rows(y_b, second):
        z = zeros_half[:, :y_b.shape[1]]
        return jnp.concatenate([z, y_b] if second else [y_b, z], axis=0)

    def prepare(ci):
        r0 = ci * c
        rows_c = pl.ds(r0, c)
        dlt = _dot(tri3, _split3(jnp.where(strict, la64_ref[rows_c, :], 0.0)))
        gc = _dot(tri3, _split3(la128_ref[rows_c, :]))
        yield
        decay = jnp.exp(jnp.where(tri, dlt, -jnp.inf))
        eg = jnp.exp(gc)
        kb = kb_ref[rows_c, :]
        k_b = k_ref[rows_c, :]
        q_b = q_ref[rows_c, :]
        k_rows = jnp.concatenate([k_b * head_sel_w[h] for h in range(N_HEADS)], axis=0)
        qk = _dot_nt(jnp.concatenate([kb, q_b], axis=0), k_rows)
        yield
        l_s[ci] = jnp.where(strict, qk[:c] * decay, 0.0).astype(BF16)
        at_s[ci] = jnp.where(tri, qk[c:] * decay, 0.0).astype(BF16)
        kbe = kb.astype(F32) * eg
        qg = (q_b.astype(F32) * eg).astype(BF16)
        gl = gc[c - 1:c, :]
        kd = (k_b.astype(F32) * jnp.exp(gl - gc)).astype(BF16)
        vb = vb_ref[rows_c, :]
        for h in range(N_HEADS):
            hs = slice(h * HEAD_DIM, (h + 1) * HEAD_DIM)
            rhs_s[ci, h, :, 0:HEAD_DIM] = vb[:, hs]
            rhs_s[ci, h, :, HEAD_DIM:2 * HEAD_DIM] = kbe[:, hs].astype(BF16)
            wq_s[ci, h, c:2 * c, :] = qg[:, hs]
            kd_s[ci, h] = kd[:, hs]
        egl_s[ci] = jnp.broadcast_to(jnp.exp(gl), (SUBLANES, GROUP_W))

    def invert(ci):
        lf = l_s[ci].astype(F32)
        n8 = jnp.where(diag8, -lf, 0.0)
        t = eye_f + n8
        p = n8.astype(BF16)
        p = _dot(p, block_diag(p)).astype(BF16)
        yield
        pt = _dot(jnp.concatenate([p, t.astype(BF16)], axis=0), block_diag(p))
        t = t + pt[c:]
        p = pt[:c].astype(BF16)
        yield
        t = t + _dot(t.astype(BF16), block_diag(p))
        yield
        for mask in merge_masks:
            off = jnp.where(mask, lf, 0.0).astype(BF16)
            a = _dot(t.astype(BF16), block_diag(off)).astype(BF16)
            yield
            t = t - _dot(a, block_diag(t.astype(BF16)))
            yield
        tb = t.astype(BF16)
        for h in range(N_HEADS):
            pr = slice((h // 2) * LANES, (h // 2 + 1) * LANES)
            x = _dot(tb[:, pr], pad_rows(rhs_s[ci, h], h % 2))
            u_s[ci, h] = x[:, :HEAD_DIM]
            wq_s[ci, h, 0:c, :] = x[:, HEAD_DIM:].astype(BF16)
        yield

    def recur(cis):
        for ci in cis:
            r0 = ci * c
            at = at_s[ci]
            egl = egl_s[ci]
            sts = [st_ref[h] for h in range(N_HEADS)]
            wss = [_dot(wq_s[ci, h], sts[h].astype(BF16)) for h in range(N_HEADS)]
            yield
            vnbs = [(u_s[ci, h] - wss[h][:c]).astype(BF16) for h in range(N_HEADS)]
            for h in range(N_HEADS):
                hs = slice(h * HEAD_DIM, (h + 1) * HEAD_DIM)
                pr = slice((h // 2) * LANES, (h // 2 + 1) * LANES)
                out = wss[h][c:] + _dot(at[:, pr], pad_rows(vnbs[h], h % 2))
                st_ref[h] = sts[h] * egl[0:1, hs] + _dot_tn(kd_s[ci, h], vnbs[h])
                zh = z_ref[pl.ds(r0, c), hs].astype(F32)
                o_ref[pl.ds(r0, c), hs] = (_rms(out, ng) * (zh * _sigmoid(zh))).astype(BF16)
            yield

    return prepare, invert, recur


def _hgrn_stages(q_ref, f_ref, i_ref, z_ref, lbl_ref, ng_ref, o_ref, st_ref, at_s, qe_s, kd_s, dec_s, layer):
    c = CHUNK
    logits = lbl_ref[...]
    e = jnp.exp(logits - jnp.max(logits, axis=0, keepdims=True))
    p = e / jnp.sum(e, axis=0, keepdims=True)
    lb = jnp.sum(p[0:layer + 1], axis=0, keepdims=True) - p[0:1]
    ng = ng_ref[...]

    tt = lax.broadcasted_iota(jnp.int32, (c, c), 0)
    ss = lax.broadcasted_iota(jnp.int32, (c, c), 1)
    trow = lax.broadcasted_iota(jnp.int32, (c, HEAD_DIM), 0)
    levels = (1, 2, 4, 8, 16, 32)
    pair_masks = [((tt & m) != 0) & ((ss & m) == 0) & ((tt // (2 * m)) == (ss // (2 * m))) for m in levels]
    odd_rows = [(trow & m) != 0 for m in levels]
    eye = tt == ss

    def local(ci, h):
        r0 = ci * c
        slot = ci % 2
        hs = slice(h * HEAD_DIM, (h + 1) * HEAD_DIM)
        fr = f_ref[pl.ds(r0, c), hs].astype(F32)
        rq = q_ref[pl.ds(r0, c), hs].astype(F32)
        lbh = lb[:, hs]
        lf = jnp.log2(lbh + (1.0 - lbh) * _sigmoid(fr))
        kk = (1.0 - lbh) * _sigmoid(-fr)
        qq = rq * _sigmoid(rq)
        attn = jnp.where(eye, _dot_nt(qq.astype(BF16), kk.astype(BF16)), 0.0)
        yield
        cs = lf
        tot = lf
        for m, pmask, odd in zip(levels, pair_masks, odd_rows):
            qm = (qq * jnp.exp2(cs)).astype(BF16)
            km = (kk * jnp.exp2(tot - cs)).astype(BF16)
            attn = jnp.where(pmask, _dot_nt(qm, km), attn)
            if m < SUBLANES:
                prev = pltpu.roll(tot, m, axis=0)
                nxt = pltpu.roll(tot, c - m, axis=0)
                cs = cs + jnp.where(odd, prev, 0.0)
                tot = tot + jnp.where(odd, prev, nxt)
            else:
                cs_parts, tot_parts = [], []
                for b0 in range(0, c, 2 * m):
                    t_even = tot[b0:b0 + m]
                    t_both = t_even + tot[b0 + m:b0 + 2 * m]
                    cs_parts += [cs[b0:b0 + m], cs[b0 + m:b0 + 2 * m] + t_even]
                    tot_parts += [t_both, t_both]
                cs = jnp.concatenate(cs_parts, axis=0)
                tot = jnp.concatenate(tot_parts, axis=0)
            yield
        at_s[slot, h] = attn.astype(BF16)
        qe_s[slot, h] = (qq * jnp.exp2(cs)).astype(BF16)
        kd_s[slot, h] = (kk * jnp.exp2(tot - cs)).astype(BF16)
        dec_s[slot, h] = jnp.exp2(tot[0:SUBLANES, :])

    def tail(ci, h):
        r0 = ci * c
        slot = ci % 2
        hs = slice(h * HEAD_DIM, (h + 1) * HEAD_DIM)
        ii = i_ref[pl.ds(r0, c), hs]
        st = st_ref[h]
        out = _dot_nt(qe_s[slot, h], st.astype(BF16)) + _dot(at_s[slot, h], ii)
        st_ref[h] = st * dec_s[slot, h, 0:1, :] + _dot_tn(ii, kd_s[slot, h])
        yield
        zh = z_ref[pl.ds(r0, c), hs].astype(F32)
        o_ref[pl.ds(r0, c), hs] = (_rms(out, ng) * _sigmoid(zh)).astype(BF16)

    return local, tail


def _round_robin(gens):
    gens = list(gens)
    while gens:
        alive = []
        for g in gens:
            try:
                next(g)
                alive.append(g)
            except StopIteration:
                pass
        gens = alive
        yield


def _chain(gens):
    for g in gens:
        yield from g


def _recurrent_kernel(gq_ref, gk_ref, gkb_ref, gvb_ref, la128_ref, la64_ref, gz_ref, gng_ref,
                      rq_ref, rf_ref, ri_ref, rz_ref, lbl_ref, hng_ref, og_ref, oh_ref,
                      gst_ref, l_s, at_s, rhs_s, wq_s, u_s, kd_s, egl_s,
                      hst_ref, hat_s, hqe_s, hkd_s, hdec_s, *, layer):
    @pl.when(pl.program_id(1) == 0)
    def _():
        gst_ref[...] = jnp.zeros(gst_ref.shape, F32)
        hst_ref[...] = jnp.zeros(hst_ref.shape, F32)

    prepare, invert, recur = _gdn_stages(gq_ref, gk_ref, gkb_ref, gvb_ref, la128_ref, la64_ref, gz_ref, gng_ref,
                                         og_ref, gst_ref, l_s, at_s, rhs_s, wq_s, u_s, kd_s, egl_s)
    local, tail = _hgrn_stages(rq_ref, rf_ref, ri_ref, rz_ref, lbl_ref, hng_ref, oh_ref,
                               hst_ref, hat_s, hqe_s, hkd_s, hdec_s, layer)
    n = gq_ref.shape[0] // CHUNK
    heads = range(N_HEADS)

    def hgrn_step(ci):
        gens = [tail(ci - 1, h) for h in heads] if ci > 0 else []
        gens += [local(ci, h) for h in heads] if ci < n else []
        return _round_robin(gens)

    per_phase = [0] + [2] * (n // 2) + [1]
    first = [sum(per_phase[:k]) for k in range(len(per_phase))]

    for k in range(n // 2 + 2):
        gens = []
        if k < n // 2:
            gens += [prepare(2 * k), prepare(2 * k + 1)]
        if 1 <= k <= n // 2:
            gens += [invert(2 * k - 2), invert(2 * k - 1)]
        if 2 <= k:
            gens += [recur([2 * k - 4, 2 * k - 3])]
        gens += [_chain([hgrn_step(first[k] + i) for i in range(per_phase[k])])]
        for _ in _round_robin(gens):
            pass


def _recurrent(gdn_in, proj, gng, lbl, hng, layer, b, s):
    ts = _seq_tile(s)
    ns = s // ts
    nc = ts // CHUNK
    assert nc % 2 == 0

    def rows(wd):
        return pl.BlockSpec((ts, wd), lambda bi, si: (bi * ns + si, 0))

    def seg(k):
        return pl.BlockSpec((None, ts, SEG_W), lambda bi, si: (k, bi * ns + si, 0))

    head_vec = pl.BlockSpec((1, HEAD_DIM), lambda bi, si: (0, 0))
    n = b * s
    return pl.pallas_call(
        functools.partial(_recurrent_kernel, layer=layer),
        grid=(b, ns),
        in_specs=[rows(GROUP_W)] * 5 + [rows(N_HEADS * CHUNK), seg(SEG_G_Z), head_vec,
                  seg(SEG_R_Q), seg(SEG_R_F), seg(SEG_R_I), seg(SEG_R_Z),
                  pl.BlockSpec(lbl.shape, lambda bi, si: (0, 0)), head_vec],
        out_specs=[rows(GROUP_W), rows(GROUP_W)],
        out_shape=[jax.ShapeDtypeStruct((n, GROUP_W), BF16)] * 2,
        scratch_shapes=[pltpu.VMEM((N_HEADS, HEAD_DIM, HEAD_DIM), F32),
                        pltpu.VMEM((nc, CHUNK, N_HEADS * CHUNK), BF16),
                        pltpu.VMEM((nc, CHUNK, N_HEADS * CHUNK), BF16),
                        pltpu.VMEM((nc, N_HEADS, CHUNK, 2 * HEAD_DIM), BF16),
                        pltpu.VMEM((nc, N_HEADS, 2 * CHUNK, HEAD_DIM), BF16),
                        pltpu.VMEM((nc, N_HEADS, CHUNK, HEAD_DIM), F32),
                        pltpu.VMEM((nc, N_HEADS, CHUNK, HEAD_DIM), BF16),
                        pltpu.VMEM((nc, SUBLANES, GROUP_W), F32),
                        pltpu.VMEM((N_HEADS, HEAD_DIM, HEAD_DIM), F32),
                        pltpu.VMEM((2, N_HEADS, CHUNK, CHUNK), BF16),
                        pltpu.VMEM((2, N_HEADS, CHUNK, HEAD_DIM), BF16),
                        pltpu.VMEM((2, N_HEADS, CHUNK, HEAD_DIM), BF16),
                        pltpu.VMEM((2, N_HEADS, SUBLANES, HEAD_DIM), F32)],
        compiler_params=_params(("parallel", "arbitrary")),
        name="recurrent",
    )(*gdn_in, proj, gng, proj, proj, proj, proj, lbl, hng)


def _mix_mlp_kernel(ya_ref, yb_ref, yc_ref, yd_ref, wo_ref, h_ref, g_ref, w1_ref, w2_ref, fg_ref,
                    o_ref, xn_ref, *, final_norm):
    f = pl.program_id(1)

    @pl.when(f == 0)
    def _():
        y = jnp.concatenate([ya_ref[...], yb_ref[...], yc_ref[...], yd_ref[...]], axis=1)
        h1 = h_ref[...] + _dot(y, wo_ref[...])
        xn_ref[...] = _rms(h1, g_ref[...]).astype(BF16)
        o_ref[...] = h1

    a = jnp.maximum(_dot(xn_ref[...], w1_ref[...]), 0.0)
    o_ref[...] += _dot((a * a).astype(BF16), w2_ref[...])

    if final_norm:
        @pl.when(f == pl.num_programs(1) - 1)
        def _():
            o_ref[...] = _rms(o_ref[...], fg_ref[...])


def _mix_mlp(ys, wo, h, g, w1, w2, fg, layer, final_norm):
    n, d = h.shape
    dff = w1.shape[2]
    tm = min(512, n)
    tf = 1024
    ysp = pl.BlockSpec((tm, GROUP_W), lambda i, f: (i, 0))
    rows = pl.BlockSpec((tm, d), lambda i, f: (i, 0))
    vec = pl.BlockSpec((1, d), lambda i, f: (0, 0))
    return pl.pallas_call(
        functools.partial(_mix_mlp_kernel, final_norm=final_norm),
        grid=(n // tm, dff // tf),
        in_specs=[ysp, ysp, ysp, ysp,
                  pl.BlockSpec((None, d, d), lambda i, f: (layer, 0, 0), pipeline_mode=pl.Buffered(1)),
                  rows, vec,
                  pl.BlockSpec((None, d, tf), lambda i, f: (layer, 0, f)),
                  pl.BlockSpec((None, tf, d), lambda i, f: (layer, f, 0)),
                  vec],
        out_specs=rows,
        out_shape=jax.ShapeDtypeStruct((n, d), F32),
        scratch_shapes=[pltpu.VMEM((tm, d), BF16)],
        compiler_params=_params(("parallel", "arbitrary")),
        name="mix_mlp",
    )(*ys, wo, h, g, w1, w2, fg)


def _pack_w_in(w_in):
    wt = jnp.swapaxes(w_in, 1, 2)
    widths = (GROUP_W, GROUP_W, GROUP_W, MLA_Q_RANK, MLA_KV_RANK, MLA_ROPE,
              GROUP_W, GROUP_W, GROUP_W, GROUP_W, N_HEADS, N_HEADS,
              GROUP_W, GROUP_W, GROUP_W, GROUP_W)
    rows, start = [], 0
    for wd in widths:
        rows.append(wt[:, start:start + wd])
        start += wd
    (sc_x, sc_c, sc_b, m_cq, m_ckv, m_kr, g_q, g_k, g_v, g_z, g_a, g_b, r_q, r_f, r_i, r_z) = rows
    l, _, d = wt.shape

    def zeros(n):
        return jnp.zeros((l, n, d), wt.dtype)

    half = MLA_ROPE // 2
    kr_swapped = jnp.concatenate([m_kr[:, half:], m_kr[:, :half]], axis=1)
    seg_q = [m_cq, g_a, g_b, zeros(SEG_W - MLA_Q_RANK - 2 * N_HEADS)]
    seg_kv = [m_ckv, m_kr, zeros(LANES - MLA_ROPE), kr_swapped, zeros(LANES - MLA_ROPE)]
    segs = [sc_x, sc_c, sc_b, g_q, g_k, g_v, g_z, r_q, r_f, r_i, r_z] + seg_q + seg_kv
    return jnp.concatenate(segs, axis=1).astype(BF16)


def _pack_mla_weights(w_uq, w_ukv):
    l = w_uq.shape[0]
    half = MLA_ROPE // 2
    wq = w_uq.reshape(l, MLA_Q_RANK, N_HEADS, MLA_NOPE + MLA_ROPE)
    nope, rp = wq[..., :MLA_NOPE], wq[..., MLA_NOPE:]
    z64 = jnp.zeros(rp.shape, rp.dtype)
    wqa = jnp.concatenate([nope, rp, z64], axis=-1).reshape(l, MLA_Q_RANK, N_HEADS * QK_PAD)
    rp_sw = jnp.concatenate([rp[..., half:], rp[..., :half]], axis=-1)
    wqb = jnp.concatenate([rp_sw, z64], axis=-1).reshape(l, MLA_Q_RANK, N_HEADS * LANES)
    wkv = w_ukv.reshape(l, MLA_KV_RANK, N_HEADS, MLA_NOPE + HEAD_DIM)
    wk = wkv[..., :MLA_NOPE].reshape(l, MLA_KV_RANK, N_HEADS * MLA_NOPE)
    wv = wkv[..., MLA_NOPE:].reshape(l, MLA_KV_RANK, N_HEADS * HEAD_DIM)
    return wqa.astype(BF16), wqb.astype(BF16), wk.astype(BF16), jnp.swapaxes(wv, 1, 2).astype(BF16)


def _pad_lanes(x):
    l, k = x.shape
    return jnp.concatenate([x, jnp.zeros((l, LANES - k), x.dtype)], axis=-1).reshape(l, 1, LANES)


def kernel(x, positions, norm1_g, w_in, sconv_w, sconv_out_g, mla_q_g, mla_kv_g, mla_w_uq, mla_w_ukv,
           mla_out_g, gdn_conv_w, gdn_a_log, gdn_dt_bias, gdn_norm_g, hgrn_lb_logits, hgrn_norm_g,
           w_o, norm2_g, w_ff1, w_ff2, final_g):
    b, s, d = x.shape
    n = b * s
    depth = w_in.shape[0]

    w_in_p = _pack_w_in(w_in)
    wqa, wqb, wk, wvt = _pack_mla_weights(mla_w_uq, mla_w_ukv)
    w_o_b = w_o.astype(BF16)
    w1_b = w_ff1.astype(BF16)
    w2_b = w_ff2.astype(BF16)
    alog_p = _pad_lanes(gdn_a_log)
    dtb_p = _pad_lanes(gdn_dt_bias)

    half = MLA_ROPE // 2
    inv_freq = ROPE_THETA ** (-jnp.arange(half, dtype=F32) / half)
    invf = jnp.concatenate([inv_freq, inv_freq, jnp.zeros((LANES - MLA_ROPE,), F32)]).reshape(1, LANES)
    cos, sin = _rope_tables(positions.astype(F32).reshape(n, 1), invf)

    h = x.reshape(n, d)
    for l in range(depth):
        proj = _in_proj(h, norm1_g[l].reshape(1, d), w_in_p, l)
        y_sc = _sconv(proj, sconv_w[l], sconv_out_g[l].reshape(1, GROUP_W), b, s)
        q, k, vt = _mla_up(proj, cos, sin, mla_q_g[l].reshape(1, -1), mla_kv_g[l].reshape(1, -1),
                           wqa[l], wqb[l], wk[l], wvt[l], b, s)
        y_mla = _attention(q, k, vt, mla_out_g[l].reshape(1, GROUP_W), b, s)
        gdn_in = _gdn_prep(proj, gdn_conv_w[l], alog_p[l], dtb_p[l], b, s)
        y_gdn, y_hg = _recurrent(gdn_in, proj, gdn_norm_g[l].reshape(1, HEAD_DIM), hgrn_lb_logits,
                                 hgrn_norm_g[l].reshape(1, HEAD_DIM), l, b, s)
        h = _mix_mlp((y_sc, y_mla, y_gdn, y_hg), w_o_b, h, norm2_g[l].reshape(1, d), w1_b, w2_b,
                     final_g.reshape(1, d), l, final_norm=(l == depth - 1))
    return h.reshape(b, s, d)
```
